```python
import jax
import jax.numpy as jnp
from jax import lax
import numpy as np

D_MODEL = 2048
BATCH = 4
SEQ = 8192
DEPTH = 2

GRID_W = 64
CTX_LEN = 256
HEAD_DIM = 128
MIX_WIDTH = D_MODEL
NA_WIDTH = MIX_WIDTH // 4
POOL_WIDTH = MIX_WIDTH // 4
WA_WIDTH = MIX_WIDTH - NA_WIDTH - POOL_WIDTH
NA_HEADS = NA_WIDTH // HEAD_DIM
NA_WIN_H = 8
NA_WIN_W = 16
POOL_WINDOWS = (2, 4, 8, 16)
POOL_GROUPS = len(POOL_WINDOWS)
POOL_GROUP_DIM = POOL_WIDTH // POOL_GROUPS
WA_Q_HEADS = WA_WIDTH // HEAD_DIM
WA_KV_HEADS = WA_Q_HEADS // 4
WA_KV_WIDTH = WA_KV_HEADS * HEAD_DIM
WA_WINDOW = 128
WA_BLOCK = 128
IN_WIDTH = 3 * NA_WIDTH + POOL_WIDTH + WA_WIDTH + 2 * WA_KV_WIDTH
D_FF = 256 * ((8 * D_MODEL // 3 + 255) // 256)
N_MOD = 9
MACARON_WEIGHT = 0.5
ROPE_BASE = 10000.0
ROPE_AXIS_DIM = HEAD_DIM // 2
RMS_EPS = 1e-6
NEG_INF = -1e30

kernel_name = 'hymba_style_na_pool_wgqa_macaron_dit'


def rms_norm(x, g):
    xf = x.astype(jnp.float32)
    y = xf * lax.rsqrt(jnp.mean(jnp.square(xf), axis=-1, keepdims=True) + RMS_EPS)
    return (y * g.astype(jnp.float32)).astype(x.dtype)


def heads(t, n):
    return t.reshape(t.shape[0], t.shape[1], n, HEAD_DIM)


def half_ffn(x, shift, scale, gate, norm_g, wi, wo):
    h = rms_norm(x, norm_g) * (1 + scale) + shift
    a, b = jnp.split(h @ wi, 2, axis=-1)
    return x + MACARON_WEIGHT * gate * ((jax.nn.silu(a) * b) @ wo)


def split_projection(p):
    sizes = [NA_WIDTH, NA_WIDTH, NA_WIDTH, POOL_WIDTH, WA_WIDTH, WA_KV_WIDTH, WA_KV_WIDTH]
    offsets = [int(o) for o in np.cumsum(sizes)[:-1]]
    return jnp.split(p, offsets, axis=-1)


def axial_rope_tables(n):
    t = jnp.arange(n, dtype=jnp.int32)
    row = (t // GRID_W).astype(jnp.float32)
    col = (t % GRID_W).astype(jnp.float32)
    inv_freq = ROPE_BASE ** (-jnp.arange(0, ROPE_AXIS_DIM, 2, dtype=jnp.float32) / ROPE_AXIS_DIM)
    ang_r = row[:, None] * inv_freq[None, :]
    ang_c = col[:, None] * inv_freq[None, :]
    return (jnp.cos(ang_r), jnp.sin(ang_r), jnp.cos(ang_c), jnp.sin(ang_c))


def rotate(x, cos, sin):
    x1, x2 = jnp.split(x, 2, axis=-1)
    cos = cos[None, :, None, :]
    sin = sin[None, :, None, :]
    return jnp.concatenate([x1 * cos - x2 * sin, x2 * cos + x1 * sin], axis=-1)


def apply_axial_rope(x, tables):
    cos_r, sin_r, cos_c, sin_c = tables
    xr, xc = jnp.split(x.astype(jnp.float32), 2, axis=-1)
    return jnp.concatenate([rotate(xr, cos_r, sin_r), rotate(xc, cos_c, sin_c)], axis=-1).astype(x.dtype)


def context_attention(q, k, v, sink):
    B, C, Hq, d = q.shape
    Hk = k.shape[2]
    G = Hq // Hk
    qg = q.reshape(B, C, Hk, G, d)
    s = jnp.einsum('bqkgd,bckd->bkgqc', qg, k, preferred_element_type=jnp.float32) * (HEAD_DIM ** -0.5)
    if sink is not None:
        s_sink = jnp.broadcast_to(sink.astype(jnp.float32).reshape(1, Hk, G, 1, 1), s.shape[:-1] + (1,))
        s = jnp.concatenate([s, s_sink], axis=-1)
    p = jax.nn.softmax(s, axis=-1)[..., :C].astype(v.dtype)
    return jnp.einsum('bkgqc,bckd->bqkgd', p, v).reshape(B, C, Hq * d)


def neighbourhood_attention(q, k, v, kc, vc, rpb):
    B, S, H, d = q.shape
    rows = S // GRID_W
    kh = min(NA_WIN_H, rows)
    kw = NA_WIN_W
    qg = q.reshape(B, rows, GRID_W, H, d)
    kg = k.reshape(B, rows, GRID_W, H, d)
    vg = v.reshape(B, rows, GRID_W, H, d)
    col = jnp.arange(GRID_W)
    c0 = jnp.clip(col - kw // 2, 0, GRID_W - kw)
    col_idx = c0[:, None] + jnp.arange(kw)[None, :]
    col_off = col_idx - col[:, None] + (NA_WIN_W - 1)
    rpb_f = rpb.astype(jnp.float32)
    scale = HEAD_DIM ** -0.5
    n_nb = kh * kw

    def one_row(r):
        r0 = jnp.clip(r - kh // 2, 0, rows - kh)
        q_r = lax.dynamic_index_in_dim(qg, r, axis=1, keepdims=False)
        k_rows = lax.dynamic_slice_in_dim(kg, r0, kh, axis=1)
        v_rows = lax.dynamic_slice_in_dim(vg, r0, kh, axis=1)
        k_nb = k_rows[:, :, col_idx]
        v_nb = v_rows[:, :, col_idx]
        row_off = r0 + jnp.arange(kh) - r + (NA_WIN_H - 1)
        bias = jnp.take(rpb_f, row_off, axis=1)[:, :, col_off]
        bias = jnp.transpose(bias, (0, 2, 1, 3))
        s_nb = jnp.einsum('bqhd,biqjhd->bhqij', q_r, k_nb, preferred_element_type=jnp.float32) * scale + bias[None]
        s_ctx = jnp.einsum('bqhd,bchd->bhqc', q_r, kc, preferred_element_type=jnp.float32) * scale
        s = jnp.concatenate([s_nb.reshape(B, H, GRID_W, n_nb), s_ctx], axis=-1)
        p = jax.nn.softmax(s, axis=-1).astype(v.dtype)
        p_nb = p[..., :n_nb].reshape(B, H, GRID_W, kh, kw)
        p_ctx = p[..., n_nb:]
        return jnp.einsum('bhqij,biqjhd->bqhd', p_nb, v_nb) + jnp.einsum('bhqc,bchd->bqhd', p_ctx, vc)

    out = lax.map(one_row, jnp.arange(rows))
    return jnp.moveaxis(out, 0, 1).reshape(B, S, H * d)


def window_attention(q, k, v, kc, vc, sink):
    B, S, Hq, d = q.shape
    Hk = k.shape[2]
    G = Hq // Hk
    C = kc.shape[1]
    nblk = S // WA_BLOCK
    nloc = 3 * WA_BLOCK
    qb = q.reshape(B, nblk, WA_BLOCK, Hk, G, d)

    def band(t):
        tp = jnp.pad(t, ((0, 0), (WA_BLOCK, WA_BLOCK), (0, 0), (0, 0))).reshape(B, nblk + 2, WA_BLOCK, Hk, d)
        return jnp.concatenate([tp[:, :-2], tp[:, 1:-1], tp[:, 2:]], axis=2)

    kb = band(k)
    vb = band(v)
    j = jnp.arange(nloc)
    p_idx = jnp.arange(WA_BLOCK)
    n_idx = jnp.arange(nblk)
    in_window = jnp.abs(j[None, :] - WA_BLOCK - p_idx[:, None]) <= WA_WINDOW
    key_pos = n_idx[:, None] * WA_BLOCK + j[None, :] - WA_BLOCK
    in_seq = (key_pos >= 0) & (key_pos < S)
    mask = in_window[None, :, :] & in_seq[:, None, :]
    scale = HEAD_DIM ** -0.5
    s_loc = jnp.einsum('bnqkgd,bnskd->bkgnqs', qb, kb, preferred_element_type=jnp.float32) * scale
    s_loc = jnp.where(mask, s_loc, NEG_INF)
    s_ctx = jnp.einsum('bnqkgd,bckd->bkgnqc', qb, kc, preferred_element_type=jnp.float32) * scale
    s_sink = jnp.broadcast_to(sink.astype(jnp.float32).reshape(1, Hk, G, 1, 1, 1), s_loc.shape[:-1] + (1,))
    prob = jax.nn.softmax(jnp.concatenate([s_loc, s_ctx, s_sink], axis=-1), axis=-1).astype(v.dtype)
    out = (jnp.einsum('bkgnqs,bnskd->bnqkgd', prob[..., :nloc], vb)
           + jnp.einsum('bkgnqc,bckd->bnqkgd', prob[..., nloc:nloc + C], vc))
    return out.reshape(B, S, Hq * d)


def multiscale_pool(u, w, scale):
    B, L, C = u.shape
    uf = u.astype(jnp.float32)
    csum = jnp.concatenate([jnp.zeros((B, 1, C), jnp.float32), jnp.cumsum(uf, axis=1)], axis=1)
    t = jnp.arange(L)
    parts = []
    for g, win in enumerate(POOL_WINDOWS):
        lo = jnp.maximum(t - win // 2, 0)
        hi = jnp.minimum(t + win - win // 2 - 1, L - 1)
        sl = slice(g * POOL_GROUP_DIM, (g + 1) * POOL_GROUP_DIM)
        cg = csum[:, :, sl]
        mean = (cg[:, hi + 1] - cg[:, lo]) / (hi - lo + 1).astype(jnp.float32)[None, :, None]
        parts.append(mean - uf[:, :, sl])
    dlt = jnp.stack(parts, axis=2).astype(u.dtype)
    y = jnp.einsum('blgc,gce->blge', dlt, w) * scale.reshape(POOL_GROUPS, POOL_GROUP_DIM)
    return y.reshape(B, L, C)


def setup_inputs(seed: int = 0) -> dict:
    key = jax.random.key(seed)
    ks = jax.random.split(key, 19)
    nrm = jax.random.normal
    L, D, F = DEPTH, D_MODEL, D_FF
    return {
        'x': nrm(ks[0], (BATCH, SEQ, D), jnp.float32),
        'c': nrm(ks[1], (BATCH, D), jnp.float32),
        'ctx': nrm(ks[2], (BATCH, CTX_LEN, D), jnp.float32),
        'c_ctx': nrm(ks[3], (D,), jnp.float32),
        'w_mod': nrm(ks[4], (L, D, N_MOD * D), jnp.float32) * (0.5 * D ** -0.5),
        'b_mod': nrm(ks[5], (L, N_MOD * D), jnp.float32) * 0.01,
        'norm_w': 1.0 + 0.02 * nrm(ks[6], (L, 3, D), jnp.float32),
        'ffn1_wi': nrm(ks[7], (L, D, 2 * F), jnp.float32) * D ** -0.5,
        'ffn1_wo': nrm(ks[8], (L, F, D), jnp.float32) * F ** -0.5,
        'ffn2_wi': nrm(ks[9], (L, D, 2 * F), jnp.float32) * D ** -0.5,
        'ffn2_wo': nrm(ks[10], (L, F, D), jnp.float32) * F ** -0.5,
        'w_in': nrm(ks[11], (L, D, IN_WIDTH), jnp.float32) * D ** -0.5,
        'w_out': nrm(ks[12], (L, MIX_WIDTH, D), jnp.float32) * MIX_WIDTH ** -0.5,
        'na_qk_gain': 1.0 + 0.02 * nrm(ks[13], (L, 2, HEAD_DIM), jnp.float32),
        'na_rpb': 0.02 * nrm(ks[14], (L, NA_HEADS, 2 * NA_WIN_H - 1, 2 * NA_WIN_W - 1), jnp.float32),
        'pool_w': nrm(ks[15], (L, POOL_GROUPS, POOL_GROUP_DIM, POOL_GROUP_DIM), jnp.float32) * POOL_GROUP_DIM ** -0.5,
        'pool_scale': 1.0 + 0.02 * nrm(ks[16], (L, POOL_WIDTH), jnp.float32),
        'wa_qk_gain': 1.0 + 0.02 * nrm(ks[17], (L, 2, HEAD_DIM), jnp.float32),
        'wa_sink': 0.5 * nrm(ks[18], (L, WA_Q_HEADS), jnp.float32),
    }


def reference(x, c, ctx, c_ctx, w_mod, b_mod, norm_w, ffn1_wi, ffn1_wo, ffn2_wi, ffn2_wo,
              w_in, w_out, na_qk_gain, na_rpb, pool_w, pool_scale, wa_qk_gain, wa_sink):
    S = x.shape[1]
    rope = axial_rope_tables(S)
    for l in range(DEPTH):
        last = l == DEPTH - 1
        mx = jnp.split((jax.nn.silu(c) @ w_mod[l] + b_mod[l])[:, None, :], N_MOD, axis=-1)
        mc = jnp.split((jax.nn.silu(c_ctx) @ w_mod[l] + b_mod[l])[None, None, :], N_MOD, axis=-1)

        x = half_ffn(x, mx[0], mx[1], mx[2], norm_w[l, 0], ffn1_wi[l], ffn1_wo[l])
        ctx = half_ffn(ctx, mc[0], mc[1], mc[2], norm_w[l, 0], ffn1_wi[l], ffn1_wo[l])

        hx = rms_norm(x, norm_w[l, 1]) * (1 + mx[4]) + mx[3]
        hc = rms_norm(ctx, norm_w[l, 1]) * (1 + mc[4]) + mc[3]
        nq, nk, nv, u, wq, wk, wv = split_projection(hx @ w_in[l])
        cnq, cnk, cnv, cu, cwq, cwk, cwv = split_projection(hc @ w_in[l])

        na_kc = rms_norm(heads(cnk, NA_HEADS), na_qk_gain[l, 1])
        na_vc = heads(cnv, NA_HEADS)
        wa_kc = rms_norm(heads(cwk, WA_KV_HEADS), wa_qk_gain[l, 1])
        wa_vc = heads(cwv, WA_KV_HEADS)

        a_out = neighbourhood_attention(rms_norm(heads(nq, NA_HEADS), na_qk_gain[l, 0]),
                                        rms_norm(heads(nk, NA_HEADS), na_qk_gain[l, 1]),
                                        heads(nv, NA_HEADS), na_kc, na_vc, na_rpb[l])
        b_out = multiscale_pool(u, pool_w[l], pool_scale[l])
        q_wa = apply_axial_rope(rms_norm(heads(wq, WA_Q_HEADS), wa_qk_gain[l, 0]), rope)
        k_wa = apply_axial_rope(rms_norm(heads(wk, WA_KV_HEADS), wa_qk_gain[l, 1]), rope)
        c_out = window_attention(q_wa, k_wa, heads(wv, WA_KV_HEADS), wa_kc, wa_vc, wa_sink[l])
        x = x + mx[5] * (jnp.concatenate([a_out, b_out, c_out], axis=-1) @ w_out[l])

        if not last:
            ca = context_attention(rms_norm(heads(cnq, NA_HEADS), na_qk_gain[l, 0]), na_kc, na_vc, None)
            cb = multiscale_pool(cu, pool_w[l], pool_scale[l])
            cc = context_attention(rms_norm(heads(cwq, WA_Q_HEADS), wa_qk_gain[l, 0]), wa_kc, wa_vc, wa_sink[l])
            ctx = ctx + mc[5] * (jnp.concatenate([ca, cb, cc], axis=-1) @ w_out[l])
            ctx = half_ffn(ctx, mc[6], mc[7], mc[8], norm_w[l, 2], ffn2_wi[l], ffn2_wo[l])

        x = half_ffn(x, mx[6], mx[7], mx[8], norm_w[l, 2], ffn2_wi[l], ffn2_wo[l])
    return x
```

```python
import functools

import numpy as np
import jax
import jax.numpy as jnp
from jax import lax
from jax.experimental import pallas as pl
from jax.experimental.pallas import tpu as pltpu

HEAD_DIM = 128
GRID_W = 64
NA_WIN_H = 8
NA_WIN_W = 16
POOL_WINDOWS = (2, 4, 8, 16)
WA_BLOCK = 128
N_MOD = 9
MACARON_WEIGHT = 0.5
ROPE_BASE = 10000.0
RMS_EPS = 1e-6
NEG_INF = -1e30
ATTN_SCALE = HEAD_DIM ** -0.5

V7X_VMEM_BYTES = 64 * 1024 * 1024
V7X_VMEM_BUDGET = 56 * 1024 * 1024
SUBLANES = 8
POOL_HALO = 16

BF16 = jnp.bfloat16
F32 = jnp.float32


def _params(semantics, vmem_bytes):
    return pltpu.CompilerParams(dimension_semantics=semantics,
                                vmem_limit_bytes=int(min(max(vmem_bytes, 16 << 20), V7X_VMEM_BUDGET)))


def _row_tile(n_rows, want):
    t = min(want, n_rows)
    assert n_rows % t == 0, (n_rows, t)
    return t


def _mod_norm(x, mod_ref, g_ref, first):
    y = x * lax.rsqrt(jnp.mean(x * x, axis=-1, keepdims=True) + RMS_EPS) * g_ref[...]
    return y * (1.0 + mod_ref[first + 1:first + 2, :]) + mod_ref[first:first + 1, :]


def _mod_kernel(c_ref, w_ref, b_ref, o_ref):
    c = c_ref[...]
    s = (c * jax.nn.sigmoid(c)).astype(BF16)
    o_ref[...] = jnp.dot(s, w_ref[...].astype(BF16), preferred_element_type=F32) + b_ref[...]


def _modulation(cvecs, w_mod, b_mod):
    n_layers, d, nd = w_mod.shape
    rows = cvecs.shape[0]
    tn = _row_tile(nd, 2048)
    return pl.pallas_call(
        _mod_kernel,
        grid=(n_layers, nd // tn),
        in_specs=[pl.BlockSpec((rows, d), lambda l, n: (0, 0)),
                  pl.BlockSpec((None, d, tn), lambda l, n: (l, 0, n)),
                  pl.BlockSpec((None, 1, tn), lambda l, n: (l, 0, n))],
        out_specs=pl.BlockSpec((None, rows, tn), lambda l, n: (l, 0, n)),
        out_shape=jax.ShapeDtypeStruct((n_layers, rows, nd), F32),
        compiler_params=_params(("parallel", "parallel"), 2 * d * tn * 4 + (8 << 20)),
        name="adaln_modulation",
    )(cvecs, w_mod, b_mod.reshape(n_layers, 1, nd))


def _ffn_kernel(x_ref, mod_ref, g_ref, wa_ref, wb_ref, wo_ref, o_ref, h_ref, *, first):
    f = pl.program_id(1)

    @pl.when(f == 0)
    def _():
        h_ref[...] = _mod_norm(x_ref[...], mod_ref, g_ref, first).astype(BF16)
        o_ref[...] = jnp.zeros(o_ref.shape, o_ref.dtype)

    h = h_ref[...]
    a = jnp.dot(h, wa_ref[...], preferred_element_type=F32)
    b = jnp.dot(h, wb_ref[...], preferred_element_type=F32)
    act = (a * jax.nn.sigmoid(a) * b).astype(BF16)
    o_ref[...] += jnp.dot(act, wo_ref[...], preferred_element_type=F32)

    @pl.when(f == pl.num_programs(1) - 1)
    def _():
        gate = MACARON_WEIGHT * mod_ref[first + 2:first + 3, :]
        o_ref[...] = x_ref[...] + gate * o_ref[...]


def _ffn(x2, mod, first, norm_g, wi, wo, *, tm=512, tf=512):
    n_tok, d = x2.shape
    d_ff = wo.shape[0]
    tm = _row_tile(n_tok, tm)
    tf = _row_tile(d_ff, tf)
    tiles_per_mod = n_tok // mod.shape[0] // tm
    nf = d_ff // tf
    vmem = 4 * tm * d * 4 + tm * d * 2 + 2 * 3 * d * tf * 2 + 6 * tm * tf * 4 + (4 << 20)
    return pl.pallas_call(
        functools.partial(_ffn_kernel, first=first),
        grid=(n_tok // tm, nf),
        in_specs=[pl.BlockSpec((tm, d), lambda i, f: (i, 0)),
                  pl.BlockSpec((None, N_MOD, d), lambda i, f: (i // tiles_per_mod, 0, 0)),
                  pl.BlockSpec((1, d), lambda i, f: (0, 0)),
                  pl.BlockSpec((d, tf), lambda i, f: (0, f)),
                  pl.BlockSpec((d, tf), lambda i, f: (0, nf + f)),
                  pl.BlockSpec((tf, d), lambda i, f: (f, 0))],
        out_specs=pl.BlockSpec((tm, d), lambda i, f: (i, 0)),
        out_shape=jax.ShapeDtypeStruct((n_tok, d), F32),
        scratch_shapes=[pltpu.VMEM((tm, d), BF16)],
        compiler_params=_params(("parallel", "arbitrary"), vmem),
        name="macaron_ffn",
    )(x2, mod, norm_g.reshape(1, d), wi, wi, wo)


def _head_norm(p, gain):
    return p * lax.rsqrt(jnp.mean(p * p, axis=-1, keepdims=True) + RMS_EPS) * gain


def _rotate(y, cos, sin_signed):
    lane = lax.broadcasted_iota(jnp.int32, y.shape, 1)
    partner = jnp.where((lane & 32) == 0, pltpu.roll(y, HEAD_DIM - 32, 1), pltpu.roll(y, 32, 1))
    return y * cos + partner * sin_signed


def _proj_kernel(*refs, use_rope, widths):
    if use_rope:
        x_ref, mod_ref, g_ref, w_ref, nag_ref, wag_ref, cos_ref, sin_ref = refs[:8]
        outs = refs[8:]
    else:
        x_ref, mod_ref, g_ref, w_ref, nag_ref, wag_ref = refs[:6]
        outs = refs[6:]
    nq_ref, nk_ref, nv_ref, u_ref, wq_ref, wk_ref, wv_ref = outs
    na_w, pool_w, wa_w, kv_w = widths
    h = _mod_norm(x_ref[...], mod_ref, g_ref, 3).astype(BF16)

    def chunk(col0, width):
        return jnp.dot(h, w_ref[:, col0:col0 + width], preferred_element_type=F32)

    def normed(p, gain, out_ref, out_col0, scale, rope):
        for j in range(p.shape[1] // HEAD_DIM):
            y = _head_norm(p[:, j * HEAD_DIM:(j + 1) * HEAD_DIM], gain)
            if rope:
                y = _rotate(y, cos_ref[...], sin_ref[...])
            if scale != 1.0:
                y = y * scale
            c0 = out_col0 + j * HEAD_DIM
            out_ref[:, c0:c0 + HEAD_DIM] = y.astype(out_ref.dtype)

    col = 0
    normed(chunk(col, na_w), nag_ref[0:1, :], nq_ref, 0, ATTN_SCALE, False)
    col += na_w
    normed(chunk(col, na_w), nag_ref[1:2, :], nk_ref, 0, 1.0, False)
    col += na_w
    nv_ref[...] = chunk(col, na_w).astype(BF16)
    col += na_w
    u_ref[...] = chunk(col, pool_w)
    col += pool_w
    half = wa_w // 2
    normed(chunk(col, half), wag_ref[0:1, :], wq_ref, 0, ATTN_SCALE, use_rope)
    normed(chunk(col + half, half), wag_ref[0:1, :], wq_ref, half, ATTN_SCALE, use_rope)
    col += wa_w
    normed(chunk(col, kv_w), wag_ref[1:2, :], wk_ref, 0, 1.0, use_rope)
    col += kv_w
    wv_ref[...] = chunk(col, kv_w).astype(BF16)


def _projection(x2, mod, norm_g, w_in, na_gain, wa_gain, rope, widths, *, tm=512):
    n_tok, d = x2.shape
    na_w, pool_w, wa_w, kv_w = widths
    n_in = w_in.shape[1]
    tm = _row_tile(n_tok, tm)
    tiles_per_mod = n_tok // mod.shape[0] // tm
    use_rope = rope is not None
    in_specs = [pl.BlockSpec((tm, d), lambda i: (i, 0)),
                pl.BlockSpec((None, N_MOD, d), lambda i: (i // tiles_per_mod, 0, 0)),
                pl.BlockSpec((1, d), lambda i: (0, 0)),
                pl.BlockSpec((d, n_in), lambda i: (0, 0)),
                pl.BlockSpec((2, HEAD_DIM), lambda i: (0, 0)),
                pl.BlockSpec((2, HEAD_DIM), lambda i: (0, 0))]
    args = [x2, mod, norm_g.reshape(1, d), w_in, na_gain, wa_gain]
    if use_rope:
        tiles_per_seq = rope[0].shape[0] // tm
        in_specs += [pl.BlockSpec((tm, HEAD_DIM), lambda i: (i % tiles_per_seq, 0))] * 2
        args += list(rope)
    out_widths = (na_w, na_w, na_w, pool_w, wa_w, kv_w, kv_w)
    out_dtypes = (BF16, BF16, BF16, F32, BF16, BF16, BF16)
    vmem = 2 * tm * d * 4 + tm * d * 2 + 2 * d * n_in * 2 + 4 * tm * n_in * 4 + (4 << 20)
    return pl.pallas_call(
        functools.partial(_proj_kernel, use_rope=use_rope, widths=widths),
        grid=(n_tok // tm,),
        in_specs=in_specs,
        out_specs=[pl.BlockSpec((tm, w), lambda i: (i, 0)) for w in out_widths],
        out_shape=[jax.ShapeDtypeStruct((n_tok, w), dt) for w, dt in zip(out_widths, out_dtypes)],
        compiler_params=_params(("parallel",), vmem),
        name="input_projection",
    )(*args)


def _softmax_pv(scores, values, extra_logit=None):
    m = functools.reduce(jnp.maximum, [jnp.max(s, axis=-1, keepdims=True) for s in scores])
    if extra_logit is not None:
        m = jnp.maximum(m, extra_logit)
    ps = [jnp.exp(s - m) for s in scores]
    denom = functools.reduce(jnp.add, [jnp.sum(p, axis=-1, keepdims=True) for p in ps])
    if extra_logit is not None:
        denom = denom + jnp.exp(extra_logit - m)
    acc = functools.reduce(jnp.add, [jnp.dot(p.astype(BF16), v, preferred_element_type=F32)
                                     for p, v in zip(ps, values)])
    return acc / denom


def _qk(q, k):
    return lax.dot_general(q, k, (((1,), (1,)), ((), ())), preferred_element_type=F32)


def _na_kernel(q_ref, kp_ref, kc_ref, kn_ref, vp_ref, vc_ref, vn_ref, kx_ref, vx_ref, bias_ref,
               o_ref, kcat_ref, vcat_ref, *, n_rows, n_heads):
    j = pl.program_id(1)
    blk = NA_WIN_H * GRID_W
    for t, (kr, vr) in enumerate(((kp_ref, vp_ref), (kc_ref, vc_ref), (kn_ref, vn_ref))):
        kcat_ref[t * blk:(t + 1) * blk, :] = kr[...]
        vcat_ref[t * blk:(t + 1) * blk, :] = vr[...]

    def row_body(i, carry):
        r = j * NA_WIN_H + i
        r0 = jnp.clip(r - NA_WIN_H // 2, 0, n_rows - NA_WIN_H)
        off = pl.multiple_of((r0 - (j - 1) * NA_WIN_H) * GRID_W, GRID_W)
        case = r - r0
        q_off = pl.multiple_of(i * GRID_W, GRID_W)
        q_row = q_ref[pl.ds(q_off, GRID_W), :]
        k_win = kcat_ref[pl.ds(off, blk), :]
        v_win = vcat_ref[pl.ds(off, blk), :]
        outs = []
        for h in range(n_heads):
            hs = slice(h * HEAD_DIM, (h + 1) * HEAD_DIM)
            qh = q_row[:, hs]
            s_nb = _qk(qh, k_win[:, hs]) + bias_ref[h, case]
            s_cx = _qk(qh, kx_ref[:, hs])
            outs.append(_softmax_pv([s_nb, s_cx], [v_win[:, hs], vx_ref[:, hs]]))
        o_ref[pl.ds(q_off, GRID_W), :] = jnp.concatenate(outs, axis=1).astype(o_ref.dtype)
        return carry

    lax.fori_loop(0, NA_WIN_H, row_body, 0)


def _na_bias_table(rpb):
    case = np.arange(NA_WIN_H)[:, None]
    row_off = np.arange(NA_WIN_H)[None, :] - case + (NA_WIN_H - 1)
    qc = np.arange(GRID_W)[:, None]
    kc = np.arange(GRID_W)[None, :]
    c0 = np.clip(qc - NA_WIN_W // 2, 0, GRID_W - NA_WIN_W)
    valid = (kc >= c0) & (kc < c0 + NA_WIN_W)
    col_off = np.clip(kc - qc + (NA_WIN_W - 1), 0, 2 * NA_WIN_W - 2)
    tbl = rpb.astype(F32)[:, row_off[:, None, :, None], col_off[None, :, None, :]]
    tbl = jnp.where(valid[None, None, :, None, :], tbl, NEG_INF)
    return tbl.reshape(rpb.shape[0], NA_WIN_H, GRID_W, NA_WIN_H * GRID_W)


def _neighbourhood_attention(q, k, v, kx, vx, bias):
    b, s, w = q.shape
    n_heads = w // HEAD_DIM
    n_rows = s // GRID_W
    blk = NA_WIN_H * GRID_W
    nj = s // blk
    cx = kx.shape[1]
    cur = lambda bb, j: (bb, j, 0)
    prev = lambda bb, j: (bb, jnp.maximum(j - 1, 0), 0)
    nxt = lambda bb, j: (bb, jnp.minimum(j + 1, nj - 1), 0)
    tok = lambda im: pl.BlockSpec((None, blk, w), im)
    ctx = pl.BlockSpec((None, cx, w), lambda bb, j: (bb, 0, 0))
    vmem = 2 * 8 * blk * w * 2 + 2 * 2 * cx * w * 2 + 2 * bias.size * 4 + 2 * 3 * blk * w * 2 + (8 << 20)
    return pl.pallas_call(
        functools.partial(_na_kernel, n_rows=n_rows, n_heads=n_heads),
        grid=(b, nj),
        in_specs=[tok(cur), tok(prev), tok(cur), tok(nxt), tok(prev), tok(cur), tok(nxt), ctx, ctx,
                  pl.BlockSpec(bias.shape, lambda bb, j: (0, 0, 0, 0))],
        out_specs=tok(cur),
        out_shape=jax.ShapeDtypeStruct((b, s, w), BF16),
        scratch_shapes=[pltpu.VMEM((3 * blk, w), BF16), pltpu.VMEM((3 * blk, w), BF16)],
        compiler_params=_params(("parallel", "parallel"), vmem),
        name="neighbourhood_attention",
    )(q, k, k, k, v, v, v, kx, vx, bias)


def _wa_kernel(sink_ref, q_ref, kp_ref, kc_ref, kn_ref, vp_ref, vc_ref, vn_ref, kx_ref, vx_ref, o_ref,
               *, group):
    j = pl.program_id(1)
    nblk = pl.num_programs(1)
    blk = WA_BLOCK
    n_kv = kc_ref.shape[1] // HEAD_DIM
    cx = kx_ref.shape[0]
    rows = group * blk
    n_keys = 3 * blk + cx
    p_idx = lax.broadcasted_iota(jnp.int32, (rows, n_keys), 0) & (blk - 1)
    k_idx = lax.broadcasted_iota(jnp.int32, (rows, n_keys), 1)
    has_prev = (j > 0).astype(jnp.int32)
    has_next = (j < nblk - 1).astype(jnp.int32)
    lo = p_idx * has_prev + blk * (1 - has_prev)
    hi = (p_idx + 2 * blk) * has_next + (2 * blk - 1) * (1 - has_next)
    valid = ((k_idx >= lo) & (k_idx <= hi)) | (k_idx >= 3 * blk)
    for kv in range(n_kv):
        hs = slice(kv * HEAD_DIM, (kv + 1) * HEAD_DIM)
        heads = [kv * group + g for g in range(group)]
        q = jnp.concatenate([q_ref[:, h * HEAD_DIM:(h + 1) * HEAD_DIM] for h in heads], axis=0)
        k = jnp.concatenate([kp_ref[:, hs], kc_ref[:, hs], kn_ref[:, hs], kx_ref[:, hs]], axis=0)
        v = jnp.concatenate([vp_ref[:, hs], vc_ref[:, hs], vn_ref[:, hs], vx_ref[:, hs]], axis=0)
        s = jnp.where(valid, _qk(q, k), NEG_INF)
        sink = jnp.concatenate([jnp.full((blk, 1), sink_ref[h], F32) for h in heads], axis=0)
        out = _softmax_pv([s], [v], extra_logit=sink)
        for g, h in enumerate(heads):
            o_ref[:, h * HEAD_DIM:(h + 1) * HEAD_DIM] = out[g * blk:(g + 1) * blk, :].astype(o_ref.dtype)


def _window_attention(q, k, v, kx, vx, sink):
    b, s, wq = q.shape
    wk = k.shape[2]
    group = wq // wk
    nblk = s // WA_BLOCK
    cx = kx.shape[1]
    cur = lambda bb, j: (bb, j, 0)
    prev = lambda bb, j: (bb, jnp.maximum(j - 1, 0), 0)
    nxt = lambda bb, j: (bb, jnp.minimum(j + 1, nblk - 1), 0)
    kvs = lambda im: pl.BlockSpec((None, WA_BLOCK, wk), im)
    ctx = pl.BlockSpec((None, cx, wk), lambda bb, j: (bb, 0, 0))
    return pl.pallas_call(
        functools.partial(_wa_kernel, group=group),
        grid=(b, nblk),
        in_specs=[pl.BlockSpec(memory_space=pltpu.SMEM),
                  pl.BlockSpec((None, WA_BLOCK, wq), cur),
                  kvs(prev), kvs(cur), kvs(nxt), kvs(prev), kvs(cur), kvs(nxt), ctx, ctx],
        out_specs=pl.BlockSpec((None, WA_BLOCK, wq), cur),
        out_shape=jax.ShapeDtypeStruct((b, s, wq), BF16),
        compiler_params=_params(("parallel", "parallel"), 24 << 20),
        name="window_attention",
    )(sink, q, k, k, k, v, v, v, kx, vx)


def _ctx_attn_kernel(*refs, group, use_sink):
    if use_sink:
        sink_ref, q_ref, k_ref, v_ref, o_ref = refs
    else:
        q_ref, k_ref, v_ref, o_ref = refs
    c = q_ref.shape[0]
    for kv in range(k_ref.shape[1] // HEAD_DIM):
        hs = slice(kv * HEAD_DIM, (kv + 1) * HEAD_DIM)
        heads = [kv * group + g for g in range(group)]
        q = jnp.concatenate([q_ref[:, h * HEAD_DIM:(h + 1) * HEAD_DIM] for h in heads], axis=0)
        sink = None
        if use_sink:
            sink = jnp.concatenate([jnp.full((c, 1), sink_ref[h], F32) for h in heads], axis=0)
        out = _softmax_pv([_qk(q, k_ref[:, hs])], [v_ref[:, hs]], extra_logit=sink)
        for g, h in enumerate(heads):
            o_ref[:, h * HEAD_DIM:(h + 1) * HEAD_DIM] = out[g * c:(g + 1) * c, :].astype(o_ref.dtype)


def _context_attention(q, k, v, sink):
    b, c, wq = q.shape
    wk = k.shape[2]
    use_sink = sink is not None
    spec = lambda w: pl.BlockSpec((None, c, w), lambda bb: (bb, 0, 0))
    in_specs = [spec(wq), spec(wk), spec(wk)]
    args = [q, k, v]
    if use_sink:
        in_specs = [pl.BlockSpec(memory_space=pltpu.SMEM)] + in_specs
        args = [sink] + args
    return pl.pallas_call(
        functools.partial(_ctx_attn_kernel, group=wq // wk, use_sink=use_sink),
        grid=(b,),
        in_specs=in_specs,
        out_specs=spec(wq),
        out_shape=jax.ShapeDtypeStruct((b, c, wq), BF16),
        compiler_params=_params(("parallel",), 24 << 20),
        name="context_attention",
    )(*args)


def _out_kernel(x_ref, mod_ref, a_ref, u_ref, up_ref, un_ref, c_ref, pw_ref, ps_ref, wo_ref, o_ref, ext_ref,
                *, seq_len):
    tm = x_ref.shape[0]
    tiles_per_seq = seq_len // tm
    t_in_seq = pl.program_id(0) % tiles_per_seq
    halo = POOL_HALO
    has_prev = (t_in_seq > 0).astype(F32)
    has_next = (t_in_seq < tiles_per_seq - 1).astype(F32)
    ext_ref[0:halo, :] = up_ref[...] * has_prev
    ext_ref[halo:halo + tm, :] = u_ref[...]
    ext_ref[halo + tm:, :] = un_ref[...] * has_next

    pos = t_in_seq * tm + lax.broadcasted_iota(jnp.int32, (tm, 1), 0)
    parts = []
    for g, win in enumerate(POOL_WINDOWS):
        ls = slice(g * HEAD_DIM, (g + 1) * HEAD_DIM)
        total = functools.reduce(jnp.add, [ext_ref[halo + d:halo + d + tm, ls]
                                           for d in range(-(win // 2), win - win // 2)])
        lo = jnp.maximum(pos - win // 2, 0)
        hi = jnp.minimum(pos + (win - win // 2 - 1), seq_len - 1)
        delta = total / (hi - lo + 1).astype(F32) - u_ref[:, ls]
        y = jnp.dot(delta.astype(BF16), pw_ref[g], preferred_element_type=F32) * ps_ref[:, ls]
        parts.append(y.astype(BF16))
    na_w = a_ref.shape[1]
    pool_w = len(POOL_WINDOWS) * HEAD_DIM
    mixed = jnp.dot(a_ref[...], wo_ref[0:na_w, :], preferred_element_type=F32)
    mixed += jnp.dot(jnp.concatenate(parts, axis=1), wo_ref[na_w:na_w + pool_w, :], preferred_element_type=F32)
    mixed += jnp.dot(c_ref[...], wo_ref[na_w + pool_w:, :], preferred_element_type=F32)
    o_ref[...] = x_ref[...] + mod_ref[5:6, :] * mixed


def _mix_out(x2, mod, a2, u2, c2, pool_w, pool_scale, w_out, seq_len, *, tm=512):
    n_tok, d = x2.shape
    tm = _row_tile(seq_len, tm)
    tiles_per_mod = n_tok // mod.shape[0] // tm
    n_halo = n_tok // POOL_HALO
    hpt = tm // POOL_HALO
    row = lambda w: pl.BlockSpec((tm, w), lambda i: (i, 0))
    vmem = 4 * tm * d * 4 + 2 * w_out.size * 2 + 4 * tm * d * 4 + (4 << 20)
    return pl.pallas_call(
        functools.partial(_out_kernel, seq_len=seq_len),
        grid=(n_tok // tm,),
        in_specs=[row(d),
                  pl.BlockSpec((None, N_MOD, d), lambda i: (i // tiles_per_mod, 0, 0)),
                  row(a2.shape[1]), row(u2.shape[1]),
                  pl.BlockSpec((POOL_HALO, u2.shape[1]), lambda i: (jnp.maximum(i * hpt - 1, 0), 0)),
                  pl.BlockSpec((POOL_HALO, u2.shape[1]), lambda i: (jnp.minimum((i + 1) * hpt, n_halo - 1), 0)),
                  row(c2.shape[1]),
                  pl.BlockSpec(pool_w.shape, lambda i: (0, 0, 0)),
                  pl.BlockSpec((1, u2.shape[1]), lambda i: (0, 0)),
                  pl.BlockSpec(w_out.shape, lambda i: (0, 0))],
        out_specs=row(d),
        out_shape=jax.ShapeDtypeStruct((n_tok, d), F32),
        scratch_shapes=[pltpu.VMEM((tm + 2 * POOL_HALO, u2.shape[1]), F32)],
        compiler_params=_params(("parallel",), vmem),
        name="pool_mix_out",
    )(x2, mod, a2, u2, u2, u2, c2, pool_w, pool_scale.reshape(1, -1), w_out)


def _rope_tables(seq_len):
    t = jnp.arange(seq_len, dtype=jnp.int32)
    axis_dim = HEAD_DIM // 2
    inv_freq = ROPE_BASE ** (-jnp.arange(0, axis_dim, 2, dtype=F32) / axis_dim)
    ang_r = (t // GRID_W).astype(F32)[:, None] * inv_freq[None, :]
    ang_c = (t % GRID_W).astype(F32)[:, None] * inv_freq[None, :]
    cos = jnp.concatenate([jnp.cos(ang_r), jnp.cos(ang_r), jnp.cos(ang_c), jnp.cos(ang_c)], axis=-1)
    sin = jnp.concatenate([-jnp.sin(ang_r), jnp.sin(ang_r), -jnp.sin(ang_c), jnp.sin(ang_c)], axis=-1)
    return cos, sin


def kernel(x, c, ctx, c_ctx, w_mod, b_mod, norm_w, ffn1_wi, ffn1_wo, ffn2_wi, ffn2_wo, w_in, w_out,
           na_qk_gain, na_rpb, pool_w, pool_scale, wa_qk_gain, wa_sink):
    b, s, d = x.shape
    cx = ctx.shape[1]
    depth = w_mod.shape[0]
    na_w = na_rpb.shape[1] * HEAD_DIM
    pw = pool_scale.shape[1]
    kv_w = (w_in.shape[2] - 3 * na_w - pw - (d - na_w - pw)) // 2
    widths = (na_w, pw, d - na_w - pw, kv_w)

    n_cond = b + 1
    cvecs = jnp.concatenate([c, c_ctx[None, :], jnp.zeros((-n_cond % SUBLANES, d), F32)], axis=0)
    mods = _modulation(cvecs, w_mod, b_mod).reshape(depth, cvecs.shape[0], N_MOD, d)
    rope = _rope_tables(s)

    x2 = x.reshape(b * s, d)
    c2 = ctx.reshape(b * cx, d)
    for l in range(depth):
        last = l == depth - 1
        mx, mc = mods[l, :b], mods[l, b:b + 1]
        wi1, wo1 = ffn1_wi[l].astype(BF16), ffn1_wo[l].astype(BF16)
        wi2, wo2 = ffn2_wi[l].astype(BF16), ffn2_wo[l].astype(BF16)
        win, wout = w_in[l].astype(BF16), w_out[l].astype(BF16)
        pwl = pool_w[l].astype(BF16)

        x2 = _ffn(x2, mx, 0, norm_w[l, 0], wi1, wo1)
        c2 = _ffn(c2, mc, 0, norm_w[l, 0], wi1, wo1)

        nq, nk, nv, u, wq, wk, wv = _projection(x2, mx, norm_w[l, 1], win, na_qk_gain[l], wa_qk_gain[l],
                                                rope, widths)
        cnq, cnk, cnv, cu, cwq, cwk, cwv = _projection(c2, mc, norm_w[l, 1], win, na_qk_gain[l],
                                                       wa_qk_gain[l], None, widths)
        seq = lambda t: t.reshape(b, s, -1)
        cseq = lambda t: t.reshape(b, cx, -1)
        a_out = _neighbourhood_attention(seq(nq), seq(nk), seq(nv), cseq(cnk), cseq(cnv),
                                         _na_bias_table(na_rpb[l]))
        c_out = _window_attention(seq(wq), seq(wk), seq(wv), cseq(cwk), cseq(cwv), wa_sink[l])
        x2 = _mix_out(x2, mx, a_out.reshape(b * s, -1), u, c_out.reshape(b * s, -1), pwl, pool_scale[l],
                      wout, s)
        if not last:
            ca = _context_attention(cseq(cnq), cseq(cnk), cseq(cnv), None)
            cc = _context_attention(cseq(cwq), cseq(cwk), cseq(cwv), wa_sink[l])
            c2 = _mix_out(c2, mc, ca.reshape(b * cx, -1), cu, cc.reshape(b * cx, -1), pwl, pool_scale[l],
                          wout, cx)
            c2 = _ffn(c2, mc, 6, norm_w[l, 2], wi2, wo2)
        x2 = _ffn(x2, mx, 6, norm_w[l, 2], wi2, wo2)
    return x2.reshape(b, s, d)
```

```python
import functools

import numpy as np
import jax
import jax.numpy as jnp
from jax import lax
from jax.experimental import pallas as pl
from jax.experimental.pallas import tpu as pltpu

HEAD_DIM = 128
GRID_W = 64
NA_WIN_H = 8
NA_WIN_W = 16
POOL_WINDOWS = (2, 4, 8, 16)
WA_BLOCK = 128
N_MOD = 9
MACARON_WEIGHT = 0.5
ROPE_BASE = 10000.0
RMS_EPS = 1e-6
NEG_INF = -1e30
ATTN_SCALE = HEAD_DIM ** -0.5

V7X_VMEM_BYTES = 64 * 1024 * 1024
V7X_VMEM_BUDGET = 56 * 1024 * 1024
SUBLANES = 8
NA_Q_ROWS = NA_WIN_H // 2
POOL_HALO = 16

BF16 = jnp.bfloat16
F32 = jnp.float32


def _params(semantics, vmem_bytes):
    return pltpu.CompilerParams(dimension_semantics=semantics,
                                vmem_limit_bytes=int(min(max(vmem_bytes, 16 << 20), V7X_VMEM_BUDGET)))


def _row_tile(n_rows, want):
    t = min(want, n_rows)
    assert n_rows % t == 0, (n_rows, t)
    return t


def _mod_norm(x, mod_ref, g_ref, first):
    y = x * lax.rsqrt(jnp.mean(x * x, axis=-1, keepdims=True) + RMS_EPS) * g_ref[...]
    return y * (1.0 + mod_ref[first + 1:first + 2, :]) + mod_ref[first:first + 1, :]


def _mod_kernel(c_ref, w_ref, b_ref, o_ref):
    c = c_ref[...]
    s = (c * jax.nn.sigmoid(c)).astype(BF16)
    o_ref[...] = jnp.dot(s, w_ref[...].astype(BF16), preferred_element_type=F32) + b_ref[...]


def _modulation(cvecs, w_mod, b_mod):
    n_layers, d, nd = w_mod.shape
    rows = cvecs.shape[0]
    tn = _row_tile(nd, 2048)
    return pl.pallas_call(
        _mod_kernel,
        grid=(n_layers, nd // tn),
        in_specs=[pl.BlockSpec((rows, d), lambda l, n: (0, 0)),
                  pl.BlockSpec((None, d, tn), lambda l, n: (l, 0, n)),
                  pl.BlockSpec((None, 1, tn), lambda l, n: (l, 0, n))],
        out_specs=pl.BlockSpec((None, rows, tn), lambda l, n: (l, 0, n)),
        out_shape=jax.ShapeDtypeStruct((n_layers, rows, nd), F32),
        compiler_params=_params(("parallel", "parallel"), 2 * d * tn * 4 + (8 << 20)),
        name="adaln_modulation",
    )(cvecs, w_mod, b_mod.reshape(n_layers, 1, nd))


def _ffn_kernel(x_ref, mod_ref, g_ref, wa_ref, wb_ref, wo_ref, o_ref, h_ref, *, first):
    f = pl.program_id(1)

    @pl.when(f == 0)
    def _():
        h_ref[...] = _mod_norm(x_ref[...], mod_ref, g_ref, first).astype(BF16)
        o_ref[...] = jnp.zeros(o_ref.shape, o_ref.dtype)

    h = h_ref[...]
    a = jnp.dot(h, wa_ref[...], preferred_element_type=F32)
    b = jnp.dot(h, wb_ref[...], preferred_element_type=F32)
    act = (a * jax.nn.sigmoid(a) * b).astype(BF16)
    o_ref[...] += jnp.dot(act, wo_ref[...], preferred_element_type=F32)

    @pl.when(f == pl.num_programs(1) - 1)
    def _():
        gate = MACARON_WEIGHT * mod_ref[first + 2:first + 3, :]
        o_ref[...] = x_ref[...] + gate * o_ref[...]


def _ffn(x2, mod, first, norm_g, wi, wo, *, tm=512, tf=512):
    n_tok, d = x2.shape
    d_ff = wo.shape[0]
    tm = _row_tile(n_tok, tm)
    tf = _row_tile(d_ff, tf)
    tiles_per_mod = n_tok // mod.shape[0] // tm
    nf = d_ff // tf
    vmem = 4 * tm * d * 4 + tm * d * 2 + 2 * 3 * d * tf * 2 + 6 * tm * tf * 4 + (4 << 20)
    return pl.pallas_call(
        functools.partial(_ffn_kernel, first=first),
        grid=(n_tok // tm, nf),
        in_specs=[pl.BlockSpec((tm, d), lambda i, f: (i, 0)),
                  pl.BlockSpec((None, N_MOD, d), lambda i, f: (i // tiles_per_mod, 0, 0)),
                  pl.BlockSpec((1, d), lambda i, f: (0, 0)),
                  pl.BlockSpec((d, tf), lambda i, f: (0, f)),
                  pl.BlockSpec((d, tf), lambda i, f: (0, nf + f)),
                  pl.BlockSpec((tf, d), lambda i, f: (f, 0))],
        out_specs=pl.BlockSpec((tm, d), lambda i, f: (i, 0)),
        out_shape=jax.ShapeDtypeStruct((n_tok, d), F32),
        scratch_shapes=[pltpu.VMEM((tm, d), BF16)],
        compiler_params=_params(("parallel", "arbitrary"), vmem),
        name="macaron_ffn",
    )(x2, mod, norm_g.reshape(1, d), wi, wi, wo)


def _head_norm(p, gain):
    return p * lax.rsqrt(jnp.mean(p * p, axis=-1, keepdims=True) + RMS_EPS) * gain


def _rotate(y, cos, sin_signed):
    lane = lax.broadcasted_iota(jnp.int32, y.shape, 1)
    partner = jnp.where((lane & 32) == 0, pltpu.roll(y, HEAD_DIM - 32, 1), pltpu.roll(y, 32, 1))
    return y * cos + partner * sin_signed


def _proj_kernel(*refs, use_rope, widths):
    if use_rope:
        x_ref, mod_ref, g_ref, w_ref, nag_ref, wag_ref, cos_ref, sin_ref = refs[:8]
        outs = refs[8:]
    else:
        x_ref, mod_ref, g_ref, w_ref, nag_ref, wag_ref = refs[:6]
        outs = refs[6:]
    nq_ref, nk_ref, nv_ref, u_ref, wq_ref, wk_ref, wv_ref = outs
    na_w, pool_w, wa_w, kv_w = widths
    h = _mod_norm(x_ref[...], mod_ref, g_ref, 3).astype(BF16)

    def chunk(col0, width):
        return jnp.dot(h, w_ref[:, col0:col0 + width], preferred_element_type=F32)

    def normed(p, gain, out_ref, out_col0, scale, rope):
        for j in range(p.shape[1] // HEAD_DIM):
            y = _head_norm(p[:, j * HEAD_DIM:(j + 1) * HEAD_DIM], gain)
            if rope:
                y = _rotate(y, cos_ref[...], sin_ref[...])
            if scale != 1.0:
                y = y * scale
            c0 = out_col0 + j * HEAD_DIM
            out_ref[:, c0:c0 + HEAD_DIM] = y.astype(out_ref.dtype)

    col = 0
    normed(chunk(col, na_w), nag_ref[0:1, :], nq_ref, 0, ATTN_SCALE, False)
    col += na_w
    normed(chunk(col, na_w), nag_ref[1:2, :], nk_ref, 0, 1.0, False)
    col += na_w
    nv_ref[...] = chunk(col, na_w).astype(BF16)
    col += na_w
    u_ref[...] = chunk(col, pool_w)
    col += pool_w
    half = wa_w // 2
    normed(chunk(col, half), wag_ref[0:1, :], wq_ref, 0, ATTN_SCALE, use_rope)
    normed(chunk(col + half, half), wag_ref[0:1, :], wq_ref, half, ATTN_SCALE, use_rope)
    col += wa_w
    normed(chunk(col, kv_w), wag_ref[1:2, :], wk_ref, 0, 1.0, use_rope)
    col += kv_w
    wv_ref[...] = chunk(col, kv_w).astype(BF16)


def _projection(x2, mod, norm_g, w_in, na_gain, wa_gain, rope, widths, *, tm=512):
    n_tok, d = x2.shape
    na_w, pool_w, wa_w, kv_w = widths
    n_in = w_in.shape[1]
    tm = _row_tile(n_tok, tm)
    tiles_per_mod = n_tok // mod.shape[0] // tm
    use_rope = rope is not None
    in_specs = [pl.BlockSpec((tm, d), lambda i: (i, 0)),
                pl.BlockSpec((None, N_MOD, d), lambda i: (i // tiles_per_mod, 0, 0)),
                pl.BlockSpec((1, d), lambda i: (0, 0)),
                pl.BlockSpec((d, n_in), lambda i: (0, 0)),
                pl.BlockSpec((2, HEAD_DIM), lambda i: (0, 0)),
                pl.BlockSpec((2, HEAD_DIM), lambda i: (0, 0))]
    args = [x2, mod, norm_g.reshape(1, d), w_in, na_gain, wa_gain]
    if use_rope:
        tiles_per_seq = rope[0].shape[0] // tm
        in_specs += [pl.BlockSpec((tm, HEAD_DIM), lambda i: (i % tiles_per_seq, 0))] * 2
        args += list(rope)
    out_widths = (na_w, na_w, na_w, pool_w, wa_w, kv_w, kv_w)
    out_dtypes = (BF16, BF16, BF16, F32, BF16, BF16, BF16)
    vmem = 2 * tm * d * 4 + tm * d * 2 + 2 * d * n_in * 2 + 4 * tm * n_in * 4 + (4 << 20)
    return pl.pallas_call(
        functools.partial(_proj_kernel, use_rope=use_rope, widths=widths),
        grid=(n_tok // tm,),
        in_specs=in_specs,
        out_specs=[pl.BlockSpec((tm, w), lambda i: (i, 0)) for w in out_widths],
        out_shape=[jax.ShapeDtypeStruct((n_tok, w), dt) for w, dt in zip(out_widths, out_dtypes)],
        compiler_params=_params(("parallel",), vmem),
        name="input_projection",
    )(*args)


def _softmax_pv(scores, values, extra_logit=None):
    m = functools.reduce(jnp.maximum, [jnp.max(s, axis=-1, keepdims=True) for s in scores])
    if extra_logit is not None:
        m = jnp.maximum(m, extra_logit)
    ps = [jnp.exp(s - m) for s in scores]
    denom = functools.reduce(jnp.add, [jnp.sum(p, axis=-1, keepdims=True) for p in ps])
    if extra_logit is not None:
        denom = denom + jnp.exp(extra_logit - m)
    acc = functools.reduce(jnp.add, [jnp.dot(p.astype(BF16), v, preferred_element_type=F32)
                                     for p, v in zip(ps, values)])
    return acc / denom


def _qk(q, k):
    return lax.dot_general(q, k, (((1,), (1,)), ((), ())), preferred_element_type=F32)


def _na_kernel(q_ref, kp_ref, kc_ref, kn_ref, vp_ref, vc_ref, vn_ref, kx_ref, vx_ref, bias_ref, o_ref):
    for h in range(q_ref.shape[1] // HEAD_DIM):
        hs = slice(h * HEAD_DIM, (h + 1) * HEAD_DIM)
        k = jnp.concatenate([kp_ref[:, hs], kc_ref[:, hs], kn_ref[:, hs], kx_ref[:, hs]], axis=0)
        v = jnp.concatenate([vp_ref[:, hs], vc_ref[:, hs], vn_ref[:, hs], vx_ref[:, hs]], axis=0)
        s = _qk(q_ref[:, hs], k) + bias_ref[h]
        o_ref[:, hs] = _softmax_pv([s], [v]).astype(o_ref.dtype)


def _na_bias_kernel(rpb_ref, o_ref, g_ref, *, n_rows):
    n_ro, n_co = 2 * NA_WIN_H - 1, 2 * NA_WIN_W - 1
    shape = (GRID_W, 2 * GRID_W)
    qc = lax.broadcasted_iota(jnp.int32, shape, 0)
    lane = lax.broadcasted_iota(jnp.int32, shape, 1)
    kc = lane & (GRID_W - 1)
    c0 = jnp.clip(qc - NA_WIN_W // 2, 0, GRID_W - NA_WIN_W)
    in_cols = (kc >= c0) & (kc < c0 + NA_WIN_W)
    col_off = kc - qc + (NA_WIN_W - 1)
    base = pl.program_id(0) * (n_ro * n_co)
    for ro in range(n_ro):
        g = jnp.zeros(shape, F32)
        for c in range(n_co):
            g = jnp.where(col_off == c, rpb_ref[base + ro * n_co + c], g)
        g_ref[ro] = jnp.where(in_cols, g, NEG_INF)

    masked = jnp.full(shape, NEG_INF, F32)
    n_local = 3 * NA_Q_ROWS * GRID_W
    n_blocks = n_rows // NA_Q_ROWS
    o_ref[:, :, n_local:] = jnp.zeros(o_ref.shape[:2] + (o_ref.shape[2] - n_local,), F32)
    for variant, jb in enumerate((0, 1, n_blocks - 1)):
        for i in range(NA_Q_ROWS):
            r = jb * NA_Q_ROWS + i
            r0 = min(max(r - NA_WIN_H // 2, 0), n_rows - NA_WIN_H)
            halves = []
            for kr in range(3 * NA_Q_ROWS):
                kr_abs = (jb - 1) * NA_Q_ROWS + kr
                in_rows = 0 <= kr_abs < n_rows and r0 <= kr_abs < r0 + NA_WIN_H
                halves.append(g_ref[kr_abs - r + NA_WIN_H - 1] if in_rows else masked)
            for m in range(len(halves) // 2):
                o_ref[variant, i * GRID_W:(i + 1) * GRID_W, m * 2 * GRID_W:(m + 1) * 2 * GRID_W] = jnp.where(
                    lane < GRID_W, halves[2 * m], halves[2 * m + 1])


def _na_bias_table(rpb, n_rows, n_ctx):
    n_heads = rpb.shape[0]
    assert n_rows % NA_Q_ROWS == 0 and n_rows // NA_Q_ROWS >= 3 and (3 * NA_Q_ROWS) % 2 == 0
    shape = (3, n_heads, NA_Q_ROWS * GRID_W, 3 * NA_Q_ROWS * GRID_W + n_ctx)
    return pl.pallas_call(
        functools.partial(_na_bias_kernel, n_rows=n_rows),
        grid=(n_heads,),
        in_specs=[pl.BlockSpec(memory_space=pltpu.SMEM)],
        out_specs=pl.BlockSpec((3, None) + shape[2:], lambda h: (0, h, 0, 0)),
        out_shape=jax.ShapeDtypeStruct(shape, F32),
        scratch_shapes=[pltpu.VMEM((2 * NA_WIN_H - 1, GRID_W, 2 * GRID_W), F32)],
        compiler_params=_params(("parallel",), 24 << 20),
        name="na_bias_table",
    )(rpb.astype(F32).reshape(-1))


def _neighbourhood_attention(q, k, v, kx, vx, bias):
    b, s, w = q.shape
    blk = NA_Q_ROWS * GRID_W
    nj = s // blk
    cx = kx.shape[1]
    cur = lambda bb, j: (bb, j, 0)
    prev = lambda bb, j: (bb, jnp.maximum(j - 1, 0), 0)
    nxt = lambda bb, j: (bb, jnp.minimum(j + 1, nj - 1), 0)
    edge = lambda bb, j: (jnp.where(j == 0, 0, jnp.where(j == nj - 1, 2, 1)), 0, 0, 0)
    tok = lambda im: pl.BlockSpec((None, blk, w), im)
    ctx = pl.BlockSpec((None, cx, w), lambda bb, j: (bb, 0, 0))
    vmem = 2 * bias[0].size * 4 + 6 * blk * bias.shape[3] * 4 + (8 << 20)
    return pl.pallas_call(
        _na_kernel,
        grid=(b, nj),
        in_specs=[tok(cur), tok(prev), tok(cur), tok(nxt), tok(prev), tok(cur), tok(nxt), ctx, ctx,
                  pl.BlockSpec((None,) + bias.shape[1:], edge)],
        out_specs=tok(cur),
        out_shape=jax.ShapeDtypeStruct((b, s, w), BF16),
        compiler_params=_params(("parallel", "parallel"), vmem),
        name="neighbourhood_attention",
    )(q, k, k, k, v, v, v, kx, vx, bias)


def _wa_band_bias(n_ctx):
    blk = WA_BLOCK
    p = np.arange(blk)[:, None]
    k = np.arange(3 * blk + n_ctx)[None, :]
    in_prev, in_next = k < blk, (k >= 2 * blk) & (k < 3 * blk)
    tables = []
    for has_prev, has_next in ((False, True), (True, True), (True, False)):
        ok_prev = (k >= p) & has_prev
        ok_next = (k - 2 * blk <= p) & has_next
        valid = np.where(in_prev, ok_prev, np.where(in_next, ok_next, True))
        tables.append(np.where(valid, 0.0, NEG_INF).astype(np.float32))
    return np.stack(tables)


def _wa_kernel(sink_ref, band_ref, q_ref, kp_ref, kc_ref, kn_ref, vp_ref, vc_ref, vn_ref, kx_ref, vx_ref,
               o_ref, *, group):
    blk = WA_BLOCK
    n_kv = kc_ref.shape[1] // HEAD_DIM
    band = jnp.concatenate([band_ref[...]] * group, axis=0)
    for kv in range(n_kv):
        hs = slice(kv * HEAD_DIM, (kv + 1) * HEAD_DIM)
        heads = [kv * group + g for g in range(group)]
        q = jnp.concatenate([q_ref[:, h * HEAD_DIM:(h + 1) * HEAD_DIM] for h in heads], axis=0)
        k = jnp.concatenate([kp_ref[:, hs], kc_ref[:, hs], kn_ref[:, hs], kx_ref[:, hs]], axis=0)
        v = jnp.concatenate([vp_ref[:, hs], vc_ref[:, hs], vn_ref[:, hs], vx_ref[:, hs]], axis=0)
        s = _qk(q, k) + band
        sink = jnp.concatenate([jnp.full((blk, 1), sink_ref[h], F32) for h in heads], axis=0)
        out = _softmax_pv([s], [v], extra_logit=sink)
        for g, h in enumerate(heads):
            o_ref[:, h * HEAD_DIM:(h + 1) * HEAD_DIM] = out[g * blk:(g + 1) * blk, :].astype(o_ref.dtype)


def _window_attention(q, k, v, kx, vx, sink):
    b, s, wq = q.shape
    wk = k.shape[2]
    group = wq // wk
    nblk = s // WA_BLOCK
    cx = kx.shape[1]
    cur = lambda bb, j: (bb, j, 0)
    prev = lambda bb, j: (bb, jnp.maximum(j - 1, 0), 0)
    nxt = lambda bb, j: (bb, jnp.minimum(j + 1, nblk - 1), 0)
    kvs = lambda im: pl.BlockSpec((None, WA_BLOCK, wk), im)
    ctx = pl.BlockSpec((None, cx, wk), lambda bb, j: (bb, 0, 0))
    assert nblk >= 2
    band = jnp.asarray(_wa_band_bias(cx))
    edge = lambda bb, j: (jnp.where(j == 0, 0, jnp.where(j == nblk - 1, 2, 1)), 0, 0)
    return pl.pallas_call(
        functools.partial(_wa_kernel, group=group),
        grid=(b, nblk),
        in_specs=[pl.BlockSpec(memory_space=pltpu.SMEM),
                  pl.BlockSpec((None,) + band.shape[1:], edge),
                  pl.BlockSpec((None, WA_BLOCK, wq), cur),
                  kvs(prev), kvs(cur), kvs(nxt), kvs(prev), kvs(cur), kvs(nxt), ctx, ctx],
        out_specs=pl.BlockSpec((None, WA_BLOCK, wq), cur),
        out_shape=jax.ShapeDtypeStruct((b, s, wq), BF16),
        compiler_params=_params(("parallel", "parallel"), 24 << 20),
        name="window_attention",
    )(sink, band, q, k, k, k, v, v, v, kx, vx)


def _ctx_attn_kernel(*refs, group, use_sink):
    if use_sink:
        sink_ref, q_ref, k_ref, v_ref, o_ref = refs
    else:
        q_ref, k_ref, v_ref, o_ref = refs
    c = q_ref.shape[0]
    for kv in range(k_ref.shape[1] // HEAD_DIM):
        hs = slice(kv * HEAD_DIM, (kv + 1) * HEAD_DIM)
        heads = [kv * group + g for g in range(group)]
        q = jnp.concatenate([q_ref[:, h * HEAD_DIM:(h + 1) * HEAD_DIM] for h in heads], axis=0)
        sink = None
        if use_sink:
            sink = jnp.concatenate([jnp.full((c, 1), sink_ref[h], F32) for h in heads], axis=0)
        out = _softmax_pv([_qk(q, k_ref[:, hs])], [v_ref[:, hs]], extra_logit=sink)
        for g, h in enumerate(heads):
            o_ref[:, h * HEAD_DIM:(h + 1) * HEAD_DIM] = out[g * c:(g + 1) * c, :].astype(o_ref.dtype)


def _context_attention(q, k, v, sink):
    b, c, wq = q.shape
    wk = k.shape[2]
    use_sink = sink is not None
    spec = lambda w: pl.BlockSpec((None, c, w), lambda bb: (bb, 0, 0))
    in_specs = [spec(wq), spec(wk), spec(wk)]
    args = [q, k, v]
    if use_sink:
        in_specs = [pl.BlockSpec(memory_space=pltpu.SMEM)] + in_specs
        args = [sink] + args
    return pl.pallas_call(
        functools.partial(_ctx_attn_kernel, group=wq // wk, use_sink=use_sink),
        grid=(b,),
        in_specs=in_specs,
        out_specs=spec(wq),
        out_shape=jax.ShapeDtypeStruct((b, c, wq), BF16),
        compiler_params=_params(("parallel",), 24 << 20),
        name="context_attention",
    )(*args)


def _out_kernel(x_ref, mod_ref, a_ref, u_ref, up_ref, un_ref, c_ref, pw_ref, ps_ref, wo_ref, o_ref, ext_ref,
                *, seq_len):
    tm = x_ref.shape[0]
    tiles_per_seq = seq_len // tm
    t_in_seq = pl.program_id(0) % tiles_per_seq
    halo = POOL_HALO
    has_prev = (t_in_seq > 0).astype(F32)
    has_next = (t_in_seq < tiles_per_seq - 1).astype(F32)
    ext_ref[0:halo, :] = up_ref[...] * has_prev
    ext_ref[halo:halo + tm, :] = u_ref[...]
    ext_ref[halo + tm:, :] = un_ref[...] * has_next

    pos = t_in_seq * tm + lax.broadcasted_iota(jnp.int32, (tm, 1), 0)
    parts = []
    for g, win in enumerate(POOL_WINDOWS):
        ls = slice(g * HEAD_DIM, (g + 1) * HEAD_DIM)
        total = functools.reduce(jnp.add, [ext_ref[halo + d:halo + d + tm, ls]
                                           for d in range(-(win // 2), win - win // 2)])
        lo = jnp.maximum(pos - win // 2, 0)
        hi = jnp.minimum(pos + (win - win // 2 - 1), seq_len - 1)
        delta = total / (hi - lo + 1).astype(F32) - u_ref[:, ls]
        y = jnp.dot(delta.astype(BF16), pw_ref[g], preferred_element_type=F32) * ps_ref[:, ls]
        parts.append(y.astype(BF16))
    na_w = a_ref.shape[1]
    pool_w = len(POOL_WINDOWS) * HEAD_DIM
    mixed = jnp.dot(a_ref[...], wo_ref[0:na_w, :], preferred_element_type=F32)
    mixed += jnp.dot(jnp.concatenate(parts, axis=1), wo_ref[na_w:na_w + pool_w, :], preferred_element_type=F32)
    mixed += jnp.dot(c_ref[...], wo_ref[na_w + pool_w:, :], preferred_element_type=F32)
    o_ref[...] = x_ref[...] + mod_ref[5:6, :] * mixed


def _mix_out(x2, mod, a2, u2, c2, pool_w, pool_scale, w_out, seq_len, *, tm=512):
    n_tok, d = x2.shape
    tm = _row_tile(seq_len, tm)
    tiles_per_mod = n_tok // mod.shape[0] // tm
    n_halo = n_tok // POOL_HALO
    hpt = tm // POOL_HALO
    row = lambda w: pl.BlockSpec((tm, w), lambda i: (i, 0))
    vmem = 4 * tm * d * 4 + 2 * w_out.size * 2 + 4 * tm * d * 4 + (4 << 20)
    return pl.pallas_call(
        functools.partial(_out_kernel, seq_len=seq_len),
        grid=(n_tok // tm,),
        in_specs=[row(d),
                  pl.BlockSpec((None, N_MOD, d), lambda i: (i // tiles_per_mod, 0, 0)),
                  row(a2.shape[1]), row(u2.shape[1]),
                  pl.BlockSpec((POOL_HALO, u2.shape[1]), lambda i: (jnp.maximum(i * hpt - 1, 0), 0)),
                  pl.BlockSpec((POOL_HALO, u2.shape[1]), lambda i: (jnp.minimum((i + 1) * hpt, n_halo - 1), 0)),
                  row(c2.shape[1]),
                  pl.BlockSpec(pool_w.shape, lambda i: (0, 0, 0)),
                  pl.BlockSpec((1, u2.shape[1]), lambda i: (0, 0)),
                  pl.BlockSpec(w_out.shape, lambda i: (0, 0))],
        out_specs=row(d),
        out_shape=jax.ShapeDtypeStruct((n_tok, d), F32),
        scratch_shapes=[pltpu.VMEM((tm + 2 * POOL_HALO, u2.shape[1]), F32)],
        compiler_params=_params(("parallel",), vmem),
        name="pool_mix_out",
    )(x2, mod, a2, u2, u2, u2, c2, pool_w, pool_scale.reshape(1, -1), w_out)


def _rope_tables(seq_len):
    t = jnp.arange(seq_len, dtype=jnp.int32)
    axis_dim = HEAD_DIM // 2
    inv_freq = ROPE_BASE ** (-jnp.arange(0, axis_dim, 2, dtype=F32) / axis_dim)
    ang_r = (t // GRID_W).astype(F32)[:, None] * inv_freq[None, :]
    ang_c = (t % GRID_W).astype(F32)[:, None] * inv_freq[None, :]
    cos = jnp.concatenate([jnp.cos(ang_r), jnp.cos(ang_r), jnp.cos(ang_c), jnp.cos(ang_c)], axis=-1)
    sin = jnp.concatenate([-jnp.sin(ang_r), jnp.sin(ang_r), -jnp.sin(ang_c), jnp.sin(ang_c)], axis=-1)
    return cos, sin


def kernel(x, c, ctx, c_ctx, w_mod, b_mod, norm_w, ffn1_wi, ffn1_wo, ffn2_wi, ffn2_wo, w_in, w_out,
           na_qk_gain, na_rpb, pool_w, pool_scale, wa_qk_gain, wa_sink):
    b, s, d = x.shape
    cx = ctx.shape[1]
    depth = w_mod.shape[0]
    na_w = na_rpb.shape[1] * HEAD_DIM
    pw = pool_scale.shape[1]
    kv_w = (w_in.shape[2] - 3 * na_w - pw - (d - na_w - pw)) // 2
    widths = (na_w, pw, d - na_w - pw, kv_w)

    n_cond = b + 1
    cvecs = jnp.concatenate([c, c_ctx[None, :], jnp.zeros((-n_cond % SUBLANES, d), F32)], axis=0)
    mods = _modulation(cvecs, w_mod, b_mod).reshape(depth, cvecs.shape[0], N_MOD, d)
    rope = _rope_tables(s)

    x2 = x.reshape(b * s, d)
    c2 = ctx.reshape(b * cx, d)
    for l in range(depth):
        last = l == depth - 1
        mx, mc = mods[l, :b], mods[l, b:b + 1]
        wi1, wo1 = ffn1_wi[l].astype(BF16), ffn1_wo[l].astype(BF16)
        wi2, wo2 = ffn2_wi[l].astype(BF16), ffn2_wo[l].astype(BF16)
        win, wout = w_in[l].astype(BF16), w_out[l].astype(BF16)
        pwl = pool_w[l].astype(BF16)

        x2 = _ffn(x2, mx, 0, norm_w[l, 0], wi1, wo1)
        c2 = _ffn(c2, mc, 0, norm_w[l, 0], wi1, wo1)

        nq, nk, nv, u, wq, wk, wv = _projection(x2, mx, norm_w[l, 1], win, na_qk_gain[l], wa_qk_gain[l],
                                                rope, widths)
        cnq, cnk, cnv, cu, cwq, cwk, cwv = _projection(c2, mc, norm_w[l, 1], win, na_qk_gain[l],
                                                       wa_qk_gain[l], None, widths)
        seq = lambda t: t.reshape(b, s, -1)
        cseq = lambda t: t.reshape(b, cx, -1)
        a_out = _neighbourhood_attention(seq(nq), seq(nk), seq(nv), cseq(cnk), cseq(cnv),
                                         _na_bias_table(na_rpb[l], s // GRID_W, cx))
        c_out = _window_attention(seq(wq), seq(wk), seq(wv), cseq(cwk), cseq(cwv), wa_sink[l])
        x2 = _mix_out(x2, mx, a_out.reshape(b * s, -1), u, c_out.reshape(b * s, -1), pwl, pool_scale[l],
                      wout, s)
        if not last:
            ca = _context_attention(cseq(cnq), cseq(cnk), cseq(cnv), None)
            cc = _context_attention(cseq(cwq), cseq(cwk), cseq(cwv), wa_sink[l])
            c2 = _mix_out(c2, mc, ca.reshape(b * cx, -1), cu, cc.reshape(b * cx, -1), pwl, pool_scale[l],
                          wout, cx)
            c2 = _ffn(c2, mc, 6, norm_w[l, 2], wi2, wo2)
        x2 = _ffn(x2, mx, 6, norm_w[l, 2], wi2, wo2)
    return x2.reshape(b, s, d)
```

```python
import functools

import numpy as np
import jax
import jax.numpy as jnp
from jax import lax
from jax.experimental import pallas as pl
from jax.experimental.pallas import tpu as pltpu

HEAD_DIM = 128
GRID_W = 64
NA_WIN_H = 8
NA_WIN_W = 16
POOL_WINDOWS = (2, 4, 8, 16)
WA_BLOCK = 128
N_MOD = 9
MACARON_WEIGHT = 0.5
ROPE_BASE = 10000.0
RMS_EPS = 1e-6
NEG_INF = -1e30
ATTN_SCALE = HEAD_DIM ** -0.5

V7X_VMEM_BYTES = 64 * 1024 * 1024
V7X_VMEM_BUDGET = 56 * 1024 * 1024
SUBLANES = 8
NA_Q_ROWS = NA_WIN_H // 2
POOL_HALO = 16

BF16 = jnp.bfloat16
F32 = jnp.float32


def _params(semantics, vmem_bytes):
    return pltpu.CompilerParams(dimension_semantics=semantics,
                                vmem_limit_bytes=int(min(max(vmem_bytes, 16 << 20), V7X_VMEM_BUDGET)))


def _row_tile(n_rows, want):
    t = min(want, n_rows)
    assert n_rows % t == 0, (n_rows, t)
    return t


def _mod_norm(x, mod_ref, g_ref, first):
    y = x * lax.rsqrt(jnp.mean(x * x, axis=-1, keepdims=True) + RMS_EPS) * g_ref[...]
    return y * (1.0 + mod_ref[first + 1:first + 2, :]) + mod_ref[first:first + 1, :]


def _mod_kernel(c_ref, w_ref, b_ref, o_ref):
    c = c_ref[...]
    s = (c * jax.nn.sigmoid(c)).astype(BF16)
    o_ref[...] = jnp.dot(s, w_ref[...].astype(BF16), preferred_element_type=F32) + b_ref[...]


def _modulation(cvecs, w_mod, b_mod):
    n_layers, d, nd = w_mod.shape
    rows = cvecs.shape[0]
    tn = _row_tile(nd, 2048)
    return pl.pallas_call(
        _mod_kernel,
        grid=(n_layers, nd // tn),
        in_specs=[pl.BlockSpec((rows, d), lambda l, n: (0, 0)),
                  pl.BlockSpec((None, d, tn), lambda l, n: (l, 0, n)),
                  pl.BlockSpec((None, 1, tn), lambda l, n: (l, 0, n))],
        out_specs=pl.BlockSpec((None, rows, tn), lambda l, n: (l, 0, n)),
        out_shape=jax.ShapeDtypeStruct((n_layers, rows, nd), F32),
        compiler_params=_params(("parallel", "parallel"), 2 * d * tn * 4 + (8 << 20)),
        name="adaln_modulation",
    )(cvecs, w_mod, b_mod.reshape(n_layers, 1, nd))


def _ffn_kernel(x_ref, mod_ref, g_ref, wa_ref, wb_ref, wo_ref, o_ref, h_ref, *, first):
    f = pl.program_id(1)

    @pl.when(f == 0)
    def _():
        h_ref[...] = _mod_norm(x_ref[...], mod_ref, g_ref, first).astype(BF16)
        o_ref[...] = jnp.zeros(o_ref.shape, o_ref.dtype)

    h = h_ref[...]
    a = jnp.dot(h, wa_ref[...], preferred_element_type=F32)
    b = jnp.dot(h, wb_ref[...], preferred_element_type=F32)
    act = (a * jax.nn.sigmoid(a) * b).astype(BF16)
    o_ref[...] += jnp.dot(act, wo_ref[...], preferred_element_type=F32)

    @pl.when(f == pl.num_programs(1) - 1)
    def _():
        gate = MACARON_WEIGHT * mod_ref[first + 2:first + 3, :]
        o_ref[...] = x_ref[...] + gate * o_ref[...]


def _ffn(x2, mod, first, norm_g, wi, wo, layer, *, tm=512, tf=512):
    n_tok, d = x2.shape
    d_ff = wo.shape[1]
    tm = _row_tile(n_tok, tm)
    tf = _row_tile(d_ff, tf)
    tiles_per_mod = n_tok // mod.shape[0] // tm
    nf = d_ff // tf
    vmem = 4 * tm * d * 4 + tm * d * 2 + 2 * 3 * d * tf * 2 + 6 * tm * tf * 4 + (4 << 20)
    return pl.pallas_call(
        functools.partial(_ffn_kernel, first=first),
        grid=(n_tok // tm, nf),
        in_specs=[pl.BlockSpec((tm, d), lambda i, f: (i, 0)),
                  pl.BlockSpec((None, N_MOD, d), lambda i, f: (i // tiles_per_mod, 0, 0)),
                  pl.BlockSpec((1, d), lambda i, f: (0, 0)),
                  pl.BlockSpec((None, d, tf), lambda i, f: (layer, 0, f)),
                  pl.BlockSpec((None, d, tf), lambda i, f: (layer, 0, nf + f)),
                  pl.BlockSpec((None, tf, d), lambda i, f: (layer, f, 0))],
        out_specs=pl.BlockSpec((tm, d), lambda i, f: (i, 0)),
        out_shape=jax.ShapeDtypeStruct((n_tok, d), F32),
        scratch_shapes=[pltpu.VMEM((tm, d), BF16)],
        compiler_params=_params(("parallel", "arbitrary"), vmem),
        name="macaron_ffn",
    )(x2, mod, norm_g.reshape(1, d), wi, wi, wo)


def _head_norm(p, gain):
    return p * lax.rsqrt(jnp.mean(p * p, axis=-1, keepdims=True) + RMS_EPS) * gain


def _rotate(y, cos, sin_signed):
    lane = lax.broadcasted_iota(jnp.int32, y.shape, 1)
    partner = jnp.where((lane & 32) == 0, pltpu.roll(y, HEAD_DIM - 32, 1), pltpu.roll(y, 32, 1))
    return y * cos + partner * sin_signed


def _proj_kernel(*refs, use_rope, widths):
    if use_rope:
        x_ref, mod_ref, g_ref, w_ref, nag_ref, wag_ref, cos_ref, sin_ref = refs[:8]
        outs = refs[8:]
    else:
        x_ref, mod_ref, g_ref, w_ref, nag_ref, wag_ref = refs[:6]
        outs = refs[6:]
    nq_ref, nk_ref, nv_ref, u_ref, wq_ref, wk_ref, wv_ref = outs
    na_w, pool_w, wa_w, kv_w = widths
    h = _mod_norm(x_ref[...], mod_ref, g_ref, 3).astype(BF16)

    def chunk(col0, width):
        return jnp.dot(h, w_ref[:, col0:col0 + width], preferred_element_type=F32)

    def normed(p, gain, out_ref, out_col0, scale, rope):
        for j in range(p.shape[1] // HEAD_DIM):
            y = _head_norm(p[:, j * HEAD_DIM:(j + 1) * HEAD_DIM], gain)
            if rope:
                y = _rotate(y, cos_ref[...], sin_ref[...])
            if scale != 1.0:
                y = y * scale
            c0 = out_col0 + j * HEAD_DIM
            out_ref[:, c0:c0 + HEAD_DIM] = y.astype(out_ref.dtype)

    c_nq, c_nk, c_nv, c_u = 0, na_w, 2 * na_w, 3 * na_w
    c_wq = c_u + pool_w
    c_wk = c_wq + wa_w
    c_wv = c_wk + kv_w
    half = wa_w // 2
    normed(chunk(c_wq, half), wag_ref[0:1, :], wq_ref, 0, ATTN_SCALE, use_rope)
    normed(chunk(c_wq + half, half), wag_ref[0:1, :], wq_ref, half, ATTN_SCALE, use_rope)
    normed(chunk(c_wk, kv_w), wag_ref[1:2, :], wk_ref, 0, 1.0, use_rope)
    normed(chunk(c_nq, na_w), nag_ref[0:1, :], nq_ref, 0, ATTN_SCALE, False)
    normed(chunk(c_nk, na_w), nag_ref[1:2, :], nk_ref, 0, 1.0, False)
    nv_ref[...] = chunk(c_nv, na_w).astype(BF16)
    u_ref[...] = chunk(c_u, pool_w)
    wv_ref[...] = chunk(c_wv, kv_w).astype(BF16)


def _projection(x2, mod, norm_g, w_in, layer, na_gain, wa_gain, rope, widths, *, tm=512):
    n_tok, d = x2.shape
    na_w, pool_w, wa_w, kv_w = widths
    n_in = w_in.shape[2]
    tm = _row_tile(n_tok, tm)
    tiles_per_mod = n_tok // mod.shape[0] // tm
    use_rope = rope is not None
    in_specs = [pl.BlockSpec((tm, d), lambda i: (i, 0)),
                pl.BlockSpec((None, N_MOD, d), lambda i: (i // tiles_per_mod, 0, 0)),
                pl.BlockSpec((1, d), lambda i: (0, 0)),
                pl.BlockSpec((None, d, n_in), lambda i: (layer, 0, 0)),
                pl.BlockSpec((2, HEAD_DIM), lambda i: (0, 0)),
                pl.BlockSpec((2, HEAD_DIM), lambda i: (0, 0))]
    args = [x2, mod, norm_g.reshape(1, d), w_in, na_gain, wa_gain]
    if use_rope:
        tiles_per_seq = rope[0].shape[0] // tm
        in_specs += [pl.BlockSpec((tm, HEAD_DIM), lambda i: (i % tiles_per_seq, 0))] * 2
        args += list(rope)
    out_widths = (na_w, na_w, na_w, pool_w, wa_w, kv_w, kv_w)
    out_dtypes = (BF16, BF16, BF16, F32, BF16, BF16, BF16)
    vmem = 2 * tm * d * 4 + tm * d * 2 + 2 * d * n_in * 2 + 4 * tm * n_in * 4 + (4 << 20)
    return pl.pallas_call(
        functools.partial(_proj_kernel, use_rope=use_rope, widths=widths),
        grid=(n_tok // tm,),
        in_specs=in_specs,
        out_specs=[pl.BlockSpec((tm, w), lambda i: (i, 0)) for w in out_widths],
        out_shape=[jax.ShapeDtypeStruct((n_tok, w), dt) for w, dt in zip(out_widths, out_dtypes)],
        compiler_params=_params(("parallel",), vmem),
        name="input_projection",
    )(*args)


def _softmax_pv(scores, values, extra_logit=None):
    m = functools.reduce(jnp.maximum, [jnp.max(s, axis=-1, keepdims=True) for s in scores])
    if extra_logit is not None:
        m = jnp.maximum(m, extra_logit)
    ps = [jnp.exp(s - m) for s in scores]
    denom = functools.reduce(jnp.add, [jnp.sum(p, axis=-1, keepdims=True) for p in ps])
    if extra_logit is not None:
        denom = denom + jnp.exp(extra_logit - m)
    acc = functools.reduce(jnp.add, [jnp.dot(p.astype(BF16), v, preferred_element_type=F32)
                                     for p, v in zip(ps, values)])
    return acc / denom


def _qk(q, k):
    return lax.dot_general(q, k, (((1,), (1,)), ((), ())), preferred_element_type=F32)


def _na_kernel(q_ref, kp_ref, kc_ref, kn_ref, vp_ref, vc_ref, vn_ref, kx_ref, vx_ref, bias_ref, o_ref):
    for h in range(q_ref.shape[1] // HEAD_DIM):
        hs = slice(h * HEAD_DIM, (h + 1) * HEAD_DIM)
        k = jnp.concatenate([kp_ref[:, hs], kc_ref[:, hs], kn_ref[:, hs], kx_ref[:, hs]], axis=0)
        v = jnp.concatenate([vp_ref[:, hs], vc_ref[:, hs], vn_ref[:, hs], vx_ref[:, hs]], axis=0)
        s = _qk(q_ref[:, hs], k) + bias_ref[h]
        o_ref[:, hs] = _softmax_pv([s], [v]).astype(o_ref.dtype)


def _na_bias_kernel(rpb_ref, o_ref, g_ref, *, n_rows):
    n_ro, n_co = 2 * NA_WIN_H - 1, 2 * NA_WIN_W - 1
    shape = (GRID_W, 2 * GRID_W)
    qc = lax.broadcasted_iota(jnp.int32, shape, 0)
    lane = lax.broadcasted_iota(jnp.int32, shape, 1)
    kc = lane & (GRID_W - 1)
    c0 = jnp.clip(qc - NA_WIN_W // 2, 0, GRID_W - NA_WIN_W)
    in_cols = (kc >= c0) & (kc < c0 + NA_WIN_W)
    col_off = kc - qc + (NA_WIN_W - 1)
    base = pl.program_id(0) * (n_ro * n_co)
    for ro in range(n_ro):
        g = jnp.zeros(shape, F32)
        for c in range(n_co):
            g = jnp.where(col_off == c, rpb_ref[base + ro * n_co + c], g)
        g_ref[ro] = jnp.where(in_cols, g, NEG_INF)

    masked = jnp.full(shape, NEG_INF, F32)
    n_local = 3 * NA_Q_ROWS * GRID_W
    n_blocks = n_rows // NA_Q_ROWS
    o_ref[:, :, n_local:] = jnp.zeros(o_ref.shape[:2] + (o_ref.shape[2] - n_local,), F32)
    for variant, jb in enumerate((0, 1, n_blocks - 1)):
        for i in range(NA_Q_ROWS):
            r = jb * NA_Q_ROWS + i
            r0 = min(max(r - NA_WIN_H // 2, 0), n_rows - NA_WIN_H)
            halves = []
            for kr in range(3 * NA_Q_ROWS):
                kr_abs = (jb - 1) * NA_Q_ROWS + kr
                in_rows = 0 <= kr_abs < n_rows and r0 <= kr_abs < r0 + NA_WIN_H
                halves.append(g_ref[kr_abs - r + NA_WIN_H - 1] if in_rows else masked)
            for m in range(len(halves) // 2):
                o_ref[variant, i * GRID_W:(i + 1) * GRID_W, m * 2 * GRID_W:(m + 1) * 2 * GRID_W] = jnp.where(
                    lane < GRID_W, halves[2 * m], halves[2 * m + 1])


def _na_bias_table(rpb, n_rows, n_ctx):
    n_heads = rpb.shape[0]
    assert n_rows % NA_Q_ROWS == 0 and n_rows // NA_Q_ROWS >= 3 and (3 * NA_Q_ROWS) % 2 == 0
    shape = (3, n_heads, NA_Q_ROWS * GRID_W, 3 * NA_Q_ROWS * GRID_W + n_ctx)
    return pl.pallas_call(
        functools.partial(_na_bias_kernel, n_rows=n_rows),
        grid=(n_heads,),
        in_specs=[pl.BlockSpec(memory_space=pltpu.SMEM)],
        out_specs=pl.BlockSpec((3, None) + shape[2:], lambda h: (0, h, 0, 0)),
        out_shape=jax.ShapeDtypeStruct(shape, F32),
        scratch_shapes=[pltpu.VMEM((2 * NA_WIN_H - 1, GRID_W, 2 * GRID_W), F32)],
        compiler_params=_params(("parallel",), 24 << 20),
        name="na_bias_table",
    )(rpb.astype(F32).reshape(-1))


def _neighbourhood_attention(q, k, v, kx, vx, bias):
    b, s, w = q.shape
    blk = NA_Q_ROWS * GRID_W
    nj = s // blk
    cx = kx.shape[1]
    cur = lambda bb, j: (bb, j, 0)
    prev = lambda bb, j: (bb, jnp.maximum(j - 1, 0), 0)
    nxt = lambda bb, j: (bb, jnp.minimum(j + 1, nj - 1), 0)
    edge = lambda bb, j: (jnp.where(j == 0, 0, jnp.where(j == nj - 1, 2, 1)), 0, 0, 0)
    tok = lambda im: pl.BlockSpec((None, blk, w), im)
    ctx = pl.BlockSpec((None, cx, w), lambda bb, j: (bb, 0, 0))
    vmem = 2 * bias[0].size * 4 + 6 * blk * bias.shape[3] * 4 + (8 << 20)
    return pl.pallas_call(
        _na_kernel,
        grid=(b, nj),
        in_specs=[tok(cur), tok(prev), tok(cur), tok(nxt), tok(prev), tok(cur), tok(nxt), ctx, ctx,
                  pl.BlockSpec((None,) + bias.shape[1:], edge)],
        out_specs=tok(cur),
        out_shape=jax.ShapeDtypeStruct((b, s, w), BF16),
        compiler_params=_params(("parallel", "parallel"), vmem),
        name="neighbourhood_attention",
    )(q, k, k, k, v, v, v, kx, vx, bias)


def _wa_band_bias(n_ctx):
    blk = WA_BLOCK
    p = np.arange(blk)[:, None]
    k = np.arange(3 * blk + n_ctx)[None, :]
    in_prev, in_next = k < blk, (k >= 2 * blk) & (k < 3 * blk)
    tables = []
    for has_prev, has_next in ((False, True), (True, True), (True, False)):
        ok_prev = (k >= p) & has_prev
        ok_next = (k - 2 * blk <= p) & has_next
        valid = np.where(in_prev, ok_prev, np.where(in_next, ok_next, True))
        tables.append(np.where(valid, 0.0, NEG_INF).astype(np.float32))
    return np.stack(tables)


def _wa_kernel(sink_ref, band_ref, q_ref, kp_ref, kc_ref, kn_ref, vp_ref, vc_ref, vn_ref, kx_ref, vx_ref,
               o_ref, *, group):
    blk = WA_BLOCK
    n_kv = kc_ref.shape[1] // HEAD_DIM
    band = jnp.concatenate([band_ref[...]] * group, axis=0)
    for kv in range(n_kv):
        hs = slice(kv * HEAD_DIM, (kv + 1) * HEAD_DIM)
        heads = [kv * group + g for g in range(group)]
        q = jnp.concatenate([q_ref[:, h * HEAD_DIM:(h + 1) * HEAD_DIM] for h in heads], axis=0)
        k = jnp.concatenate([kp_ref[:, hs], kc_ref[:, hs], kn_ref[:, hs], kx_ref[:, hs]], axis=0)
        v = jnp.concatenate([vp_ref[:, hs], vc_ref[:, hs], vn_ref[:, hs], vx_ref[:, hs]], axis=0)
        s = _qk(q, k) + band
        sink = jnp.concatenate([jnp.full((blk, 1), sink_ref[h], F32) for h in heads], axis=0)
        out = _softmax_pv([s], [v], extra_logit=sink)
        for g, h in enumerate(heads):
            o_ref[:, h * HEAD_DIM:(h + 1) * HEAD_DIM] = out[g * blk:(g + 1) * blk, :].astype(o_ref.dtype)


def _window_attention(q, k, v, kx, vx, sink):
    b, s, wq = q.shape
    wk = k.shape[2]
    group = wq // wk
    nblk = s // WA_BLOCK
    cx = kx.shape[1]
    cur = lambda bb, j: (bb, j, 0)
    prev = lambda bb, j: (bb, jnp.maximum(j - 1, 0), 0)
    nxt = lambda bb, j: (bb, jnp.minimum(j + 1, nblk - 1), 0)
    kvs = lambda im: pl.BlockSpec((None, WA_BLOCK, wk), im)
    ctx = pl.BlockSpec((None, cx, wk), lambda bb, j: (bb, 0, 0))
    assert nblk >= 2
    band = jnp.asarray(_wa_band_bias(cx))
    edge = lambda bb, j: (jnp.where(j == 0, 0, jnp.where(j == nblk - 1, 2, 1)), 0, 0)
    return pl.pallas_call(
        functools.partial(_wa_kernel, group=group),
        grid=(b, nblk),
        in_specs=[pl.BlockSpec(memory_space=pltpu.SMEM),
                  pl.BlockSpec((None,) + band.shape[1:], edge),
                  pl.BlockSpec((None, WA_BLOCK, wq), cur),
                  kvs(prev), kvs(cur), kvs(nxt), kvs(prev), kvs(cur), kvs(nxt), ctx, ctx],
        out_specs=pl.BlockSpec((None, WA_BLOCK, wq), cur),
        out_shape=jax.ShapeDtypeStruct((b, s, wq), BF16),
        compiler_params=_params(("parallel", "parallel"), 24 << 20),
        name="window_attention",
    )(sink, band, q, k, k, k, v, v, v, kx, vx)


def _ctx_attn_kernel(*refs, group, use_sink):
    if use_sink:
        sink_ref, q_ref, k_ref, v_ref, o_ref = refs
    else:
        q_ref, k_ref, v_ref, o_ref = refs
    c = q_ref.shape[0]
    for kv in range(k_ref.shape[1] // HEAD_DIM):
        hs = slice(kv * HEAD_DIM, (kv + 1) * HEAD_DIM)
        heads = [kv * group + g for g in range(group)]
        q = jnp.concatenate([q_ref[:, h * HEAD_DIM:(h + 1) * HEAD_DIM] for h in heads], axis=0)
        sink = None
        if use_sink:
            sink = jnp.concatenate([jnp.full((c, 1), sink_ref[h], F32) for h in heads], axis=0)
        out = _softmax_pv([_qk(q, k_ref[:, hs])], [v_ref[:, hs]], extra_logit=sink)
        for g, h in enumerate(heads):
            o_ref[:, h * HEAD_DIM:(h + 1) * HEAD_DIM] = out[g * c:(g + 1) * c, :].astype(o_ref.dtype)


def _context_attention(q, k, v, sink):
    b, c, wq = q.shape
    wk = k.shape[2]
    use_sink = sink is not None
    spec = lambda w: pl.BlockSpec((None, c, w), lambda bb: (bb, 0, 0))
    in_specs = [spec(wq), spec(wk), spec(wk)]
    args = [q, k, v]
    if use_sink:
        in_specs = [pl.BlockSpec(memory_space=pltpu.SMEM)] + in_specs
        args = [sink] + args
    return pl.pallas_call(
        functools.partial(_ctx_attn_kernel, group=wq // wk, use_sink=use_sink),
        grid=(b,),
        in_specs=in_specs,
        out_specs=spec(wq),
        out_shape=jax.ShapeDtypeStruct((b, c, wq), BF16),
        compiler_params=_params(("parallel",), 24 << 20),
        name="context_attention",
    )(*args)


def _out_kernel(x_ref, mod_ref, a_ref, u_ref, up_ref, un_ref, c_ref, pw_ref, ps_ref, wo_ref, o_ref, ext_ref,
                *, seq_len):
    tm = x_ref.shape[0]
    tiles_per_seq = seq_len // tm
    t_in_seq = pl.program_id(0) % tiles_per_seq
    halo = POOL_HALO
    has_prev = (t_in_seq > 0).astype(F32)
    has_next = (t_in_seq < tiles_per_seq - 1).astype(F32)
    ext_ref[0:halo, :] = up_ref[...] * has_prev
    ext_ref[halo:halo + tm, :] = u_ref[...]
    ext_ref[halo + tm:, :] = un_ref[...] * has_next

    pos = t_in_seq * tm + lax.broadcasted_iota(jnp.int32, (tm, 1), 0)
    parts = []
    for g, win in enumerate(POOL_WINDOWS):
        ls = slice(g * HEAD_DIM, (g + 1) * HEAD_DIM)
        total = functools.reduce(jnp.add, [ext_ref[halo + d:halo + d + tm, ls]
                                           for d in range(-(win // 2), win - win // 2)])
        lo = jnp.maximum(pos - win // 2, 0)
        hi = jnp.minimum(pos + (win - win // 2 - 1), seq_len - 1)
        delta = total / (hi - lo + 1).astype(F32) - u_ref[:, ls]
        y = jnp.dot(delta.astype(BF16), pw_ref[g], preferred_element_type=F32) * ps_ref[:, ls]
        parts.append(y.astype(BF16))
    na_w = a_ref.shape[1]
    pool_w = len(POOL_WINDOWS) * HEAD_DIM
    mixed = jnp.dot(a_ref[...], wo_ref[0:na_w, :], preferred_element_type=F32)
    mixed += jnp.dot(c_ref[...], wo_ref[na_w + pool_w:, :], preferred_element_type=F32)
    mixed += jnp.dot(jnp.concatenate(parts, axis=1), wo_ref[na_w:na_w + pool_w, :], preferred_element_type=F32)
    o_ref[...] = x_ref[...] + mod_ref[5:6, :] * mixed


def _mix_out(x2, mod, a2, u2, c2, pool_w, pool_scale, w_out, layer, seq_len, *, tm=512):
    n_tok, d = x2.shape
    tm = _row_tile(seq_len, tm)
    tiles_per_mod = n_tok // mod.shape[0] // tm
    n_halo = n_tok // POOL_HALO
    hpt = tm // POOL_HALO
    row = lambda w: pl.BlockSpec((tm, w), lambda i: (i, 0))
    vmem = 4 * tm * d * 4 + 2 * w_out[0].size * 2 + 4 * tm * d * 4 + (4 << 20)
    return pl.pallas_call(
        functools.partial(_out_kernel, seq_len=seq_len),
        grid=(n_tok // tm,),
        in_specs=[row(d),
                  pl.BlockSpec((None, N_MOD, d), lambda i: (i // tiles_per_mod, 0, 0)),
                  row(a2.shape[1]), row(u2.shape[1]),
                  pl.BlockSpec((POOL_HALO, u2.shape[1]), lambda i: (jnp.maximum(i * hpt - 1, 0), 0)),
                  pl.BlockSpec((POOL_HALO, u2.shape[1]), lambda i: (jnp.minimum((i + 1) * hpt, n_halo - 1), 0)),
                  row(c2.shape[1]),
                  pl.BlockSpec((None,) + pool_w.shape[1:], lambda i: (layer, 0, 0, 0)),
                  pl.BlockSpec((1, u2.shape[1]), lambda i: (0, 0)),
                  pl.BlockSpec((None,) + w_out.shape[1:], lambda i: (layer, 0, 0))],
        out_specs=row(d),
        out_shape=jax.ShapeDtypeStruct((n_tok, d), F32),
        scratch_shapes=[pltpu.VMEM((tm + 2 * POOL_HALO, u2.shape[1]), F32)],
        compiler_params=_params(("parallel",), vmem),
        name="pool_mix_out",
    )(x2, mod, a2, u2, u2, u2, c2, pool_w, pool_scale.reshape(1, -1), w_out)


def _rope_tables(seq_len):
    n_rows = seq_len // GRID_W
    axis_dim = HEAD_DIM // 2
    inv_freq = ROPE_BASE ** (-jnp.arange(0, axis_dim, 2, dtype=F32) / axis_dim)
    ang_r = jnp.arange(n_rows, dtype=F32)[:, None] * inv_freq[None, :]
    ang_c = jnp.arange(GRID_W, dtype=F32)[:, None] * inv_freq[None, :]
    by_row = lambda t: jnp.broadcast_to(t[:, None, :], (n_rows, GRID_W, t.shape[-1]))
    by_col = lambda t: jnp.broadcast_to(t[None, :, :], (n_rows, GRID_W, t.shape[-1]))
    cos_r, sin_r = by_row(jnp.cos(ang_r)), by_row(jnp.sin(ang_r))
    cos_c, sin_c = by_col(jnp.cos(ang_c)), by_col(jnp.sin(ang_c))
    cos = jnp.concatenate([cos_r, cos_r, cos_c, cos_c], axis=-1).reshape(seq_len, HEAD_DIM)
    sin = jnp.concatenate([-sin_r, sin_r, -sin_c, sin_c], axis=-1).reshape(seq_len, HEAD_DIM)
    return cos, sin


def kernel(x, c, ctx, c_ctx, w_mod, b_mod, norm_w, ffn1_wi, ffn1_wo, ffn2_wi, ffn2_wo, w_in, w_out,
           na_qk_gain, na_rpb, pool_w, pool_scale, wa_qk_gain, wa_sink):
    b, s, d = x.shape
    cx = ctx.shape[1]
    depth = w_mod.shape[0]
    na_w = na_rpb.shape[1] * HEAD_DIM
    pw = pool_scale.shape[1]
    kv_w = (w_in.shape[2] - 3 * na_w - pw - (d - na_w - pw)) // 2
    widths = (na_w, pw, d - na_w - pw, kv_w)

    n_cond = b + 1
    cvecs = jnp.concatenate([c, c_ctx[None, :], jnp.zeros((-n_cond % SUBLANES, d), F32)], axis=0)
    mods = _modulation(cvecs, w_mod, b_mod).reshape(depth, cvecs.shape[0], N_MOD, d)
    rope = _rope_tables(s)
    wi1, wo1, wi2, wo2 = (t.astype(BF16) for t in (ffn1_wi, ffn1_wo, ffn2_wi, ffn2_wo))
    win, wout, pwb = w_in.astype(BF16), w_out.astype(BF16), pool_w.astype(BF16)

    x2 = x.reshape(b * s, d)
    c2 = ctx.reshape(b * cx, d)
    seq = lambda t: t.reshape(b, s, -1)
    cseq = lambda t: t.reshape(b, cx, -1)
    for l in range(depth):
        last = l == depth - 1
        mx, mc = mods[l, :b], mods[l, b:b + 1]

        x2 = _ffn(x2, mx, 0, norm_w[l, 0], wi1, wo1, l)
        c2 = _ffn(c2, mc, 0, norm_w[l, 0], wi1, wo1, l)

        nq, nk, nv, u, wq, wk, wv = _projection(x2, mx, norm_w[l, 1], win, l, na_qk_gain[l], wa_qk_gain[l],
                                                rope, widths)
        cnq, cnk, cnv, cu, cwq, cwk, cwv = _projection(c2, mc, norm_w[l, 1], win, l, na_qk_gain[l],
                                                       wa_qk_gain[l], None, widths)
        a_out = _neighbourhood_attention(seq(nq), seq(nk), seq(nv), cseq(cnk), cseq(cnv),
                                         _na_bias_table(na_rpb[l], s // GRID_W, cx))
        c_out = _window_attention(seq(wq), seq(wk), seq(wv), cseq(cwk), cseq(cwv), wa_sink[l])
        x2 = _mix_out(x2, mx, a_out.reshape(b * s, -1), u, c_out.reshape(b * s, -1), pwb, pool_scale[l],
                      wout, l, s)
        if not last:
            ca = _context_attention(cseq(cnq), cseq(cnk), cseq(cnv), None)
            cc = _context_attention(cseq(cwq), cseq(cwk), cseq(cwv), wa_sink[l])
            c2 = _mix_out(c2, mc, ca.reshape(b * cx, -1), cu, cc.reshape(b * cx, -1), pwb, pool_scale[l],
                          wout, l, cx)
            c2 = _ffn(c2, mc, 6, norm_w[l, 2], wi2, wo2, l)
        x2 = _ffn(x2, mx, 6, norm_w[l, 2], wi2, wo2, l)
    return x2.reshape(b, s, d)
```

```python
import functools

import numpy as np
import jax
import jax.numpy as jnp
from jax import lax
from jax.experimental import pallas as pl
from jax.experimental.pallas import tpu as pltpu

HEAD_DIM = 128
GRID_W = 64
NA_WIN_H = 8
NA_WIN_W = 16
POOL_WINDOWS = (2, 4, 8, 16)
WA_BLOCK = 128
N_MOD = 9
MACARON_WEIGHT = 0.5
ROPE_BASE = 10000.0
RMS_EPS = 1e-6
NEG_INF = -1e30
ATTN_SCALE = HEAD_DIM ** -0.5

V7X_VMEM_BYTES = 64 * 1024 * 1024
V7X_VMEM_BUDGET = 56 * 1024 * 1024
SUBLANES = 8
NA_Q_ROWS = NA_WIN_H // 2
POOL_HALO = 16

BF16 = jnp.bfloat16
F32 = jnp.float32


def _params(semantics, vmem_bytes):
    return pltpu.CompilerParams(dimension_semantics=semantics,
                                vmem_limit_bytes=int(min(max(vmem_bytes, 16 << 20), V7X_VMEM_BUDGET)))


def _row_tile(n_rows, want):
    t = min(want, n_rows)
    assert n_rows % t == 0, (n_rows, t)
    return t


def _mod_norm(x, mod_ref, g_ref, first):
    y = x * lax.rsqrt(jnp.mean(x * x, axis=-1, keepdims=True) + RMS_EPS) * g_ref[...]
    return y * (1.0 + mod_ref[first + 1:first + 2, :]) + mod_ref[first:first + 1, :]


def _mod_kernel(c_ref, w_ref, b_ref, o_ref):
    c = c_ref[...]
    s = (c * jax.nn.sigmoid(c)).astype(BF16)
    o_ref[...] = jnp.dot(s, w_ref[...].astype(BF16), preferred_element_type=F32) + b_ref[...]


def _modulation(cvecs, w_mod, b_mod):
    n_layers, d, nd = w_mod.shape
    rows = cvecs.shape[0]
    tn = _row_tile(nd, 2048)
    return pl.pallas_call(
        _mod_kernel,
        grid=(n_layers, nd // tn),
        in_specs=[pl.BlockSpec((rows, d), lambda l, n: (0, 0)),
                  pl.BlockSpec((None, d, tn), lambda l, n: (l, 0, n)),
                  pl.BlockSpec((None, 1, tn), lambda l, n: (l, 0, n))],
        out_specs=pl.BlockSpec((None, rows, tn), lambda l, n: (l, 0, n)),
        out_shape=jax.ShapeDtypeStruct((n_layers, rows, nd), F32),
        compiler_params=_params(("parallel", "parallel"), 2 * d * tn * 4 + (8 << 20)),
        name="adaln_modulation",
    )(cvecs, w_mod, b_mod.reshape(n_layers, 1, nd))


def _ffn_kernel(x_hbm, mod_ref, g_ref, wa_ref, wb_ref, wo_ref, o_hbm,
                xa_ref, xb_ref, ha_ref, hb_ref, in_sem, out_sem, *, first, tiles_per_mod, n_chunks):
    i, f = pl.program_id(0), pl.program_id(1)
    n_tiles, nf = pl.num_programs(0), pl.num_programs(1)
    tm = xa_ref.shape[0]
    rows = tm // n_chunks
    head = tm - rows

    def x_in(tile, dst, part, slot):
        r0, n = ((0, head), (head, rows))[part]
        return pltpu.make_async_copy(x_hbm.at[pl.ds(tile * tm + r0, n), :], dst.at[pl.ds(r0, n), :],
                                     in_sem.at[slot, part])

    def x_out(tile, src, slot):
        return pltpu.make_async_copy(src, o_hbm.at[pl.ds(tile * tm, tm), :], out_sem.at[slot])

    def normed(x_rows, batch):
        return _mod_norm(x_rows, mod_ref.at[batch], g_ref, first).astype(BF16)

    def tile_step(slot, x_cur, x_nxt, h_cur, h_nxt):
        other = 1 - slot
        has_next = i + 1 < n_tiles

        if slot == 0:
            @pl.when((i == 0) & (f == 0))
            def _():
                for part in (0, 1):
                    x_in(0, x_cur, part, slot).start()
                x_nxt[...] = jnp.zeros(x_nxt.shape, x_nxt.dtype)
                for part in (0, 1):
                    x_in(0, x_cur, part, slot).wait()
                h_cur[...] = normed(x_cur[...], 0)

        @pl.when((f == 1) & (i > 0))
        def _():
            x_out(i - 1, x_nxt, other).wait()

        @pl.when((f == 1) & has_next)
        def _():
            x_in(i + 1, x_nxt, 0, other).start()

        @pl.when((f == 2) & has_next)
        def _():
            x_in(i + 1, x_nxt, 0, other).wait()
            x_in(i + 1, x_nxt, 1, other).start()

        @pl.when((f == 3) & has_next)
        def _():
            x_in(i + 1, x_nxt, 1, other).wait()

        chunk = jnp.where((f >= 2) & (f < 2 + n_chunks), f - 2, n_chunks - 1)
        r0 = pl.multiple_of(chunk * rows, rows)
        next_batch = jnp.minimum(i + 1, n_tiles - 1) // tiles_per_mod
        h_nxt[pl.ds(r0, rows), :] = normed(x_nxt[pl.ds(r0, rows), :], next_batch)

        gate = MACARON_WEIGHT * mod_ref[i // tiles_per_mod, first + 2:first + 3, :]
        h = h_cur[...]
        a = jnp.dot(h, wa_ref[...], preferred_element_type=F32)
        b = jnp.dot(h, wb_ref[...], preferred_element_type=F32)
        act = (a * jax.nn.sigmoid(a) * b).astype(BF16)
        x_cur[...] += gate * jnp.dot(act, wo_ref[...], preferred_element_type=F32)

        @pl.when(f == nf - 1)
        def _():
            x_out(i, x_cur, slot).start()

        @pl.when((f == nf - 1) & (i == n_tiles - 1))
        def _():
            x_out(i, x_cur, slot).wait()

    @pl.when(i % 2 == 0)
    def _():
        tile_step(0, xa_ref, xb_ref, ha_ref, hb_ref)

    @pl.when(i % 2 == 1)
    def _():
        tile_step(1, xb_ref, xa_ref, hb_ref, ha_ref)


def _ffn(x2, mod, first, norm_g, wi, wo, layer, *, tm=1024, tf=512):
    n_tok, d = x2.shape
    d_ff = wo.shape[1]
    tm = _row_tile(n_tok // mod.shape[0], tm)
    tf = _row_tile(d_ff, tf)
    tiles_per_mod = n_tok // mod.shape[0] // tm
    nf = d_ff // tf
    n_chunks = 1 << ((nf - 2).bit_length() - 1)
    assert nf >= 4 and tm % (n_chunks * 2 * SUBLANES) == 0, (nf, tm)
    vmem = 2 * tm * d * 4 + 2 * tm * d * 2 + 2 * 3 * d * tf * 2 + 8 * tm * tf * 4 + (4 << 20)
    return pl.pallas_call(
        functools.partial(_ffn_kernel, first=first, tiles_per_mod=tiles_per_mod, n_chunks=n_chunks),
        grid=(n_tok // tm, nf),
        in_specs=[pl.BlockSpec(memory_space=pl.ANY),
                  pl.BlockSpec(mod.shape, lambda i, f: (0, 0, 0)),
                  pl.BlockSpec((1, d), lambda i, f: (0, 0)),
                  pl.BlockSpec((None, d, tf), lambda i, f: (layer, 0, f)),
                  pl.BlockSpec((None, d, tf), lambda i, f: (layer, 0, nf + f)),
                  pl.BlockSpec((None, tf, d), lambda i, f: (layer, f, 0))],
        out_specs=pl.BlockSpec(memory_space=pl.ANY),
        out_shape=jax.ShapeDtypeStruct((n_tok, d), F32),
        scratch_shapes=[pltpu.VMEM((tm, d), F32), pltpu.VMEM((tm, d), F32),
                        pltpu.VMEM((tm, d), BF16), pltpu.VMEM((tm, d), BF16),
                        pltpu.SemaphoreType.DMA((2, 2)), pltpu.SemaphoreType.DMA((2,))],
        compiler_params=_params(("arbitrary", "arbitrary"), vmem),
        name="macaron_ffn",
    )(x2, mod, norm_g.reshape(1, d), wi, wi, wo)


def _head_norm(p, gain):
    return p * lax.rsqrt(jnp.mean(p * p, axis=-1, keepdims=True) + RMS_EPS) * gain


def _rotate(y, cos, sin_signed):
    lane = lax.broadcasted_iota(jnp.int32, y.shape, 1)
    partner = jnp.where((lane & 32) == 0, pltpu.roll(y, HEAD_DIM - 32, 1), pltpu.roll(y, 32, 1))
    return y * cos + partner * sin_signed


def _proj_kernel(*refs, use_rope, widths):
    if use_rope:
        x_ref, mod_ref, g_ref, w_ref, nag_ref, wag_ref, cos_ref, sin_ref = refs[:8]
        outs = refs[8:]
    else:
        x_ref, mod_ref, g_ref, w_ref, nag_ref, wag_ref = refs[:6]
        outs = refs[6:]
    nq_ref, nk_ref, nv_ref, u_ref, wq_ref, wk_ref, wv_ref = outs
    na_w, pool_w, wa_w, kv_w = widths
    h = _mod_norm(x_ref[...], mod_ref, g_ref, 3).astype(BF16)

    def chunk(col0, width):
        return jnp.dot(h, w_ref[:, col0:col0 + width], preferred_element_type=F32)

    def normed(p, gain, out_ref, out_col0, scale, rope):
        for j in range(p.shape[1] // HEAD_DIM):
            y = _head_norm(p[:, j * HEAD_DIM:(j + 1) * HEAD_DIM], gain)
            if rope:
                y = _rotate(y, cos_ref[...], sin_ref[...])
            if scale != 1.0:
                y = y * scale
            c0 = out_col0 + j * HEAD_DIM
            out_ref[:, c0:c0 + HEAD_DIM] = y.astype(out_ref.dtype)

    c_nq, c_nk, c_nv, c_u = 0, na_w, 2 * na_w, 3 * na_w
    c_wq = c_u + pool_w
    c_wk = c_wq + wa_w
    c_wv = c_wk + kv_w
    half = wa_w // 2
    normed(chunk(c_wq, half), wag_ref[0:1, :], wq_ref, 0, ATTN_SCALE, use_rope)
    normed(chunk(c_wq + half, half), wag_ref[0:1, :], wq_ref, half, ATTN_SCALE, use_rope)
    normed(chunk(c_wk, kv_w), wag_ref[1:2, :], wk_ref, 0, 1.0, use_rope)
    normed(chunk(c_nq, na_w), nag_ref[0:1, :], nq_ref, 0, ATTN_SCALE, False)
    normed(chunk(c_nk, na_w), nag_ref[1:2, :], nk_ref, 0, 1.0, False)
    nv_ref[...] = chunk(c_nv, na_w).astype(BF16)
    u_ref[...] = chunk(c_u, pool_w)
    wv_ref[...] = chunk(c_wv, kv_w).astype(BF16)


def _projection(x2, mod, norm_g, w_in, layer, na_gain, wa_gain, rope, widths, *, tm=512):
    n_tok, d = x2.shape
    na_w, pool_w, wa_w, kv_w = widths
    n_in = w_in.shape[2]
    tm = _row_tile(n_tok, tm)
    tiles_per_mod = n_tok // mod.shape[0] // tm
    use_rope = rope is not None
    in_specs = [pl.BlockSpec((tm, d), lambda i: (i, 0)),
                pl.BlockSpec((None, N_MOD, d), lambda i: (i // tiles_per_mod, 0, 0)),
                pl.BlockSpec((1, d), lambda i: (0, 0)),
                pl.BlockSpec((None, d, n_in), lambda i: (layer, 0, 0)),
                pl.BlockSpec((2, HEAD_DIM), lambda i: (0, 0)),
                pl.BlockSpec((2, HEAD_DIM), lambda i: (0, 0))]
    args = [x2, mod, norm_g.reshape(1, d), w_in, na_gain, wa_gain]
    if use_rope:
        tiles_per_seq = rope[0].shape[0] // tm
        in_specs += [pl.BlockSpec((tm, HEAD_DIM), lambda i: (i % tiles_per_seq, 0))] * 2
        args += list(rope)
    out_widths = (na_w, na_w, na_w, pool_w, wa_w, kv_w, kv_w)
    out_dtypes = (BF16, BF16, BF16, F32, BF16, BF16, BF16)
    vmem = 2 * tm * d * 4 + tm * d * 2 + 2 * d * n_in * 2 + 4 * tm * n_in * 4 + (4 << 20)
    return pl.pallas_call(
        functools.partial(_proj_kernel, use_rope=use_rope, widths=widths),
        grid=(n_tok // tm,),
        in_specs=in_specs,
        out_specs=[pl.BlockSpec((tm, w), lambda i: (i, 0)) for w in out_widths],
        out_shape=[jax.ShapeDtypeStruct((n_tok, w), dt) for w, dt in zip(out_widths, out_dtypes)],
        compiler_params=_params(("parallel",), vmem),
        name="input_projection",
    )(*args)


def _softmax_pv(scores, values, extra_logit=None):
    m = functools.reduce(jnp.maximum, [jnp.max(s, axis=-1, keepdims=True) for s in scores])
    if extra_logit is not None:
        m = jnp.maximum(m, extra_logit)
    ps = [jnp.exp(s - m) for s in scores]
    denom = functools.reduce(jnp.add, [jnp.sum(p, axis=-1, keepdims=True) for p in ps])
    if extra_logit is not None:
        denom = denom + jnp.exp(extra_logit - m)
    acc = functools.reduce(jnp.add, [jnp.dot(p.astype(BF16), v, preferred_element_type=F32)
                                     for p, v in zip(ps, values)])
    return acc / denom


def _qk(q, k):
    return lax.dot_general(q, k, (((1,), (1,)), ((), ())), preferred_element_type=F32)


def _na_kernel(q_ref, kp_ref, kc_ref, kn_ref, vp_ref, vc_ref, vn_ref, kx_ref, vx_ref, bias_ref, o_ref):
    for h in range(q_ref.shape[1] // HEAD_DIM):
        hs = slice(h * HEAD_DIM, (h + 1) * HEAD_DIM)
        k = jnp.concatenate([kp_ref[:, hs], kc_ref[:, hs], kn_ref[:, hs], kx_ref[:, hs]], axis=0)
        v = jnp.concatenate([vp_ref[:, hs], vc_ref[:, hs], vn_ref[:, hs], vx_ref[:, hs]], axis=0)
        s = _qk(q_ref[:, hs], k) + bias_ref[h]
        o_ref[:, hs] = _softmax_pv([s], [v]).astype(o_ref.dtype)


def _na_bias_kernel(rpb_ref, o_ref, g_ref, *, n_rows):
    n_ro, n_co = 2 * NA_WIN_H - 1, 2 * NA_WIN_W - 1
    shape = (GRID_W, 2 * GRID_W)
    qc = lax.broadcasted_iota(jnp.int32, shape, 0)
    lane = lax.broadcasted_iota(jnp.int32, shape, 1)
    kc = lane & (GRID_W - 1)
    c0 = jnp.clip(qc - NA_WIN_W // 2, 0, GRID_W - NA_WIN_W)
    in_cols = (kc >= c0) & (kc < c0 + NA_WIN_W)
    col_off = kc - qc + (NA_WIN_W - 1)
    base = pl.program_id(0) * (n_ro * n_co)
    for ro in range(n_ro):
        g = jnp.zeros(shape, F32)
        for c in range(n_co):
            g = jnp.where(col_off == c, rpb_ref[base + ro * n_co + c], g)
        g_ref[ro] = jnp.where(in_cols, g, NEG_INF)

    masked = jnp.full(shape, NEG_INF, F32)
    n_local = 3 * NA_Q_ROWS * GRID_W
    n_blocks = n_rows // NA_Q_ROWS
    o_ref[:, :, n_local:] = jnp.zeros(o_ref.shape[:2] + (o_ref.shape[2] - n_local,), F32)
    for variant, jb in enumerate((0, 1, n_blocks - 1)):
        for i in range(NA_Q_ROWS):
            r = jb * NA_Q_ROWS + i
            r0 = min(max(r - NA_WIN_H // 2, 0), n_rows - NA_WIN_H)
            halves = []
            for kr in range(3 * NA_Q_ROWS):
                kr_abs = (jb - 1) * NA_Q_ROWS + kr
                in_rows = 0 <= kr_abs < n_rows and r0 <= kr_abs < r0 + NA_WIN_H
                halves.append(g_ref[kr_abs - r + NA_WIN_H - 1] if in_rows else masked)
            for m in range(len(halves) // 2):
                o_ref[variant, i * GRID_W:(i + 1) * GRID_W, m * 2 * GRID_W:(m + 1) * 2 * GRID_W] = jnp.where(
                    lane < GRID_W, halves[2 * m], halves[2 * m + 1])


def _na_bias_table(rpb, n_rows, n_ctx):
    n_heads = rpb.shape[0]
    assert n_rows % NA_Q_ROWS == 0 and n_rows // NA_Q_ROWS >= 3 and (3 * NA_Q_ROWS) % 2 == 0
    shape = (3, n_heads, NA_Q_ROWS * GRID_W, 3 * NA_Q_ROWS * GRID_W + n_ctx)
    return pl.pallas_call(
        functools.partial(_na_bias_kernel, n_rows=n_rows),
        grid=(n_heads,),
        in_specs=[pl.BlockSpec(memory_space=pltpu.SMEM)],
        out_specs=pl.BlockSpec((3, None) + shape[2:], lambda h: (0, h, 0, 0)),
        out_shape=jax.ShapeDtypeStruct(shape, F32),
        scratch_shapes=[pltpu.VMEM((2 * NA_WIN_H - 1, GRID_W, 2 * GRID_W), F32)],
        compiler_params=_params(("parallel",), 24 << 20),
        name="na_bias_table",
    )(rpb.astype(F32).reshape(-1))


def _neighbourhood_attention(q, k, v, kx, vx, bias):
    b, s, w = q.shape
    blk = NA_Q_ROWS * GRID_W
    nj = s // blk
    cx = kx.shape[1]
    cur = lambda bb, j: (bb, j, 0)
    prev = lambda bb, j: (bb, jnp.maximum(j - 1, 0), 0)
    nxt = lambda bb, j: (bb, jnp.minimum(j + 1, nj - 1), 0)
    edge = lambda bb, j: (jnp.where(j == 0, 0, jnp.where(j == nj - 1, 2, 1)), 0, 0, 0)
    tok = lambda im: pl.BlockSpec((None, blk, w), im)
    ctx = pl.BlockSpec((None, cx, w), lambda bb, j: (bb, 0, 0))
    vmem = 2 * bias[0].size * 4 + 6 * blk * bias.shape[3] * 4 + (8 << 20)
    return pl.pallas_call(
        _na_kernel,
        grid=(b, nj),
        in_specs=[tok(cur), tok(prev), tok(cur), tok(nxt), tok(prev), tok(cur), tok(nxt), ctx, ctx,
                  pl.BlockSpec((None,) + bias.shape[1:], edge)],
        out_specs=tok(cur),
        out_shape=jax.ShapeDtypeStruct((b, s, w), BF16),
        compiler_params=_params(("parallel", "parallel"), vmem),
        name="neighbourhood_attention",
    )(q, k, k, k, v, v, v, kx, vx, bias)


def _wa_band_bias(n_ctx):
    blk = WA_BLOCK
    p = np.arange(blk)[:, None]
    k = np.arange(3 * blk + n_ctx)[None, :]
    in_prev, in_next = k < blk, (k >= 2 * blk) & (k < 3 * blk)
    tables = []
    for has_prev, has_next in ((False, True), (True, True), (True, False)):
        ok_prev = (k >= p) & has_prev
        ok_next = (k - 2 * blk <= p) & has_next
        valid = np.where(in_prev, ok_prev, np.where(in_next, ok_next, True))
        tables.append(np.where(valid, 0.0, NEG_INF).astype(np.float32))
    return np.stack(tables)


def _wa_kernel(sink_ref, band_ref, q_ref, kp_ref, kc_ref, kn_ref, vp_ref, vc_ref, vn_ref, kx_ref, vx_ref,
               o_ref, *, group):
    blk = WA_BLOCK
    n_kv = kc_ref.shape[1] // HEAD_DIM
    band = jnp.concatenate([band_ref[...]] * group, axis=0)
    for kv in range(n_kv):
        hs = slice(kv * HEAD_DIM, (kv + 1) * HEAD_DIM)
        heads = [kv * group + g for g in range(group)]
        q = jnp.concatenate([q_ref[:, h * HEAD_DIM:(h + 1) * HEAD_DIM] for h in heads], axis=0)
        k = jnp.concatenate([kp_ref[:, hs], kc_ref[:, hs], kn_ref[:, hs], kx_ref[:, hs]], axis=0)
        v = jnp.concatenate([vp_ref[:, hs], vc_ref[:, hs], vn_ref[:, hs], vx_ref[:, hs]], axis=0)
        s = _qk(q, k) + band
        sink = jnp.concatenate([jnp.full((blk, 1), sink_ref[h], F32) for h in heads], axis=0)
        out = _softmax_pv([s], [v], extra_logit=sink)
        for g, h in enumerate(heads):
            o_ref[:, h * HEAD_DIM:(h + 1) * HEAD_DIM] = out[g * blk:(g + 1) * blk, :].astype(o_ref.dtype)


def _window_attention(q, k, v, kx, vx, sink):
    b, s, wq = q.shape
    wk = k.shape[2]
    group = wq // wk
    nblk = s // WA_BLOCK
    cx = kx.shape[1]
    cur = lambda bb, j: (bb, j, 0)
    prev = lambda bb, j: (bb, jnp.maximum(j - 1, 0), 0)
    nxt = lambda bb, j: (bb, jnp.minimum(j + 1, nblk - 1), 0)
    kvs = lambda im: pl.BlockSpec((None, WA_BLOCK, wk), im)
    ctx = pl.BlockSpec((None, cx, wk), lambda bb, j: (bb, 0, 0))
    assert nblk >= 2
    band = jnp.asarray(_wa_band_bias(cx))
    edge = lambda bb, j: (jnp.where(j == 0, 0, jnp.where(j == nblk - 1, 2, 1)), 0, 0)
    return pl.pallas_call(
        functools.partial(_wa_kernel, group=group),
        grid=(b, nblk),
        in_specs=[pl.BlockSpec(memory_space=pltpu.SMEM),
                  pl.BlockSpec((None,) + band.shape[1:], edge),
                  pl.BlockSpec((None, WA_BLOCK, wq), cur),
                  kvs(prev), kvs(cur), kvs(nxt), kvs(prev), kvs(cur), kvs(nxt), ctx, ctx],
        out_specs=pl.BlockSpec((None, WA_BLOCK, wq), cur),
        out_shape=jax.ShapeDtypeStruct((b, s, wq), BF16),
        compiler_params=_params(("parallel", "parallel"), 24 << 20),
        name="window_attention",
    )(sink, band, q, k, k, k, v, v, v, kx, vx)


def _ctx_attn_kernel(*refs, group, use_sink):
    if use_sink:
        sink_ref, q_ref, k_ref, v_ref, o_ref = refs
    else:
        q_ref, k_ref, v_ref, o_ref = refs
    c = q_ref.shape[0]
    for kv in range(k_ref.shape[1] // HEAD_DIM):
        hs = slice(kv * HEAD_DIM, (kv + 1) * HEAD_DIM)
        heads = [kv * group + g for g in range(group)]
        q = jnp.concatenate([q_ref[:, h * HEAD_DIM:(h + 1) * HEAD_DIM] for h in heads], axis=0)
        sink = None
        if use_sink:
            sink = jnp.concatenate([jnp.full((c, 1), sink_ref[h], F32) for h in heads], axis=0)
        out = _softmax_pv([_qk(q, k_ref[:, hs])], [v_ref[:, hs]], extra_logit=sink)
        for g, h in enumerate(heads):
            o_ref[:, h * HEAD_DIM:(h + 1) * HEAD_DIM] = out[g * c:(g + 1) * c, :].astype(o_ref.dtype)


def _context_attention(q, k, v, sink):
    b, c, wq = q.shape
    wk = k.shape[2]
    use_sink = sink is not None
    spec = lambda w: pl.BlockSpec((None, c, w), lambda bb: (bb, 0, 0))
    in_specs = [spec(wq), spec(wk), spec(wk)]
    args = [q, k, v]
    if use_sink:
        in_specs = [pl.BlockSpec(memory_space=pltpu.SMEM)] + in_specs
        args = [sink] + args
    return pl.pallas_call(
        functools.partial(_ctx_attn_kernel, group=wq // wk, use_sink=use_sink),
        grid=(b,),
        in_specs=in_specs,
        out_specs=spec(wq),
        out_shape=jax.ShapeDtypeStruct((b, c, wq), BF16),
        compiler_params=_params(("parallel",), 24 << 20),
        name="context_attention",
    )(*args)


def _out_kernel(x_ref, mod_ref, a_ref, u_ref, up_ref, un_ref, c_ref, pw_ref, ps_ref, wo_ref, o_ref, ext_ref,
                *, seq_len):
    tm = x_ref.shape[0]
    tiles_per_seq = seq_len // tm
    t_in_seq = pl.program_id(0) % tiles_per_seq
    halo = POOL_HALO
    has_prev = (t_in_seq > 0).astype(F32)
    has_next = (t_in_seq < tiles_per_seq - 1).astype(F32)
    ext_ref[0:halo, :] = up_ref[...] * has_prev
    ext_ref[halo:halo + tm, :] = u_ref[...]
    ext_ref[halo + tm:, :] = un_ref[...] * has_next

    pos = t_in_seq * tm + lax.broadcasted_iota(jnp.int32, (tm, 1), 0)
    parts = []
    for g, win in enumerate(POOL_WINDOWS):
        ls = slice(g * HEAD_DIM, (g + 1) * HEAD_DIM)
        total = functools.reduce(jnp.add, [ext_ref[halo + d:halo + d + tm, ls]
                                           for d in range(-(win // 2), win - win // 2)])
        lo = jnp.maximum(pos - win // 2, 0)
        hi = jnp.minimum(pos + (win - win // 2 - 1), seq_len - 1)
        delta = total / (hi - lo + 1).astype(F32) - u_ref[:, ls]
        y = jnp.dot(delta.astype(BF16), pw_ref[g], preferred_element_type=F32) * ps_ref[:, ls]
        parts.append(y.astype(BF16))
    na_w = a_ref.shape[1]
    pool_w = len(POOL_WINDOWS) * HEAD_DIM
    mixed = jnp.dot(a_ref[...], wo_ref[0:na_w, :], preferred_element_type=F32)
    mixed += jnp.dot(c_ref[...], wo_ref[na_w + pool_w:, :], preferred_element_type=F32)
    mixed += jnp.dot(jnp.concatenate(parts, axis=1), wo_ref[na_w:na_w + pool_w, :], preferred_element_type=F32)
    o_ref[...] = x_ref[...] + mod_ref[5:6, :] * mixed


def _mix_out(x2, mod, a2, u2, c2, pool_w, pool_scale, w_out, layer, seq_len, *, tm=512):
    n_tok, d = x2.shape
    tm = _row_tile(seq_len, tm)
    tiles_per_mod = n_tok // mod.shape[0] // tm
    n_halo = n_tok // POOL_HALO
    hpt = tm // POOL_HALO
    row = lambda w: pl.BlockSpec((tm, w), lambda i: (i, 0))
    vmem = 4 * tm * d * 4 + 2 * w_out[0].size * 2 + 4 * tm * d * 4 + (4 << 20)
    return pl.pallas_call(
        functools.partial(_out_kernel, seq_len=seq_len),
        grid=(n_tok // tm,),
        in_specs=[row(d),
                  pl.BlockSpec((None, N_MOD, d), lambda i: (i // tiles_per_mod, 0, 0)),
                  row(a2.shape[1]), row(u2.shape[1]),
                  pl.BlockSpec((POOL_HALO, u2.shape[1]), lambda i: (jnp.maximum(i * hpt - 1, 0), 0)),
                  pl.BlockSpec((POOL_HALO, u2.shape[1]), lambda i: (jnp.minimum((i + 1) * hpt, n_halo - 1), 0)),
                  row(c2.shape[1]),
                  pl.BlockSpec((None,) + pool_w.shape[1:], lambda i: (layer, 0, 0, 0)),
                  pl.BlockSpec((1, u2.shape[1]), lambda i: (0, 0)),
                  pl.BlockSpec((None,) + w_out.shape[1:], lambda i: (layer, 0, 0))],
        out_specs=row(d),
        out_shape=jax.ShapeDtypeStruct((n_tok, d), F32),
        scratch_shapes=[pltpu.VMEM((tm + 2 * POOL_HALO, u2.shape[1]), F32)],
        compiler_params=_params(("parallel",), vmem),
        name="pool_mix_out",
    )(x2, mod, a2, u2, u2, u2, c2, pool_w, pool_scale.reshape(1, -1), w_out)


def _rope_tables(seq_len):
    n_rows = seq_len // GRID_W
    axis_dim = HEAD_DIM // 2
    inv_freq = ROPE_BASE ** (-jnp.arange(0, axis_dim, 2, dtype=F32) / axis_dim)
    ang_r = jnp.arange(n_rows, dtype=F32)[:, None] * inv_freq[None, :]
    ang_c = jnp.arange(GRID_W, dtype=F32)[:, None] * inv_freq[None, :]
    by_row = lambda t: jnp.broadcast_to(t[:, None, :], (n_rows, GRID_W, t.shape[-1]))
    by_col = lambda t: jnp.broadcast_to(t[None, :, :], (n_rows, GRID_W, t.shape[-1]))
    cos_r, sin_r = by_row(jnp.cos(ang_r)), by_row(jnp.sin(ang_r))
    cos_c, sin_c = by_col(jnp.cos(ang_c)), by_col(jnp.sin(ang_c))
    cos = jnp.concatenate([cos_r, cos_r, cos_c, cos_c], axis=-1).reshape(seq_len, HEAD_DIM)
    sin = jnp.concatenate([-sin_r, sin_r, -sin_c, sin_c], axis=-1).reshape(seq_len, HEAD_DIM)
    return cos, sin


def kernel(x, c, ctx, c_ctx, w_mod, b_mod, norm_w, ffn1_wi, ffn1_wo, ffn2_wi, ffn2_wo, w_in, w_out,
           na_qk_gain, na_rpb, pool_w, pool_scale, wa_qk_gain, wa_sink):
    b, s, d = x.shape
    cx = ctx.shape[1]
    depth = w_mod.shape[0]
    na_w = na_rpb.shape[1] * HEAD_DIM
    pw = pool_scale.shape[1]
    kv_w = (w_in.shape[2] - 3 * na_w - pw - (d - na_w - pw)) // 2
    widths = (na_w, pw, d - na_w - pw, kv_w)

    n_cond = b + 1
    cvecs = jnp.concatenate([c, c_ctx[None, :], jnp.zeros((-n_cond % SUBLANES, d), F32)], axis=0)
    mods = _modulation(cvecs, w_mod, b_mod).reshape(depth, cvecs.shape[0], N_MOD, d)
    rope = _rope_tables(s)
    wi1, wo1, wi2, wo2 = (t.astype(BF16) for t in (ffn1_wi, ffn1_wo, ffn2_wi, ffn2_wo))
    win, wout, pwb = w_in.astype(BF16), w_out.astype(BF16), pool_w.astype(BF16)

    x2 = x.reshape(b * s, d)
    c2 = ctx.reshape(b * cx, d)
    seq = lambda t: t.reshape(b, s, -1)
    cseq = lambda t: t.reshape(b, cx, -1)
    for l in range(depth):
        last = l == depth - 1
        mx, mc = mods[l, :b], mods[l, b:b + 1]

        x2 = _ffn(x2, mx, 0, norm_w[l, 0], wi1, wo1, l)
        c2 = _ffn(c2, mc, 0, norm_w[l, 0], wi1, wo1, l)

        nq, nk, nv, u, wq, wk, wv = _projection(x2, mx, norm_w[l, 1], win, l, na_qk_gain[l], wa_qk_gain[l],
                                                rope, widths)
        cnq, cnk, cnv, cu, cwq, cwk, cwv = _projection(c2, mc, norm_w[l, 1], win, l, na_qk_gain[l],
                                                       wa_qk_gain[l], None, widths)
        a_out = _neighbourhood_attention(seq(nq), seq(nk), seq(nv), cseq(cnk), cseq(cnv),
                                         _na_bias_table(na_rpb[l], s // GRID_W, cx))
        c_out = _window_attention(seq(wq), seq(wk), seq(wv), cseq(cwk), cseq(cwv), wa_sink[l])
        x2 = _mix_out(x2, mx, a_out.reshape(b * s, -1), u, c_out.reshape(b * s, -1), pwb, pool_scale[l],
                      wout, l, s)
        if not last:
            ca = _context_attention(cseq(cnq), cseq(cnk), cseq(cnv), None)
            cc = _context_attention(cseq(cwq), cseq(cwk), cseq(cwv), wa_sink[l])
            c2 = _mix_out(c2, mc, ca.reshape(b * cx, -1), cu, cc.reshape(b * cx, -1), pwb, pool_scale[l],
                          wout, l, cx)
            c2 = _ffn(c2, mc, 6, norm_w[l, 2], wi2, wo2, l)
        x2 = _ffn(x2, mx, 6, norm_w[l, 2], wi2, wo2, l)
    return x2.reshape(b, s, d)
```

```python
import functools

import numpy as np
import jax
import jax.numpy as jnp
from jax import lax
from jax.experimental import pallas as pl
from jax.experimental.pallas import tpu as pltpu

HEAD_DIM = 128
GRID_W = 64
NA_WIN_H = 8
NA_WIN_W = 16
POOL_WINDOWS = (2, 4, 8, 16)
WA_BLOCK = 128
WA_Q_BLOCKS = 1
N_MOD = 9
MACARON_WEIGHT = 0.5
ROPE_BASE = 10000.0
RMS_EPS = 1e-6
NEG_INF = -1e30
LOG2E = 1.4426950408889634
Q_SCALE = HEAD_DIM ** -0.5 * LOG2E

V7X_VMEM_BYTES = 64 * 1024 * 1024
V7X_VMEM_BUDGET = 56 * 1024 * 1024
SUBLANES = 8
NA_Q_ROWS = NA_WIN_H // 2
POOL_HALO = 16

BF16 = jnp.bfloat16
F32 = jnp.float32


def _params(semantics, vmem_bytes):
    return pltpu.CompilerParams(dimension_semantics=semantics,
                                vmem_limit_bytes=int(min(max(vmem_bytes, 16 << 20), V7X_VMEM_BUDGET)))


def _row_tile(n_rows, want):
    t = min(want, n_rows)
    assert n_rows % t == 0, (n_rows, t)
    return t


def _mod_norm(x, mod_ref, g_ref, first):
    y = x * lax.rsqrt(jnp.mean(x * x, axis=-1, keepdims=True) + RMS_EPS) * g_ref[...]
    return y * (1.0 + mod_ref[first + 1:first + 2, :]) + mod_ref[first:first + 1, :]


def _mod_kernel(c_ref, w_ref, b_ref, o_ref):
    c = c_ref[...]
    s = (c * jax.nn.sigmoid(c)).astype(BF16)
    o_ref[...] = jnp.dot(s, w_ref[...].astype(BF16), preferred_element_type=F32) + b_ref[...]


def _modulation(cvecs, w_mod, b_mod):
    n_layers, d, nd = w_mod.shape
    rows = cvecs.shape[0]
    tn = _row_tile(nd, 2048)
    return pl.pallas_call(
        _mod_kernel,
        grid=(n_layers, nd // tn),
        in_specs=[pl.BlockSpec((rows, d), lambda l, n: (0, 0)),
                  pl.BlockSpec((None, d, tn), lambda l, n: (l, 0, n)),
                  pl.BlockSpec((None, 1, tn), lambda l, n: (l, 0, n))],
        out_specs=pl.BlockSpec((None, rows, tn), lambda l, n: (l, 0, n)),
        out_shape=jax.ShapeDtypeStruct((n_layers, rows, nd), F32),
        compiler_params=_params(("parallel", "parallel"), 2 * d * tn * 4 + (8 << 20)),
        name="adaln_modulation",
    )(cvecs, w_mod, b_mod.reshape(n_layers, 1, nd))


def _ffn_kernel(x_hbm, mod_ref, g_ref, wa_ref, wb_ref, wo_ref, o_hbm,
                xa_ref, xb_ref, ha_ref, hb_ref, in_sem, out_sem, *, first, tiles_per_mod, n_chunks):
    i, f = pl.program_id(0), pl.program_id(1)
    n_tiles, nf = pl.num_programs(0), pl.num_programs(1)
    tm = xa_ref.shape[0]
    rows = tm // n_chunks
    head = tm - rows

    def x_in(tile, dst, part, slot):
        r0, n = ((0, head), (head, rows))[part]
        return pltpu.make_async_copy(x_hbm.at[pl.ds(tile * tm + r0, n), :], dst.at[pl.ds(r0, n), :],
                                     in_sem.at[slot, part])

    def x_out(tile, src, slot):
        return pltpu.make_async_copy(src, o_hbm.at[pl.ds(tile * tm, tm), :], out_sem.at[slot])

    def normed(x_rows, batch):
        return _mod_norm(x_rows, mod_ref.at[batch], g_ref, first).astype(BF16)

    def tile_step(slot, x_cur, x_nxt, h_cur, h_nxt):
        other = 1 - slot
        has_next = i + 1 < n_tiles

        if slot == 0:
            @pl.when((i == 0) & (f == 0))
            def _():
                for part in (0, 1):
                    x_in(0, x_cur, part, slot).start()
                x_nxt[...] = jnp.zeros(x_nxt.shape, x_nxt.dtype)
                for part in (0, 1):
                    x_in(0, x_cur, part, slot).wait()
                h_cur[...] = normed(x_cur[...], 0)

        @pl.when((f == 1) & (i > 0))
        def _():
            x_out(i - 1, x_nxt, other).wait()

        @pl.when((f == 1) & has_next)
        def _():
            x_in(i + 1, x_nxt, 0, other).start()

        @pl.when((f == 2) & has_next)
        def _():
            x_in(i + 1, x_nxt, 0, other).wait()
            x_in(i + 1, x_nxt, 1, other).start()

        @pl.when((f == 3) & has_next)
        def _():
            x_in(i + 1, x_nxt, 1, other).wait()

        chunk = jnp.where((f >= 2) & (f < 2 + n_chunks), f - 2, n_chunks - 1)
        r0 = pl.multiple_of(chunk * rows, rows)
        next_batch = jnp.minimum(i + 1, n_tiles - 1) // tiles_per_mod
        h_nxt[pl.ds(r0, rows), :] = normed(x_nxt[pl.ds(r0, rows), :], next_batch)

        gate = MACARON_WEIGHT * mod_ref[i // tiles_per_mod, first + 2:first + 3, :]
        h = h_cur[...]
        a = jnp.dot(h, wa_ref[...], preferred_element_type=F32)
        b = jnp.dot(h, wb_ref[...], preferred_element_type=F32)
        act = (a * jax.nn.sigmoid(a) * b).astype(BF16)
        x_cur[...] += gate * jnp.dot(act, wo_ref[...], preferred_element_type=F32)

        @pl.when(f == nf - 1)
        def _():
            x_out(i, x_cur, slot).start()

        @pl.when((f == nf - 1) & (i == n_tiles - 1))
        def _():
            x_out(i, x_cur, slot).wait()

    @pl.when(i % 2 == 0)
    def _():
        tile_step(0, xa_ref, xb_ref, ha_ref, hb_ref)

    @pl.when(i % 2 == 1)
    def _():
        tile_step(1, xb_ref, xa_ref, hb_ref, ha_ref)


def _ffn(x2, mod, first, norm_g, wi, wo, layer, *, tm=1024, tf=512):
    n_tok, d = x2.shape
    d_ff = wo.shape[1]
    tm = _row_tile(n_tok // mod.shape[0], tm)
    tf = _row_tile(d_ff, tf)
    tiles_per_mod = n_tok // mod.shape[0] // tm
    nf = d_ff // tf
    n_chunks = 1 << ((nf - 2).bit_length() - 1)
    assert nf >= 4 and tm % (n_chunks * 2 * SUBLANES) == 0, (nf, tm)
    vmem = 2 * tm * d * 4 + 2 * tm * d * 2 + 2 * 3 * d * tf * 2 + 8 * tm * tf * 4 + (4 << 20)
    return pl.pallas_call(
        functools.partial(_ffn_kernel, first=first, tiles_per_mod=tiles_per_mod, n_chunks=n_chunks),
        grid=(n_tok // tm, nf),
        in_specs=[pl.BlockSpec(memory_space=pl.ANY),
                  pl.BlockSpec(mod.shape, lambda i, f: (0, 0, 0)),
                  pl.BlockSpec((1, d), lambda i, f: (0, 0)),
                  pl.BlockSpec((None, d, tf), lambda i, f: (layer, 0, f)),
                  pl.BlockSpec((None, d, tf), lambda i, f: (layer, 0, nf + f)),
                  pl.BlockSpec((None, tf, d), lambda i, f: (layer, f, 0))],
        out_specs=pl.BlockSpec(memory_space=pl.ANY),
        out_shape=jax.ShapeDtypeStruct((n_tok, d), F32),
        scratch_shapes=[pltpu.VMEM((tm, d), F32), pltpu.VMEM((tm, d), F32),
                        pltpu.VMEM((tm, d), BF16), pltpu.VMEM((tm, d), BF16),
                        pltpu.SemaphoreType.DMA((2, 2)), pltpu.SemaphoreType.DMA((2,))],
        compiler_params=_params(("arbitrary", "arbitrary"), vmem),
        name="macaron_ffn",
    )(x2, mod, norm_g.reshape(1, d), wi, wi, wo)


def _head_norm(p, gain):
    return p * lax.rsqrt(jnp.mean(p * p, axis=-1, keepdims=True) + RMS_EPS) * gain


def _rotate(y, cos, sin_signed):
    lane = lax.broadcasted_iota(jnp.int32, y.shape, 1)
    partner = jnp.where((lane & 32) == 0, pltpu.roll(y, HEAD_DIM - 32, 1), pltpu.roll(y, 32, 1))
    return y * cos + partner * sin_signed


def _proj_kernel(*refs, use_rope, widths):
    if use_rope:
        x_ref, mod_ref, g_ref, w_ref, nag_ref, wag_ref, cos_ref, sin_ref = refs[:8]
        outs = refs[8:]
    else:
        x_ref, mod_ref, g_ref, w_ref, nag_ref, wag_ref = refs[:6]
        outs = refs[6:]
    nq_ref, nk_ref, nv_ref, u_ref, wq_ref, wk_ref, wv_ref = outs
    na_w, pool_w, wa_w, kv_w = widths
    h = _mod_norm(x_ref[...], mod_ref, g_ref, 3).astype(BF16)

    def chunk(col0, width):
        return jnp.dot(h, w_ref[:, col0:col0 + width], preferred_element_type=F32)

    def normed(p, gain, out_ref, out_col0, scale, rope):
        for j in range(p.shape[1] // HEAD_DIM):
            y = _head_norm(p[:, j * HEAD_DIM:(j + 1) * HEAD_DIM], gain)
            if rope:
                y = _rotate(y, cos_ref[...], sin_ref[...])
            if scale != 1.0:
                y = y * scale
            c0 = out_col0 + j * HEAD_DIM
            out_ref[:, c0:c0 + HEAD_DIM] = y.astype(out_ref.dtype)

    c_nq, c_nk, c_nv, c_u = 0, na_w, 2 * na_w, 3 * na_w
    c_wq = c_u + pool_w
    c_wk = c_wq + wa_w
    c_wv = c_wk + kv_w
    half = wa_w // 2
    normed(chunk(c_wq, half), wag_ref[0:1, :], wq_ref, 0, Q_SCALE, use_rope)
    normed(chunk(c_wq + half, half), wag_ref[0:1, :], wq_ref, half, Q_SCALE, use_rope)
    normed(chunk(c_wk, kv_w), wag_ref[1:2, :], wk_ref, 0, 1.0, use_rope)
    normed(chunk(c_nq, na_w), nag_ref[0:1, :], nq_ref, 0, Q_SCALE, False)
    normed(chunk(c_nk, na_w), nag_ref[1:2, :], nk_ref, 0, 1.0, False)
    nv_ref[...] = chunk(c_nv, na_w).astype(BF16)
    u_ref[...] = chunk(c_u, pool_w)
    wv_ref[...] = chunk(c_wv, kv_w).astype(BF16)


def _projection(x2, mod, norm_g, w_in, layer, na_gain, wa_gain, rope, widths, *, tm=512):
    n_tok, d = x2.shape
    na_w, pool_w, wa_w, kv_w = widths
    n_in = w_in.shape[2]
    tm = _row_tile(n_tok, tm)
    tiles_per_mod = n_tok // mod.shape[0] // tm
    use_rope = rope is not None
    in_specs = [pl.BlockSpec((tm, d), lambda i: (i, 0)),
                pl.BlockSpec((None, N_MOD, d), lambda i: (i // tiles_per_mod, 0, 0)),
                pl.BlockSpec((1, d), lambda i: (0, 0)),
                pl.BlockSpec((None, d, n_in), lambda i: (layer, 0, 0)),
                pl.BlockSpec((2, HEAD_DIM), lambda i: (0, 0)),
                pl.BlockSpec((2, HEAD_DIM), lambda i: (0, 0))]
    args = [x2, mod, norm_g.reshape(1, d), w_in, na_gain, wa_gain]
    if use_rope:
        tiles_per_seq = rope[0].shape[0] // tm
        in_specs += [pl.BlockSpec((tm, HEAD_DIM), lambda i: (i % tiles_per_seq, 0))] * 2
        args += list(rope)
    out_widths = (na_w, na_w, na_w, pool_w, wa_w, kv_w, kv_w)
    out_dtypes = (BF16, BF16, BF16, F32, BF16, BF16, BF16)
    vmem = 2 * tm * d * 4 + tm * d * 2 + 2 * d * n_in * 2 + 4 * tm * n_in * 4 + (4 << 20)
    return pl.pallas_call(
        functools.partial(_proj_kernel, use_rope=use_rope, widths=widths),
        grid=(n_tok // tm,),
        in_specs=in_specs,
        out_specs=[pl.BlockSpec((tm, w), lambda i: (i, 0)) for w in out_widths],
        out_shape=[jax.ShapeDtypeStruct((n_tok, w), dt) for w, dt in zip(out_widths, out_dtypes)],
        compiler_params=_params(("parallel",), vmem),
        name="input_projection",
    )(*args)


def _softmax_weights(s, extra_logit=None):
    m = jnp.max(s, axis=-1, keepdims=True)
    if extra_logit is not None:
        m = jnp.maximum(m, extra_logit)
    p = jnp.exp2(s - m)
    denom = jnp.sum(p, axis=-1, keepdims=True)
    if extra_logit is not None:
        denom = denom + jnp.exp2(extra_logit - m)
    return p.astype(BF16), denom


def _attend(scores, values, extra_logits=None):
    extra_logits = extra_logits or [None] * len(scores)
    weights = [_softmax_weights(s, e) for s, e in zip(scores, extra_logits)]
    return [jnp.dot(p, v, preferred_element_type=F32) / denom for (p, denom), v in zip(weights, values)]


def _qk(q, k):
    return lax.dot_general(q, k, (((1,), (1,)), ((), ())), preferred_element_type=F32)


def _na_kernel(q_ref, kp_ref, kc_ref, kn_ref, vp_ref, vc_ref, vn_ref, kx_ref, vx_ref, bias_ref, o_ref):
    n_heads = q_ref.shape[1] // HEAD_DIM
    heads = [slice(h * HEAD_DIM, (h + 1) * HEAD_DIM) for h in range(n_heads)]
    scores, values = [], []
    for h, hs in enumerate(heads):
        k = jnp.concatenate([kp_ref[:, hs], kc_ref[:, hs], kn_ref[:, hs], kx_ref[:, hs]], axis=0)
        scores.append(_qk(q_ref[:, hs], k) + bias_ref[h])
        values.append(jnp.concatenate([vp_ref[:, hs], vc_ref[:, hs], vn_ref[:, hs], vx_ref[:, hs]], axis=0))
    for hs, out in zip(heads, _attend(scores, values)):
        o_ref[:, hs] = out.astype(o_ref.dtype)


def _na_bias_kernel(rpb_ref, o_ref, g_ref, *, n_rows):
    n_ro, n_co = 2 * NA_WIN_H - 1, 2 * NA_WIN_W - 1
    shape = (GRID_W, 2 * GRID_W)
    qc = lax.broadcasted_iota(jnp.int32, shape, 0)
    lane = lax.broadcasted_iota(jnp.int32, shape, 1)
    kc = lane & (GRID_W - 1)
    c0 = jnp.clip(qc - NA_WIN_W // 2, 0, GRID_W - NA_WIN_W)
    in_cols = (kc >= c0) & (kc < c0 + NA_WIN_W)
    col_off = kc - qc + (NA_WIN_W - 1)
    base = pl.program_id(0) * (n_ro * n_co)
    for ro in range(n_ro):
        g = jnp.zeros(shape, F32)
        for c in range(n_co):
            g = jnp.where(col_off == c, rpb_ref[base + ro * n_co + c] * LOG2E, g)
        g_ref[ro] = jnp.where(in_cols, g, NEG_INF)

    masked = jnp.full(shape, NEG_INF, F32)
    n_local = 3 * NA_Q_ROWS * GRID_W
    n_blocks = n_rows // NA_Q_ROWS
    o_ref[:, :, n_local:] = jnp.zeros(o_ref.shape[:2] + (o_ref.shape[2] - n_local,), F32)
    for variant, jb in enumerate((0, 1, n_blocks - 1)):
        for i in range(NA_Q_ROWS):
            r = jb * NA_Q_ROWS + i
            r0 = min(max(r - NA_WIN_H // 2, 0), n_rows - NA_WIN_H)
            halves = []
            for kr in range(3 * NA_Q_ROWS):
                kr_abs = (jb - 1) * NA_Q_ROWS + kr
                in_rows = 0 <= kr_abs < n_rows and r0 <= kr_abs < r0 + NA_WIN_H
                halves.append(g_ref[kr_abs - r + NA_WIN_H - 1] if in_rows else masked)
            for m in range(len(halves) // 2):
                o_ref[variant, i * GRID_W:(i + 1) * GRID_W, m * 2 * GRID_W:(m + 1) * 2 * GRID_W] = jnp.where(
                    lane < GRID_W, halves[2 * m], halves[2 * m + 1])


def _na_bias_table(rpb, n_rows, n_ctx):
    n_heads = rpb.shape[0]
    assert n_rows % NA_Q_ROWS == 0 and n_rows // NA_Q_ROWS >= 3 and (3 * NA_Q_ROWS) % 2 == 0
    shape = (3, n_heads, NA_Q_ROWS * GRID_W, 3 * NA_Q_ROWS * GRID_W + n_ctx)
    return pl.pallas_call(
        functools.partial(_na_bias_kernel, n_rows=n_rows),
        grid=(n_heads,),
        in_specs=[pl.BlockSpec(memory_space=pltpu.SMEM)],
        out_specs=pl.BlockSpec((3, None) + shape[2:], lambda h: (0, h, 0, 0)),
        out_shape=jax.ShapeDtypeStruct(shape, F32),
        scratch_shapes=[pltpu.VMEM((2 * NA_WIN_H - 1, GRID_W, 2 * GRID_W), F32)],
        compiler_params=_params(("parallel",), 24 << 20),
        name="na_bias_table",
    )(rpb.astype(F32).reshape(-1))


def _neighbourhood_attention(q, k, v, kx, vx, bias):
    b, s, w = q.shape
    blk = NA_Q_ROWS * GRID_W
    nj = s // blk
    cx = kx.shape[1]
    cur = lambda bb, j: (bb, j, 0)
    prev = lambda bb, j: (bb, jnp.maximum(j - 1, 0), 0)
    nxt = lambda bb, j: (bb, jnp.minimum(j + 1, nj - 1), 0)
    edge = lambda bb, j: (jnp.where(j == 0, 0, jnp.where(j == nj - 1, 2, 1)), 0, 0, 0)
    tok = lambda im: pl.BlockSpec((None, blk, w), im)
    ctx = pl.BlockSpec((None, cx, w), lambda bb, j: (bb, 0, 0))
    vmem = 2 * bias[0].size * 4 + 6 * blk * bias.shape[3] * 4 + (8 << 20)
    return pl.pallas_call(
        _na_kernel,
        grid=(b, nj),
        in_specs=[tok(cur), tok(prev), tok(cur), tok(nxt), tok(prev), tok(cur), tok(nxt), ctx, ctx,
                  pl.BlockSpec((None,) + bias.shape[1:], edge)],
        out_specs=tok(cur),
        out_shape=jax.ShapeDtypeStruct((b, s, w), BF16),
        compiler_params=_params(("parallel", "parallel"), vmem),
        name="neighbourhood_attention",
    )(q, k, k, k, v, v, v, kx, vx, bias)


def _wa_band_bias(q_rows, n_ctx):
    halo = WA_BLOCK
    p = np.arange(q_rows)[:, None]
    k = np.arange(q_rows + 2 * halo + n_ctx)[None, :]
    in_band = (k >= p) & (k <= p + 2 * halo)
    is_prev, is_next, is_ctx = k < halo, (k >= halo + q_rows) & (k < 2 * halo + q_rows), k >= 2 * halo + q_rows
    tables = []
    for has_prev, has_next in ((False, True), (True, True), (True, False)):
        valid = is_ctx | (in_band & (has_prev | ~is_prev) & (has_next | ~is_next))
        tables.append(np.where(valid, 0.0, NEG_INF).astype(np.float32))
    return np.stack(tables)


def _wa_kernel(sink_ref, band_ref, q_ref, kp_ref, kc_ref, kn_ref, vp_ref, vc_ref, vn_ref, kx_ref, vx_ref,
               o_ref, *, group):
    rows = q_ref.shape[0]
    n_kv = kc_ref.shape[1] // HEAD_DIM
    band = jnp.concatenate([band_ref[...]] * group, axis=0)
    scores, values, sinks = [], [], []
    for kv in range(n_kv):
        hs = slice(kv * HEAD_DIM, (kv + 1) * HEAD_DIM)
        heads = [kv * group + g for g in range(group)]
        q = jnp.concatenate([q_ref[:, h * HEAD_DIM:(h + 1) * HEAD_DIM] for h in heads], axis=0)
        k = jnp.concatenate([kp_ref[:, hs], kc_ref[:, hs], kn_ref[:, hs], kx_ref[:, hs]], axis=0)
        scores.append(_qk(q, k) + band)
        values.append(jnp.concatenate([vp_ref[:, hs], vc_ref[:, hs], vn_ref[:, hs], vx_ref[:, hs]], axis=0))
        sinks.append(jnp.concatenate([jnp.full((rows, 1), sink_ref[h] * LOG2E, F32) for h in heads], axis=0))
    for kv, out in enumerate(_attend(scores, values, sinks)):
        for g in range(group):
            h = kv * group + g
            o_ref[:, h * HEAD_DIM:(h + 1) * HEAD_DIM] = out[g * rows:(g + 1) * rows, :].astype(o_ref.dtype)


def _window_attention(q, k, v, kx, vx, sink, *, q_blocks=WA_Q_BLOCKS):
    b, s, wq = q.shape
    wk = k.shape[2]
    group = wq // wk
    rows = q_blocks * WA_BLOCK
    n_tiles = s // rows
    n_halo = s // WA_BLOCK
    cx = kx.shape[1]
    assert s % rows == 0 and n_tiles >= 2
    cur = lambda bb, j: (bb, j, 0)
    prev = lambda bb, j: (bb, jnp.maximum(j * q_blocks - 1, 0), 0)
    nxt = lambda bb, j: (bb, jnp.minimum((j + 1) * q_blocks, n_halo - 1), 0)
    halo = lambda im: pl.BlockSpec((None, WA_BLOCK, wk), im)
    tile = pl.BlockSpec((None, rows, wk), cur)
    ctx = pl.BlockSpec((None, cx, wk), lambda bb, j: (bb, 0, 0))
    band = jnp.asarray(_wa_band_bias(rows, cx))
    edge = lambda bb, j: (jnp.where(j == 0, 0, jnp.where(j == n_tiles - 1, 2, 1)), 0, 0)
    vmem = 12 * group * rows * band.shape[2] * 4 + (8 << 20)
    return pl.pallas_call(
        functools.partial(_wa_kernel, group=group),
        grid=(b, n_tiles),
        in_specs=[pl.BlockSpec(memory_space=pltpu.SMEM),
                  pl.BlockSpec((None,) + band.shape[1:], edge),
                  pl.BlockSpec((None, rows, wq), cur),
                  halo(prev), tile, halo(nxt), halo(prev), tile, halo(nxt), ctx, ctx],
        out_specs=pl.BlockSpec((None, rows, wq), cur),
        out_shape=jax.ShapeDtypeStruct((b, s, wq), BF16),
        compiler_params=_params(("parallel", "parallel"), vmem),
        name="window_attention",
    )(sink, band, q, k, k, k, v, v, v, kx, vx)


def _ctx_attn_kernel(*refs, group, use_sink):
    if use_sink:
        sink_ref, q_ref, k_ref, v_ref, o_ref = refs
    else:
        q_ref, k_ref, v_ref, o_ref = refs
    c = q_ref.shape[0]
    for kv in range(k_ref.shape[1] // HEAD_DIM):
        hs = slice(kv * HEAD_DIM, (kv + 1) * HEAD_DIM)
        heads = [kv * group + g for g in range(group)]
        q = jnp.concatenate([q_ref[:, h * HEAD_DIM:(h + 1) * HEAD_DIM] for h in heads], axis=0)
        sink = None
        if use_sink:
            sink = jnp.concatenate([jnp.full((c, 1), sink_ref[h] * LOG2E, F32) for h in heads], axis=0)
        out, = _attend([_qk(q, k_ref[:, hs])], [v_ref[:, hs]], [sink])
        for g, h in enumerate(heads):
            o_ref[:, h * HEAD_DIM:(h + 1) * HEAD_DIM] = out[g * c:(g + 1) * c, :].astype(o_ref.dtype)


def _context_attention(q, k, v, sink):
    b, c, wq = q.shape
    wk = k.shape[2]
    use_sink = sink is not None
    spec = lambda w: pl.BlockSpec((None, c, w), lambda bb: (bb, 0, 0))
    in_specs = [spec(wq), spec(wk), spec(wk)]
    args = [q, k, v]
    if use_sink:
        in_specs = [pl.BlockSpec(memory_space=pltpu.SMEM)] + in_specs
        args = [sink] + args
    return pl.pallas_call(
        functools.partial(_ctx_attn_kernel, group=wq // wk, use_sink=use_sink),
        grid=(b,),
        in_specs=in_specs,
        out_specs=spec(wq),
        out_shape=jax.ShapeDtypeStruct((b, c, wq), BF16),
        compiler_params=_params(("parallel",), 24 << 20),
        name="context_attention",
    )(*args)


def _out_kernel(x_ref, mod_ref, a_ref, u_ref, up_ref, un_ref, c_ref, pw_ref, ps_ref, wo_ref, o_ref, ext_ref,
                *, seq_len):
    tm = x_ref.shape[0]
    tiles_per_seq = seq_len // tm
    t_in_seq = pl.program_id(0) % tiles_per_seq
    halo = POOL_HALO
    has_prev = (t_in_seq > 0).astype(F32)
    has_next = (t_in_seq < tiles_per_seq - 1).astype(F32)
    ext_ref[0:halo, :] = up_ref[...] * has_prev
    ext_ref[halo:halo + tm, :] = u_ref[...]
    ext_ref[halo + tm:, :] = un_ref[...] * has_next

    na_w = a_ref.shape[1]
    pool_w = len(POOL_WINDOWS) * HEAD_DIM
    mixed = jnp.dot(a_ref[...], wo_ref[0:na_w, :], preferred_element_type=F32)
    mixed += jnp.dot(c_ref[...], wo_ref[na_w + pool_w:, :], preferred_element_type=F32)

    pos = t_in_seq * tm + lax.broadcasted_iota(jnp.int32, (tm, 1), 0)
    parts = []
    for g, win in enumerate(POOL_WINDOWS):
        ls = slice(g * HEAD_DIM, (g + 1) * HEAD_DIM)
        total = functools.reduce(jnp.add, [ext_ref[halo + d:halo + d + tm, ls]
                                           for d in range(-(win // 2), win - win // 2)])
        lo = jnp.maximum(pos - win // 2, 0)
        hi = jnp.minimum(pos + (win - win // 2 - 1), seq_len - 1)
        delta = total / (hi - lo + 1).astype(F32) - u_ref[:, ls]
        y = jnp.dot(delta.astype(BF16), pw_ref[g], preferred_element_type=F32) * ps_ref[:, ls]
        parts.append(y.astype(BF16))
    mixed += jnp.dot(jnp.concatenate(parts, axis=1), wo_ref[na_w:na_w + pool_w, :], preferred_element_type=F32)
    o_ref[...] = x_ref[...] + mod_ref[5:6, :] * mixed


def _mix_out(x2, mod, a2, u2, c2, pool_w, pool_scale, w_out, layer, seq_len, *, tm=512):
    n_tok, d = x2.shape
    tm = _row_tile(seq_len, tm)
    tiles_per_mod = n_tok // mod.shape[0] // tm
    n_halo = n_tok // POOL_HALO
    hpt = tm // POOL_HALO
    row = lambda w: pl.BlockSpec((tm, w), lambda i: (i, 0))
    vmem = 4 * tm * d * 4 + 2 * w_out[0].size * 2 + 4 * tm * d * 4 + (4 << 20)
    return pl.pallas_call(
        functools.partial(_out_kernel, seq_len=seq_len),
        grid=(n_tok // tm,),
        in_specs=[row(d),
                  pl.BlockSpec((None, N_MOD, d), lambda i: (i // tiles_per_mod, 0, 0)),
                  row(a2.shape[1]), row(u2.shape[1]),
                  pl.BlockSpec((POOL_HALO, u2.shape[1]), lambda i: (jnp.maximum(i * hpt - 1, 0), 0)),
                  pl.BlockSpec((POOL_HALO, u2.shape[1]), lambda i: (jnp.minimum((i + 1) * hpt, n_halo - 1), 0)),
                  row(c2.shape[1]),
                  pl.BlockSpec((None,) + pool_w.shape[1:], lambda i: (layer, 0, 0, 0)),
                  pl.BlockSpec((1, u2.shape[1]), lambda i: (0, 0)),
                  pl.BlockSpec((None,) + w_out.shape[1:], lambda i: (layer, 0, 0))],
        out_specs=row(d),
        out_shape=jax.ShapeDtypeStruct((n_tok, d), F32),
        scratch_shapes=[pltpu.VMEM((tm + 2 * POOL_HALO, u2.shape[1]), F32)],
        compiler_params=_params(("parallel",), vmem),
        name="pool_mix_out",
    )(x2, mod, a2, u2, u2, u2, c2, pool_w, pool_scale.reshape(1, -1), w_out)


def _rope_tables(seq_len):
    n_rows = seq_len // GRID_W
    axis_dim = HEAD_DIM // 2
    inv_freq = ROPE_BASE ** (-jnp.arange(0, axis_dim, 2, dtype=F32) / axis_dim)
    ang_r = jnp.arange(n_rows, dtype=F32)[:, None] * inv_freq[None, :]
    ang_c = jnp.arange(GRID_W, dtype=F32)[:, None] * inv_freq[None, :]
    by_row = lambda t: jnp.broadcast_to(t[:, None, :], (n_rows, GRID_W, t.shape[-1]))
    by_col = lambda t: jnp.broadcast_to(t[None, :, :], (n_rows, GRID_W, t.shape[-1]))
    cos_r, sin_r = by_row(jnp.cos(ang_r)), by_row(jnp.sin(ang_r))
    cos_c, sin_c = by_col(jnp.cos(ang_c)), by_col(jnp.sin(ang_c))
    cos = jnp.concatenate([cos_r, cos_r, cos_c, cos_c], axis=-1).reshape(seq_len, HEAD_DIM)
    sin = jnp.concatenate([-sin_r, sin_r, -sin_c, sin_c], axis=-1).reshape(seq_len, HEAD_DIM)
    return cos, sin


def kernel(x, c, ctx, c_ctx, w_mod, b_mod, norm_w, ffn1_wi, ffn1_wo, ffn2_wi, ffn2_wo, w_in, w_out,
           na_qk_gain, na_rpb, pool_w, pool_scale, wa_qk_gain, wa_sink):
    b, s, d = x.shape
    cx = ctx.shape[1]
    depth = w_mod.shape[0]
    na_w = na_rpb.shape[1] * HEAD_DIM
    pw = pool_scale.shape[1]
    kv_w = (w_in.shape[2] - 3 * na_w - pw - (d - na_w - pw)) // 2
    widths = (na_w, pw, d - na_w - pw, kv_w)

    n_cond = b + 1
    cvecs = jnp.concatenate([c, c_ctx[None, :], jnp.zeros((-n_cond % SUBLANES, d), F32)], axis=0)
    mods = _modulation(cvecs, w_mod, b_mod).reshape(depth, cvecs.shape[0], N_MOD, d)
    rope = _rope_tables(s)
    wi1, wo1, wi2, wo2 = (t.astype(BF16) for t in (ffn1_wi, ffn1_wo, ffn2_wi, ffn2_wo))
    win, wout, pwb = w_in.astype(BF16), w_out.astype(BF16), pool_w.astype(BF16)

    x2 = x.reshape(b * s, d)
    c2 = ctx.reshape(b * cx, d)
    seq = lambda t: t.reshape(b, s, -1)
    cseq = lambda t: t.reshape(b, cx, -1)
    for l in range(depth):
        last = l == depth - 1
        mx, mc = mods[l, :b], mods[l, b:b + 1]

        x2 = _ffn(x2, mx, 0, norm_w[l, 0], wi1, wo1, l)
        c2 = _ffn(c2, mc, 0, norm_w[l, 0], wi1, wo1, l)

        nq, nk, nv, u, wq, wk, wv = _projection(x2, mx, norm_w[l, 1], win, l, na_qk_gain[l], wa_qk_gain[l],
                                                rope, widths)
        cnq, cnk, cnv, cu, cwq, cwk, cwv = _projection(c2, mc, norm_w[l, 1], win, l, na_qk_gain[l],
                                                       wa_qk_gain[l], None, widths)
        a_out = _neighbourhood_attention(seq(nq), seq(nk), seq(nv), cseq(cnk), cseq(cnv),
                                         _na_bias_table(na_rpb[l], s // GRID_W, cx))
        c_out = _window_attention(seq(wq), seq(wk), seq(wv), cseq(cwk), cseq(cwv), wa_sink[l])
        x2 = _mix_out(x2, mx, a_out.reshape(b * s, -1), u, c_out.reshape(b * s, -1), pwb, pool_scale[l],
                      wout, l, s)
        if not last:
            ca = _context_attention(cseq(cnq), cseq(cnk), cseq(cnv), None)
            cc = _context_attention(cseq(cwq), cseq(cwk), cseq(cwv), wa_sink[l])
            c2 = _mix_out(c2, mc, ca.reshape(b * cx, -1), cu, cc.reshape(b * cx, -1), pwb, pool_scale[l],
                          wout, l, cx)
            c2 = _ffn(c2, mc, 6, norm_w[l, 2], wi2, wo2, l)
        x2 = _ffn(x2, mx, 6, norm_w[l, 2], wi2, wo2, l)
    return x2.reshape(b, s, d)
```

```python
import functools

import numpy as np
import jax
import jax.numpy as jnp
from jax import lax
from jax.experimental import pallas as pl
from jax.experimental.pallas import tpu as pltpu

HEAD_DIM = 128
GRID_W = 64
NA_WIN_H = 8
NA_WIN_W = 16
POOL_WINDOWS = (2, 4, 8, 16)
WA_BLOCK = 128
WA_Q_BLOCKS = 1
N_MOD = 9
MACARON_WEIGHT = 0.5
ROPE_BASE = 10000.0
RMS_EPS = 1e-6
NEG_INF = -1e30
LOG2E = 1.4426950408889634
Q_SCALE = HEAD_DIM ** -0.5 * LOG2E

V7X_VMEM_BYTES = 64 * 1024 * 1024
V7X_VMEM_BUDGET = 56 * 1024 * 1024
SUBLANES = 8
NA_Q_ROWS = NA_WIN_H // 2
POOL_HALO = 16

BF16 = jnp.bfloat16
F32 = jnp.float32


def _params(semantics, vmem_bytes):
    return pltpu.CompilerParams(dimension_semantics=semantics,
                                vmem_limit_bytes=int(min(max(vmem_bytes, 16 << 20), V7X_VMEM_BUDGET)))


def _row_tile(n_rows, want):
    t = min(want, n_rows)
    assert n_rows % t == 0, (n_rows, t)
    return t


def _mod_norm(x, mod_ref, g_ref, first):
    y = x * lax.rsqrt(jnp.mean(x * x, axis=-1, keepdims=True) + RMS_EPS) * g_ref[...]
    return y * (1.0 + mod_ref[first + 1:first + 2, :]) + mod_ref[first:first + 1, :]


def _mod_kernel(c_ref, w_ref, b_ref, o_ref):
    c = c_ref[...]
    s = (c * jax.nn.sigmoid(c)).astype(BF16)
    o_ref[...] = jnp.dot(s, w_ref[...].astype(BF16), preferred_element_type=F32) + b_ref[...]


def _modulation(cvecs, w_mod, b_mod):
    n_layers, d, nd = w_mod.shape
    rows = cvecs.shape[0]
    tn = _row_tile(nd, 2048)
    return pl.pallas_call(
        _mod_kernel,
        grid=(n_layers, nd // tn),
        in_specs=[pl.BlockSpec((rows, d), lambda l, n: (0, 0)),
                  pl.BlockSpec((None, d, tn), lambda l, n: (l, 0, n)),
                  pl.BlockSpec((None, 1, tn), lambda l, n: (l, 0, n))],
        out_specs=pl.BlockSpec((None, rows, tn), lambda l, n: (l, 0, n)),
        out_shape=jax.ShapeDtypeStruct((n_layers, rows, nd), F32),
        compiler_params=_params(("parallel", "parallel"), 2 * d * tn * 4 + (8 << 20)),
        name="adaln_modulation",
    )(cvecs, w_mod, b_mod.reshape(n_layers, 1, nd))


def _ffn_kernel(*refs, first, tiles_per_mod, n_chunks, n_cast):
    x_hbm, mod_ref, g_ref, wa_ref, wb_ref, wo_ref = refs[:6]
    cast_in = refs[6:6 + n_cast]
    o_hbm = refs[6 + n_cast]
    cast_out = refs[7 + n_cast:7 + 2 * n_cast]
    xa_ref, xb_ref, ha_ref, hb_ref, in_sem, out_sem = refs[7 + 2 * n_cast:]
    i, f = pl.program_id(0), pl.program_id(1)
    n_tiles, nf = pl.num_programs(0), pl.num_programs(1)
    tm = xa_ref.shape[0]
    rows = tm // n_chunks
    head = tm - rows

    def x_in(tile, dst, part, slot):
        r0, n = ((0, head), (head, rows))[part]
        return pltpu.make_async_copy(x_hbm.at[pl.ds(tile * tm + r0, n), :], dst.at[pl.ds(r0, n), :],
                                     in_sem.at[slot, part])

    def x_out(tile, src, slot):
        return pltpu.make_async_copy(src, o_hbm.at[pl.ds(tile * tm, tm), :], out_sem.at[slot])

    def normed(x_rows, batch):
        return _mod_norm(x_rows, mod_ref.at[batch], g_ref, first).astype(BF16)

    def tile_step(slot, x_cur, x_nxt, h_cur, h_nxt):
        other = 1 - slot
        has_next = i + 1 < n_tiles

        if slot == 0:
            @pl.when((i == 0) & (f == 0))
            def _():
                for part in (0, 1):
                    x_in(0, x_cur, part, slot).start()
                x_nxt[...] = jnp.zeros(x_nxt.shape, x_nxt.dtype)
                for part in (0, 1):
                    x_in(0, x_cur, part, slot).wait()
                h_cur[...] = normed(x_cur[...], 0)

        @pl.when((f == 1) & (i > 0))
        def _():
            x_out(i - 1, x_nxt, other).wait()

        @pl.when((f == 1) & has_next)
        def _():
            x_in(i + 1, x_nxt, 0, other).start()

        @pl.when((f == 2) & has_next)
        def _():
            x_in(i + 1, x_nxt, 0, other).wait()
            x_in(i + 1, x_nxt, 1, other).start()

        @pl.when((f == 3) & has_next)
        def _():
            x_in(i + 1, x_nxt, 1, other).wait()

        for src, dst in zip(cast_in, cast_out):
            dst[...] = src[...].astype(dst.dtype)

        chunk = jnp.where((f >= 2) & (f < 2 + n_chunks), f - 2, n_chunks - 1)
        r0 = pl.multiple_of(chunk * rows, rows)
        next_batch = jnp.minimum(i + 1, n_tiles - 1) // tiles_per_mod
        h_nxt[pl.ds(r0, rows), :] = normed(x_nxt[pl.ds(r0, rows), :], next_batch)

        gate = MACARON_WEIGHT * mod_ref[i // tiles_per_mod, first + 2:first + 3, :]
        h = h_cur[...]
        a = jnp.dot(h, wa_ref[...], preferred_element_type=F32)
        b = jnp.dot(h, wb_ref[...], preferred_element_type=F32)
        act = (a * jax.nn.sigmoid(a) * b).astype(BF16)
        x_cur[...] += gate * jnp.dot(act, wo_ref[...], preferred_element_type=F32)

        @pl.when(f == nf - 1)
        def _():
            x_out(i, x_cur, slot).start()

        @pl.when((f == nf - 1) & (i == n_tiles - 1))
        def _():
            x_out(i, x_cur, slot).wait()

    @pl.when(i % 2 == 0)
    def _():
        tile_step(0, xa_ref, xb_ref, ha_ref, hb_ref)

    @pl.when(i % 2 == 1)
    def _():
        tile_step(1, xb_ref, xa_ref, hb_ref, ha_ref)


def _ffn(x2, mod, first, norm_g, wi, wo, layer, cast=(), *, tm=1024, tf=512):
    n_tok, d = x2.shape
    d_ff = wo.shape[1]
    tm = _row_tile(n_tok // mod.shape[0], tm)
    tf = _row_tile(d_ff, tf)
    tiles_per_mod = n_tok // mod.shape[0] // tm
    nf = d_ff // tf
    n_steps = (n_tok // tm) * nf
    n_chunks = 1 << ((nf - 2).bit_length() - 1)
    assert nf >= 4 and tm % (n_chunks * 2 * SUBLANES) == 0, (nf, tm)

    cast_in_specs, cast_out_specs, cast_shapes = [], [], []
    for arr, lyr in cast:
        _, rows, cols = arr.shape
        block = next(r for r in range(2 * SUBLANES, rows + 1, 2 * SUBLANES)
                     if rows % r == 0 and rows // r <= n_steps)
        last = rows // block - 1
        cast_in_specs.append(pl.BlockSpec(
            (None, block, cols), lambda i, f, last=last, lyr=lyr: (lyr, jnp.minimum(i * nf + f, last), 0)))
        cast_out_specs.append(pl.BlockSpec(
            (None, block, cols), lambda i, f, last=last: (0, jnp.minimum(i * nf + f, last), 0)))
        cast_shapes.append(jax.ShapeDtypeStruct((1, rows, cols), BF16))

    vmem = 2 * tm * d * 4 + 2 * tm * d * 2 + 2 * 3 * d * tf * 2 + 8 * tm * tf * 4 + (6 << 20)
    out, *cast_out = pl.pallas_call(
        functools.partial(_ffn_kernel, first=first, tiles_per_mod=tiles_per_mod, n_chunks=n_chunks,
                          n_cast=len(cast)),
        grid=(n_tok // tm, nf),
        in_specs=[pl.BlockSpec(memory_space=pl.ANY),
                  pl.BlockSpec(mod.shape, lambda i, f: (0, 0, 0)),
                  pl.BlockSpec((1, d), lambda i, f: (0, 0)),
                  pl.BlockSpec((None, d, tf), lambda i, f: (layer, 0, f)),
                  pl.BlockSpec((None, d, tf), lambda i, f: (layer, 0, nf + f)),
                  pl.BlockSpec((None, tf, d), lambda i, f: (layer, f, 0))] + cast_in_specs,
        out_specs=[pl.BlockSpec(memory_space=pl.ANY)] + cast_out_specs,
        out_shape=[jax.ShapeDtypeStruct((n_tok, d), F32)] + cast_shapes,
        scratch_shapes=[pltpu.VMEM((tm, d), F32), pltpu.VMEM((tm, d), F32),
                        pltpu.VMEM((tm, d), BF16), pltpu.VMEM((tm, d), BF16),
                        pltpu.SemaphoreType.DMA((2, 2)), pltpu.SemaphoreType.DMA((2,))],
        compiler_params=_params(("arbitrary", "arbitrary"), vmem),
        name="macaron_ffn",
    )(x2, mod, norm_g.reshape(1, d), wi, wi, wo, *[arr for arr, _ in cast])
    return out, cast_out


def _head_norm(p, gain):
    return p * lax.rsqrt(jnp.mean(p * p, axis=-1, keepdims=True) + RMS_EPS) * gain


def _rotate(y, cos, sin_signed):
    lane = lax.broadcasted_iota(jnp.int32, y.shape, 1)
    partner = jnp.where((lane & 32) == 0, pltpu.roll(y, HEAD_DIM - 32, 1), pltpu.roll(y, 32, 1))
    return y * cos + partner * sin_signed


def _proj_kernel(*refs, use_rope, widths):
    if use_rope:
        x_ref, mod_ref, g_ref, w_ref, nag_ref, wag_ref, cos_ref, sin_ref = refs[:8]
        outs = refs[8:]
    else:
        x_ref, mod_ref, g_ref, w_ref, nag_ref, wag_ref = refs[:6]
        outs = refs[6:]
    nq_ref, nk_ref, nv_ref, u_ref, wq_ref, wk_ref, wv_ref = outs
    na_w, pool_w, wa_w, kv_w = widths
    h = _mod_norm(x_ref[...], mod_ref, g_ref, 3).astype(BF16)

    def chunk(col0, width):
        return jnp.dot(h, w_ref[:, col0:col0 + width], preferred_element_type=F32)

    def normed(p, gain, out_ref, out_col0, scale, rope):
        for j in range(p.shape[1] // HEAD_DIM):
            y = _head_norm(p[:, j * HEAD_DIM:(j + 1) * HEAD_DIM], gain)
            if rope:
                y = _rotate(y, cos_ref[...], sin_ref[...])
            if scale != 1.0:
                y = y * scale
            c0 = out_col0 + j * HEAD_DIM
            out_ref[:, c0:c0 + HEAD_DIM] = y.astype(out_ref.dtype)

    c_nq, c_nk, c_nv, c_u = 0, na_w, 2 * na_w, 3 * na_w
    c_wq = c_u + pool_w
    c_wk = c_wq + wa_w
    c_wv = c_wk + kv_w
    half = wa_w // 2
    normed(chunk(c_wq, half), wag_ref[0:1, :], wq_ref, 0, Q_SCALE, use_rope)
    normed(chunk(c_wq + half, half), wag_ref[0:1, :], wq_ref, half, Q_SCALE, use_rope)
    normed(chunk(c_wk, kv_w), wag_ref[1:2, :], wk_ref, 0, 1.0, use_rope)
    normed(chunk(c_nq, na_w), nag_ref[0:1, :], nq_ref, 0, Q_SCALE, False)
    normed(chunk(c_nk, na_w), nag_ref[1:2, :], nk_ref, 0, 1.0, False)
    nv_ref[...] = chunk(c_nv, na_w).astype(BF16)
    u_ref[...] = chunk(c_u, pool_w)
    wv_ref[...] = chunk(c_wv, kv_w).astype(BF16)


def _projection(x2, mod, norm_g, w_in, layer, na_gain, wa_gain, rope, widths, *, tm=512):
    n_tok, d = x2.shape
    na_w, pool_w, wa_w, kv_w = widths
    n_in = w_in.shape[2]
    tm = _row_tile(n_tok, tm)
    tiles_per_mod = n_tok // mod.shape[0] // tm
    use_rope = rope is not None
    in_specs = [pl.BlockSpec((tm, d), lambda i: (i, 0)),
                pl.BlockSpec((None, N_MOD, d), lambda i: (i // tiles_per_mod, 0, 0)),
                pl.BlockSpec((1, d), lambda i: (0, 0)),
                pl.BlockSpec((None, d, n_in), lambda i: (layer, 0, 0)),
                pl.BlockSpec((2, HEAD_DIM), lambda i: (0, 0)),
                pl.BlockSpec((2, HEAD_DIM), lambda i: (0, 0))]
    args = [x2, mod, norm_g.reshape(1, d), w_in, na_gain, wa_gain]
    if use_rope:
        tiles_per_seq = rope[0].shape[0] // tm
        in_specs += [pl.BlockSpec((tm, HEAD_DIM), lambda i: (i % tiles_per_seq, 0))] * 2
        args += list(rope)
    out_widths = (na_w, na_w, na_w, pool_w, wa_w, kv_w, kv_w)
    out_dtypes = (BF16, BF16, BF16, F32, BF16, BF16, BF16)
    vmem = 2 * tm * d * 4 + tm * d * 2 + 2 * d * n_in * 2 + 4 * tm * n_in * 4 + (4 << 20)
    return pl.pallas_call(
        functools.partial(_proj_kernel, use_rope=use_rope, widths=widths),
        grid=(n_tok // tm,),
        in_specs=in_specs,
        out_specs=[pl.BlockSpec((tm, w), lambda i: (i, 0)) for w in out_widths],
        out_shape=[jax.ShapeDtypeStruct((n_tok, w), dt) for w, dt in zip(out_widths, out_dtypes)],
        compiler_params=_params(("parallel",), vmem),
        name="input_projection",
    )(*args)


def _softmax_weights(s, extra_logit=None):
    m = jnp.max(s, axis=-1, keepdims=True)
    if extra_logit is not None:
        m = jnp.maximum(m, extra_logit)
    p = jnp.exp2(s - m)
    denom = jnp.sum(p, axis=-1, keepdims=True)
    if extra_logit is not None:
        denom = denom + jnp.exp2(extra_logit - m)
    return p.astype(BF16), denom


def _attend(scores, values, extra_logits=None):
    extra_logits = extra_logits or [None] * len(scores)
    weights = [_softmax_weights(s, e) for s, e in zip(scores, extra_logits)]
    return [jnp.dot(p, v, preferred_element_type=F32) / denom for (p, denom), v in zip(weights, values)]


def _qk(q, k):
    return lax.dot_general(q, k, (((1,), (1,)), ((), ())), preferred_element_type=F32)


def _na_kernel(q_ref, kp_ref, kc_ref, kn_ref, vp_ref, vc_ref, vn_ref, kx_ref, vx_ref, bias_ref, o_ref):
    n_heads = q_ref.shape[1] // HEAD_DIM
    heads = [slice(h * HEAD_DIM, (h + 1) * HEAD_DIM) for h in range(n_heads)]
    scores, values = [], []
    for h, hs in enumerate(heads):
        k = jnp.concatenate([kp_ref[:, hs], kc_ref[:, hs], kn_ref[:, hs], kx_ref[:, hs]], axis=0)
        scores.append(_qk(q_ref[:, hs], k) + bias_ref[h])
        values.append(jnp.concatenate([vp_ref[:, hs], vc_ref[:, hs], vn_ref[:, hs], vx_ref[:, hs]], axis=0))
    for hs, out in zip(heads, _attend(scores, values)):
        o_ref[:, hs] = out.astype(o_ref.dtype)


def _na_bias_kernel(rpb_ref, o_ref, g_ref, *, n_rows):
    n_ro, n_co = 2 * NA_WIN_H - 1, 2 * NA_WIN_W - 1
    shape = (GRID_W, 2 * GRID_W)
    qc = lax.broadcasted_iota(jnp.int32, shape, 0)
    lane = lax.broadcasted_iota(jnp.int32, shape, 1)
    kc = lane & (GRID_W - 1)
    c0 = jnp.clip(qc - NA_WIN_W // 2, 0, GRID_W - NA_WIN_W)
    in_cols = (kc >= c0) & (kc < c0 + NA_WIN_W)
    col_off = kc - qc + (NA_WIN_W - 1)
    base = pl.program_id(0) * (n_ro * n_co)
    for ro in range(n_ro):
        g = jnp.zeros(shape, F32)
        for c in range(n_co):
            g = jnp.where(col_off == c, rpb_ref[base + ro * n_co + c] * LOG2E, g)
        g_ref[ro] = jnp.where(in_cols, g, NEG_INF)

    masked = jnp.full(shape, NEG_INF, F32)
    n_local = 3 * NA_Q_ROWS * GRID_W
    n_blocks = n_rows // NA_Q_ROWS
    o_ref[:, :, n_local:] = jnp.zeros(o_ref.shape[:2] + (o_ref.shape[2] - n_local,), F32)
    for variant, jb in enumerate((0, 1, n_blocks - 1)):
        for i in range(NA_Q_ROWS):
            r = jb * NA_Q_ROWS + i
            r0 = min(max(r - NA_WIN_H // 2, 0), n_rows - NA_WIN_H)
            halves = []
            for kr in range(3 * NA_Q_ROWS):
                kr_abs = (jb - 1) * NA_Q_ROWS + kr
                in_rows = 0 <= kr_abs < n_rows and r0 <= kr_abs < r0 + NA_WIN_H
                halves.append(g_ref[kr_abs - r + NA_WIN_H - 1] if in_rows else masked)
            for m in range(len(halves) // 2):
                o_ref[variant, i * GRID_W:(i + 1) * GRID_W, m * 2 * GRID_W:(m + 1) * 2 * GRID_W] = jnp.where(
                    lane < GRID_W, halves[2 * m], halves[2 * m + 1])


def _na_bias_table(rpb, n_rows, n_ctx):
    n_heads = rpb.shape[0]
    assert n_rows % NA_Q_ROWS == 0 and n_rows // NA_Q_ROWS >= 3 and (3 * NA_Q_ROWS) % 2 == 0
    shape = (3, n_heads, NA_Q_ROWS * GRID_W, 3 * NA_Q_ROWS * GRID_W + n_ctx)
    return pl.pallas_call(
        functools.partial(_na_bias_kernel, n_rows=n_rows),
        grid=(n_heads,),
        in_specs=[pl.BlockSpec(memory_space=pltpu.SMEM)],
        out_specs=pl.BlockSpec((3, None) + shape[2:], lambda h: (0, h, 0, 0)),
        out_shape=jax.ShapeDtypeStruct(shape, F32),
        scratch_shapes=[pltpu.VMEM((2 * NA_WIN_H - 1, GRID_W, 2 * GRID_W), F32)],
        compiler_params=_params(("parallel",), 24 << 20),
        name="na_bias_table",
    )(rpb.astype(F32).reshape(-1))


def _neighbourhood_attention(q, k, v, kx, vx, bias):
    b, s, w = q.shape
    blk = NA_Q_ROWS * GRID_W
    nj = s // blk
    cx = kx.shape[1]
    cur = lambda bb, j: (bb, j, 0)
    prev = lambda bb, j: (bb, jnp.maximum(j - 1, 0), 0)
    nxt = lambda bb, j: (bb, jnp.minimum(j + 1, nj - 1), 0)
    edge = lambda bb, j: (jnp.where(j == 0, 0, jnp.where(j == nj - 1, 2, 1)), 0, 0, 0)
    tok = lambda im: pl.BlockSpec((None, blk, w), im)
    ctx = pl.BlockSpec((None, cx, w), lambda bb, j: (bb, 0, 0))
    vmem = 2 * bias[0].size * 4 + 6 * blk * bias.shape[3] * 4 + (8 << 20)
    return pl.pallas_call(
        _na_kernel,
        grid=(b, nj),
        in_specs=[tok(cur), tok(prev), tok(cur), tok(nxt), tok(prev), tok(cur), tok(nxt), ctx, ctx,
                  pl.BlockSpec((None,) + bias.shape[1:], edge)],
        out_specs=tok(cur),
        out_shape=jax.ShapeDtypeStruct((b, s, w), BF16),
        compiler_params=_params(("parallel", "parallel"), vmem),
        name="neighbourhood_attention",
    )(q, k, k, k, v, v, v, kx, vx, bias)


def _wa_band_bias(q_rows, n_ctx):
    halo = WA_BLOCK
    p = np.arange(q_rows)[:, None]
    k = np.arange(q_rows + 2 * halo + n_ctx)[None, :]
    in_band = (k >= p) & (k <= p + 2 * halo)
    is_prev, is_next, is_ctx = k < halo, (k >= halo + q_rows) & (k < 2 * halo + q_rows), k >= 2 * halo + q_rows
    tables = []
    for has_prev, has_next in ((False, True), (True, True), (True, False)):
        valid = is_ctx | (in_band & (has_prev | ~is_prev) & (has_next | ~is_next))
        tables.append(np.where(valid, 0.0, NEG_INF).astype(np.float32))
    return np.stack(tables)


def _wa_kernel(sink_ref, band_ref, q_ref, kp_ref, kc_ref, kn_ref, vp_ref, vc_ref, vn_ref, kx_ref, vx_ref,
               o_ref, *, group):
    rows = q_ref.shape[0]
    n_kv = kc_ref.shape[1] // HEAD_DIM
    band = jnp.concatenate([band_ref[...]] * group, axis=0)
    scores, values, sinks = [], [], []
    for kv in range(n_kv):
        hs = slice(kv * HEAD_DIM, (kv + 1) * HEAD_DIM)
        heads = [kv * group + g for g in range(group)]
        q = jnp.concatenate([q_ref[:, h * HEAD_DIM:(h + 1) * HEAD_DIM] for h in heads], axis=0)
        k = jnp.concatenate([kp_ref[:, hs], kc_ref[:, hs], kn_ref[:, hs], kx_ref[:, hs]], axis=0)
        scores.append(_qk(q, k) + band)
        values.append(jnp.concatenate([vp_ref[:, hs], vc_ref[:, hs], vn_ref[:, hs], vx_ref[:, hs]], axis=0))
        sinks.append(jnp.concatenate([jnp.full((rows, 1), sink_ref[h] * LOG2E, F32) for h in heads], axis=0))
    for kv, out in enumerate(_attend(scores, values, sinks)):
        for g in range(group):
            h = kv * group + g
            o_ref[:, h * HEAD_DIM:(h + 1) * HEAD_DIM] = out[g * rows:(g + 1) * rows, :].astype(o_ref.dtype)


def _window_attention(q, k, v, kx, vx, sink, *, q_blocks=WA_Q_BLOCKS):
    b, s, wq = q.shape
    wk = k.shape[2]
    group = wq // wk
    rows = q_blocks * WA_BLOCK
    n_tiles = s // rows
    n_halo = s // WA_BLOCK
    cx = kx.shape[1]
    assert s % rows == 0 and n_tiles >= 2
    cur = lambda bb, j: (bb, j, 0)
    prev = lambda bb, j: (bb, jnp.maximum(j * q_blocks - 1, 0), 0)
    nxt = lambda bb, j: (bb, jnp.minimum((j + 1) * q_blocks, n_halo - 1), 0)
    halo = lambda im: pl.BlockSpec((None, WA_BLOCK, wk), im)
    tile = pl.BlockSpec((None, rows, wk), cur)
    ctx = pl.BlockSpec((None, cx, wk), lambda bb, j: (bb, 0, 0))
    band = jnp.asarray(_wa_band_bias(rows, cx))
    edge = lambda bb, j: (jnp.where(j == 0, 0, jnp.where(j == n_tiles - 1, 2, 1)), 0, 0)
    vmem = 12 * group * rows * band.shape[2] * 4 + (8 << 20)
    return pl.pallas_call(
        functools.partial(_wa_kernel, group=group),
        grid=(b, n_tiles),
        in_specs=[pl.BlockSpec(memory_space=pltpu.SMEM),
                  pl.BlockSpec((None,) + band.shape[1:], edge),
                  pl.BlockSpec((None, rows, wq), cur),
                  halo(prev), tile, halo(nxt), halo(prev), tile, halo(nxt), ctx, ctx],
        out_specs=pl.BlockSpec((None, rows, wq), cur),
        out_shape=jax.ShapeDtypeStruct((b, s, wq), BF16),
        compiler_params=_params(("parallel", "parallel"), vmem),
        name="window_attention",
    )(sink, band, q, k, k, k, v, v, v, kx, vx)


def _ctx_attn_kernel(*refs, group, use_sink):
    if use_sink:
        sink_ref, q_ref, k_ref, v_ref, o_ref = refs
    else:
        q_ref, k_ref, v_ref, o_ref = refs
    c = q_ref.shape[0]
    for kv in range(k_ref.shape[1] // HEAD_DIM):
        hs = slice(kv * HEAD_DIM, (kv + 1) * HEAD_DIM)
        heads = [kv * group + g for g in range(group)]
        q = jnp.concatenate([q_ref[:, h * HEAD_DIM:(h + 1) * HEAD_DIM] for h in heads], axis=0)
        sink = None
        if use_sink:
            sink = jnp.concatenate([jnp.full((c, 1), sink_ref[h] * LOG2E, F32) for h in heads], axis=0)
        out, = _attend([_qk(q, k_ref[:, hs])], [v_ref[:, hs]], [sink])
        for g, h in enumerate(heads):
            o_ref[:, h * HEAD_DIM:(h + 1) * HEAD_DIM] = out[g * c:(g + 1) * c, :].astype(o_ref.dtype)


def _context_attention(q, k, v, sink):
    b, c, wq = q.shape
    wk = k.shape[2]
    use_sink = sink is not None
    spec = lambda w: pl.BlockSpec((None, c, w), lambda bb: (bb, 0, 0))
    in_specs = [spec(wq), spec(wk), spec(wk)]
    args = [q, k, v]
    if use_sink:
        in_specs = [pl.BlockSpec(memory_space=pltpu.SMEM)] + in_specs
        args = [sink] + args
    return pl.pallas_call(
        functools.partial(_ctx_attn_kernel, group=wq // wk, use_sink=use_sink),
        grid=(b,),
        in_specs=in_specs,
        out_specs=spec(wq),
        out_shape=jax.ShapeDtypeStruct((b, c, wq), BF16),
        compiler_params=_params(("parallel",), 24 << 20),
        name="context_attention",
    )(*args)


def _out_kernel(x_ref, mod_ref, a_ref, u_ref, up_ref, un_ref, c_ref, pw_ref, ps_ref, wo_ref, o_ref, ext_ref,
                *, seq_len):
    tm = x_ref.shape[0]
    tiles_per_seq = seq_len // tm
    t_in_seq = pl.program_id(0) % tiles_per_seq
    halo = POOL_HALO
    has_prev = (t_in_seq > 0).astype(F32)
    has_next = (t_in_seq < tiles_per_seq - 1).astype(F32)
    ext_ref[0:halo, :] = up_ref[...] * has_prev
    ext_ref[halo:halo + tm, :] = u_ref[...]
    ext_ref[halo + tm:, :] = un_ref[...] * has_next

    pos = t_in_seq * tm + lax.broadcasted_iota(jnp.int32, (tm, 1), 0)
    parts = []
    for g, win in enumerate(POOL_WINDOWS):
        ls = slice(g * HEAD_DIM, (g + 1) * HEAD_DIM)
        total = functools.reduce(jnp.add, [ext_ref[halo + d:halo + d + tm, ls]
                                           for d in range(-(win // 2), win - win // 2)])
        lo = jnp.maximum(pos - win // 2, 0)
        hi = jnp.minimum(pos + (win - win // 2 - 1), seq_len - 1)
        delta = total / (hi - lo + 1).astype(F32) - u_ref[:, ls]
        y = jnp.dot(delta.astype(BF16), pw_ref[g], preferred_element_type=F32) * ps_ref[:, ls]
        parts.append(y.astype(BF16))
    na_w = a_ref.shape[1]
    pool_w = len(POOL_WINDOWS) * HEAD_DIM
    mixed = jnp.dot(a_ref[...], wo_ref[0:na_w, :], preferred_element_type=F32)
    mixed += jnp.dot(c_ref[...], wo_ref[na_w + pool_w:, :], preferred_element_type=F32)
    mixed += jnp.dot(jnp.concatenate(parts, axis=1), wo_ref[na_w:na_w + pool_w, :], preferred_element_type=F32)
    o_ref[...] = x_ref[...] + mod_ref[5:6, :] * mixed


def _mix_out(x2, mod, a2, u2, c2, pool_w, pool_layer, pool_scale, w_out, seq_len, *, tm=512):
    n_tok, d = x2.shape
    tm = _row_tile(seq_len, tm)
    tiles_per_mod = n_tok // mod.shape[0] // tm
    n_halo = n_tok // POOL_HALO
    hpt = tm // POOL_HALO
    row = lambda w: pl.BlockSpec((tm, w), lambda i: (i, 0))
    vmem = 4 * tm * d * 4 + 2 * w_out[0].size * 2 + 4 * tm * d * 4 + (4 << 20)
    return pl.pallas_call(
        functools.partial(_out_kernel, seq_len=seq_len),
        grid=(n_tok // tm,),
        in_specs=[row(d),
                  pl.BlockSpec((None, N_MOD, d), lambda i: (i // tiles_per_mod, 0, 0)),
                  row(a2.shape[1]), row(u2.shape[1]),
                  pl.BlockSpec((POOL_HALO, u2.shape[1]), lambda i: (jnp.maximum(i * hpt - 1, 0), 0)),
                  pl.BlockSpec((POOL_HALO, u2.shape[1]), lambda i: (jnp.minimum((i + 1) * hpt, n_halo - 1), 0)),
                  row(c2.shape[1]),
                  pl.BlockSpec((None,) + pool_w.shape[1:], lambda i: (pool_layer, 0, 0, 0)),
                  pl.BlockSpec((1, u2.shape[1]), lambda i: (0, 0)),
                  pl.BlockSpec((None,) + w_out.shape[1:], lambda i: (0, 0, 0))],
        out_specs=row(d),
        out_shape=jax.ShapeDtypeStruct((n_tok, d), F32),
        scratch_shapes=[pltpu.VMEM((tm + 2 * POOL_HALO, u2.shape[1]), F32)],
        compiler_params=_params(("parallel",), vmem),
        name="pool_mix_out",
    )(x2, mod, a2, u2, u2, u2, c2, pool_w, pool_scale.reshape(1, -1), w_out)


def _rope_tables(seq_len):
    n_rows = seq_len // GRID_W
    axis_dim = HEAD_DIM // 2
    inv_freq = ROPE_BASE ** (-jnp.arange(0, axis_dim, 2, dtype=F32) / axis_dim)
    ang_r = jnp.arange(n_rows, dtype=F32)[:, None] * inv_freq[None, :]
    ang_c = jnp.arange(GRID_W, dtype=F32)[:, None] * inv_freq[None, :]
    by_row = lambda t: jnp.broadcast_to(t[:, None, :], (n_rows, GRID_W, t.shape[-1]))
    by_col = lambda t: jnp.broadcast_to(t[None, :, :], (n_rows, GRID_W, t.shape[-1]))
    cos_r, sin_r = by_row(jnp.cos(ang_r)), by_row(jnp.sin(ang_r))
    cos_c, sin_c = by_col(jnp.cos(ang_c)), by_col(jnp.sin(ang_c))
    cos = jnp.concatenate([cos_r, cos_r, cos_c, cos_c], axis=-1).reshape(seq_len, HEAD_DIM)
    sin = jnp.concatenate([-sin_r, sin_r, -sin_c, sin_c], axis=-1).reshape(seq_len, HEAD_DIM)
    return cos, sin


def kernel(x, c, ctx, c_ctx, w_mod, b_mod, norm_w, ffn1_wi, ffn1_wo, ffn2_wi, ffn2_wo, w_in, w_out,
           na_qk_gain, na_rpb, pool_w, pool_scale, wa_qk_gain, wa_sink):
    b, s, d = x.shape
    cx = ctx.shape[1]
    depth = w_mod.shape[0]
    na_w = na_rpb.shape[1] * HEAD_DIM
    pw = pool_scale.shape[1]
    kv_w = (w_in.shape[2] - 3 * na_w - pw - (d - na_w - pw)) // 2
    widths = (na_w, pw, d - na_w - pw, kv_w)

    n_cond = b + 1
    cvecs = jnp.concatenate([c, c_ctx[None, :], jnp.zeros((-n_cond % SUBLANES, d), F32)], axis=0)
    mods = _modulation(cvecs, w_mod, b_mod).reshape(depth, cvecs.shape[0], N_MOD, d)
    rope = _rope_tables(s)
    wi1, wo1 = ffn1_wi[:1].astype(BF16), ffn1_wo[:1].astype(BF16)
    pwb = pool_w.astype(BF16)

    x2 = x.reshape(b * s, d)
    c2 = ctx.reshape(b * cx, d)
    seq = lambda t: t.reshape(b, s, -1)
    cseq = lambda t: t.reshape(b, cx, -1)
    for l in range(depth):
        last = l == depth - 1
        mx, mc = mods[l, :b], mods[l, b:b + 1]

        x2, (win, wout, wi2, wo2) = _ffn(x2, mx, 0, norm_w[l, 0], wi1, wo1, 0,
                                         cast=[(w_in, l), (w_out, l), (ffn2_wi, l), (ffn2_wo, l)])
        c2, _ = _ffn(c2, mc, 0, norm_w[l, 0], wi1, wo1, 0)

        nq, nk, nv, u, wq, wk, wv = _projection(x2, mx, norm_w[l, 1], win, 0, na_qk_gain[l], wa_qk_gain[l],
                                                rope, widths)
        cnq, cnk, cnv, cu, cwq, cwk, cwv = _projection(c2, mc, norm_w[l, 1], win, 0, na_qk_gain[l],
                                                       wa_qk_gain[l], None, widths)
        a_out = _neighbourhood_attention(seq(nq), seq(nk), seq(nv), cseq(cnk), cseq(cnv),
                                         _na_bias_table(na_rpb[l], s // GRID_W, cx))
        c_out = _window_attention(seq(wq), seq(wk), seq(wv), cseq(cwk), cseq(cwv), wa_sink[l])
        x2 = _mix_out(x2, mx, a_out.reshape(b * s, -1), u, c_out.reshape(b * s, -1), pwb, l, pool_scale[l],
                      wout, s)
        if not last:
            ca = _context_attention(cseq(cnq), cseq(cnk), cseq(cnv), None)
            cc = _context_attention(cseq(cwq), cseq(cwk), cseq(cwv), wa_sink[l])
            c2 = _mix_out(c2, mc, ca.reshape(b * cx, -1), cu, cc.reshape(b * cx, -1), pwb, l, pool_scale[l],
                          wout, cx)
            c2, _ = _ffn(c2, mc, 6, norm_w[l, 2], wi2, wo2, 0)
        x2, nxt = _ffn(x2, mx, 6, norm_w[l, 2], wi2, wo2, 0,
                       cast=[] if last else [(ffn1_wi, l + 1), (ffn1_wo, l + 1)])
        if not last:
            wi1, wo1 = nxt
    return x2.reshape(b, s, d)
```

```python
import functools

import numpy as np
import jax
import jax.numpy as jnp
from jax import lax
from jax.experimental import pallas as pl
from jax.experimental.pallas import tpu as pltpu

HEAD_DIM = 128
GRID_W = 64
NA_WIN_H = 8
NA_WIN_W = 16
POOL_WINDOWS = (2, 4, 8, 16)
WA_BLOCK = 128
WA_Q_BLOCKS = 4
N_MOD = 9
MACARON_WEIGHT = 0.5
ROPE_BASE = 10000.0
RMS_EPS = 1e-6
NEG_INF = -1e30
LOG2E = 1.4426950408889634
Q_SCALE = HEAD_DIM ** -0.5 * LOG2E

V7X_VMEM_BYTES = 64 * 1024 * 1024
V7X_VMEM_BUDGET = 56 * 1024 * 1024
SUBLANES = 8
NA_Q_BLOCKS = 4
NA_Q_ROWS = NA_WIN_H // 2
POOL_HALO = 16

BF16 = jnp.bfloat16
F32 = jnp.float32


def _params(semantics, vmem_bytes):
    return pltpu.CompilerParams(dimension_semantics=semantics,
                                vmem_limit_bytes=int(min(max(vmem_bytes, 16 << 20), V7X_VMEM_BUDGET)))


def _row_tile(n_rows, want):
    t = min(want, n_rows)
    assert n_rows % t == 0, (n_rows, t)
    return t


def _mod_norm(x, mod_ref, g_ref, first):
    y = x * lax.rsqrt(jnp.mean(x * x, axis=-1, keepdims=True) + RMS_EPS) * g_ref[...]
    return y * (1.0 + mod_ref[first + 1:first + 2, :]) + mod_ref[first:first + 1, :]


def _mod_kernel(c_ref, w_ref, b_ref, o_ref):
    c = c_ref[...]
    s = (c * jax.nn.sigmoid(c)).astype(BF16)
    o_ref[...] = jnp.dot(s, w_ref[...].astype(BF16), preferred_element_type=F32) + b_ref[...]


def _modulation(cvecs, w_mod, b_mod):
    n_layers, d, nd = w_mod.shape
    rows = cvecs.shape[0]
    tn = _row_tile(nd, 2048)
    return pl.pallas_call(
        _mod_kernel,
        grid=(n_layers, nd // tn),
        in_specs=[pl.BlockSpec((rows, d), lambda l, n: (0, 0)),
                  pl.BlockSpec((None, d, tn), lambda l, n: (l, 0, n)),
                  pl.BlockSpec((None, 1, tn), lambda l, n: (l, 0, n))],
        out_specs=pl.BlockSpec((None, rows, tn), lambda l, n: (l, 0, n)),
        out_shape=jax.ShapeDtypeStruct((n_layers, rows, nd), F32),
        compiler_params=_params(("parallel", "parallel"), 2 * d * tn * 4 + (8 << 20)),
        name="adaln_modulation",
    )(cvecs, w_mod, b_mod.reshape(n_layers, 1, nd))


def _zero_like_row(v):
    words = pltpu.bitcast(v, jnp.uint32)
    step = SUBLANES
    folded = functools.reduce(jnp.bitwise_or, [words[r:r + step] for r in range(0, words.shape[0], step)])
    zeros = jnp.right_shift(jnp.right_shift(folded, jnp.uint32(16)), jnp.uint32(16))
    return jnp.sum(zeros.astype(jnp.int32).astype(F32), axis=0, keepdims=True)


def _ffn_kernel(*refs, first, tiles_per_mod, n_chunks, n_cast):
    x_hbm, mod_ref, g_ref, wa_ref, wb_ref, wo_ref = refs[:6]
    cast_in = refs[6:6 + n_cast]
    o_hbm = refs[6 + n_cast]
    cast_out = refs[7 + n_cast:7 + 2 * n_cast]
    xa_ref, xb_ref, ha_ref, hb_ref, in_sem, out_sem = refs[7 + 2 * n_cast:]
    i, f = pl.program_id(0), pl.program_id(1)
    n_tiles, nf = pl.num_programs(0), pl.num_programs(1)
    tm = xa_ref.shape[0]
    rows = tm // n_chunks
    head = tm - rows

    def x_in(tile, dst, part, slot):
        r0, n = ((0, head), (head, rows))[part]
        return pltpu.make_async_copy(x_hbm.at[pl.ds(tile * tm + r0, n), :], dst.at[pl.ds(r0, n), :],
                                     in_sem.at[slot, part])

    def x_out(tile, src, slot):
        return pltpu.make_async_copy(src, o_hbm.at[pl.ds(tile * tm, tm), :], out_sem.at[slot])

    def normed(x_rows, batch):
        return _mod_norm(x_rows, mod_ref.at[batch], g_ref, first).astype(BF16)

    def tile_step(slot, x_cur, x_nxt, h_cur, h_nxt):
        other = 1 - slot
        has_next = i + 1 < n_tiles

        if slot == 0:
            @pl.when((i == 0) & (f == 0))
            def _():
                for part in (0, 1):
                    x_in(0, x_cur, part, slot).start()
                x_nxt[...] = jnp.zeros(x_nxt.shape, x_nxt.dtype)
                for part in (0, 1):
                    x_in(0, x_cur, part, slot).wait()
                h_cur[...] = normed(x_cur[...], 0)

        @pl.when((f == 1) & (i > 0))
        def _():
            x_out(i - 1, x_nxt, other).wait()

        @pl.when((f == 1) & has_next)
        def _():
            x_in(i + 1, x_nxt, 0, other).start()

        @pl.when((f == 2) & has_next)
        def _():
            x_in(i + 1, x_nxt, 0, other).wait()
            x_in(i + 1, x_nxt, 1, other).start()

        @pl.when((f == 3) & has_next)
        def _():
            x_in(i + 1, x_nxt, 1, other).wait()

        h = h_cur[...]
        a = jnp.dot(h, wa_ref[...], preferred_element_type=F32)
        b = jnp.dot(h, wb_ref[...], preferred_element_type=F32)

        chunk = jnp.where((f >= 2) & (f < 2 + n_chunks), f - 2, n_chunks - 1)
        r0 = pl.multiple_of(chunk * rows, rows)
        next_batch = jnp.minimum(i + 1, n_tiles - 1) // tiles_per_mod
        h_rows = normed(x_nxt[pl.ds(r0, rows), :], next_batch)
        h_nxt[pl.ds(r0, rows), :] = h_rows

        zero_row = _zero_like_row(h_rows)
        gate = MACARON_WEIGHT * mod_ref[i // tiles_per_mod, first + 2:first + 3, :] + zero_row
        for src, dst in zip(cast_in, cast_out):
            dst[...] = (src[...] + zero_row[:, 0:1]).astype(dst.dtype)
        act = (a * jax.nn.sigmoid(a) * b).astype(BF16)
        x_cur[...] += gate * jnp.dot(act, wo_ref[...], preferred_element_type=F32)

        @pl.when(f == nf - 1)
        def _():
            x_out(i, x_cur, slot).start()

        @pl.when((f == nf - 1) & (i == n_tiles - 1))
        def _():
            x_out(i, x_cur, slot).wait()

    @pl.when(i % 2 == 0)
    def _():
        tile_step(0, xa_ref, xb_ref, ha_ref, hb_ref)

    @pl.when(i % 2 == 1)
    def _():
        tile_step(1, xb_ref, xa_ref, hb_ref, ha_ref)


def _ffn(x2, mod, first, norm_g, wi, wo, layer, cast=(), *, tm=1024, tf=512):
    n_tok, d = x2.shape
    d_ff = wo.shape[1]
    tm = _row_tile(n_tok // mod.shape[0], tm)
    tf = _row_tile(d_ff, tf)
    tiles_per_mod = n_tok // mod.shape[0] // tm
    nf = d_ff // tf
    n_steps = (n_tok // tm) * nf
    n_chunks = 1 << ((nf - 2).bit_length() - 1)
    assert nf >= 4 and tm % (n_chunks * 2 * SUBLANES) == 0, (nf, tm)

    cast_in_specs, cast_out_specs, cast_shapes = [], [], []
    for arr, lyr in cast:
        _, rows, cols = arr.shape
        block = next(r for r in range(2 * SUBLANES, rows + 1, 2 * SUBLANES)
                     if rows % r == 0 and rows // r <= n_steps)
        last = rows // block - 1
        cast_in_specs.append(pl.BlockSpec(
            (None, block, cols), lambda i, f, last=last, lyr=lyr: (lyr, jnp.minimum(i * nf + f, last), 0)))
        cast_out_specs.append(pl.BlockSpec(
            (None, block, cols), lambda i, f, last=last: (0, jnp.minimum(i * nf + f, last), 0)))
        cast_shapes.append(jax.ShapeDtypeStruct((1, rows, cols), BF16))

    vmem = 2 * tm * d * 4 + 2 * tm * d * 2 + 2 * 3 * d * tf * 2 + 8 * tm * tf * 4 + (6 << 20)
    out, *cast_out = pl.pallas_call(
        functools.partial(_ffn_kernel, first=first, tiles_per_mod=tiles_per_mod, n_chunks=n_chunks,
                          n_cast=len(cast)),
        grid=(n_tok // tm, nf),
        in_specs=[pl.BlockSpec(memory_space=pl.ANY),
                  pl.BlockSpec(mod.shape, lambda i, f: (0, 0, 0)),
                  pl.BlockSpec((1, d), lambda i, f: (0, 0)),
                  pl.BlockSpec((None, d, tf), lambda i, f: (layer, 0, f)),
                  pl.BlockSpec((None, d, tf), lambda i, f: (layer, 0, nf + f)),
                  pl.BlockSpec((None, tf, d), lambda i, f: (layer, f, 0))] + cast_in_specs,
        out_specs=[pl.BlockSpec(memory_space=pl.ANY)] + cast_out_specs,
        out_shape=[jax.ShapeDtypeStruct((n_tok, d), F32)] + cast_shapes,
        scratch_shapes=[pltpu.VMEM((tm, d), F32), pltpu.VMEM((tm, d), F32),
                        pltpu.VMEM((tm, d), BF16), pltpu.VMEM((tm, d), BF16),
                        pltpu.SemaphoreType.DMA((2, 2)), pltpu.SemaphoreType.DMA((2,))],
        compiler_params=_params(("arbitrary", "arbitrary"), vmem),
        name="macaron_ffn",
    )(x2, mod, norm_g.reshape(1, d), wi, wi, wo, *[arr for arr, _ in cast])
    return out, cast_out


def _head_norm(p, gain):
    return p * lax.rsqrt(jnp.mean(p * p, axis=-1, keepdims=True) + RMS_EPS) * gain


def _rotate(y, cos, sin_signed):
    lane = lax.broadcasted_iota(jnp.int32, y.shape, 1)
    partner = jnp.where((lane & 32) == 0, pltpu.roll(y, HEAD_DIM - 32, 1), pltpu.roll(y, 32, 1))
    return y * cos + partner * sin_signed


def _proj_kernel(*refs, use_rope, widths):
    if use_rope:
        x_ref, mod_ref, g_ref, w_ref, nag_ref, wag_ref, cos_ref, sin_ref = refs[:8]
        outs = refs[8:]
    else:
        x_ref, mod_ref, g_ref, w_ref, nag_ref, wag_ref = refs[:6]
        outs = refs[6:]
    nq_ref, nk_ref, nv_ref, u_ref, wq_ref, wk_ref, wv_ref = outs
    na_w, pool_w, wa_w, kv_w = widths
    h = _mod_norm(x_ref[...], mod_ref, g_ref, 3).astype(BF16)

    def chunk(col0, width):
        return jnp.dot(h, w_ref[:, col0:col0 + width], preferred_element_type=F32)

    def normed(p, gain, out_ref, out_col0, scale, rope):
        for j in range(p.shape[1] // HEAD_DIM):
            y = _head_norm(p[:, j * HEAD_DIM:(j + 1) * HEAD_DIM], gain)
            if rope:
                y = _rotate(y, cos_ref[...], sin_ref[...])
            if scale != 1.0:
                y = y * scale
            c0 = out_col0 + j * HEAD_DIM
            out_ref[:, c0:c0 + HEAD_DIM] = y.astype(out_ref.dtype)

    c_nq, c_nk, c_nv, c_u = 0, na_w, 2 * na_w, 3 * na_w
    c_wq = c_u + pool_w
    c_wk = c_wq + wa_w
    c_wv = c_wk + kv_w
    half = wa_w // 2
    normed(chunk(c_wq, half), wag_ref[0:1, :], wq_ref, 0, Q_SCALE, use_rope)
    normed(chunk(c_wq + half, half), wag_ref[0:1, :], wq_ref, half, Q_SCALE, use_rope)
    normed(chunk(c_wk, kv_w), wag_ref[1:2, :], wk_ref, 0, 1.0, use_rope)
    normed(chunk(c_nq, na_w), nag_ref[0:1, :], nq_ref, 0, Q_SCALE, False)
    normed(chunk(c_nk, na_w), nag_ref[1:2, :], nk_ref, 0, 1.0, False)
    nv_ref[...] = chunk(c_nv, na_w).astype(BF16)
    u_ref[...] = chunk(c_u, pool_w)
    wv_ref[...] = chunk(c_wv, kv_w).astype(BF16)


def _projection(x2, mod, norm_g, w_in, layer, na_gain, wa_gain, rope, widths, *, tm=512):
    n_tok, d = x2.shape
    na_w, pool_w, wa_w, kv_w = widths
    n_in = w_in.shape[2]
    tm = _row_tile(n_tok, tm)
    tiles_per_mod = n_tok // mod.shape[0] // tm
    use_rope = rope is not None
    in_specs = [pl.BlockSpec((tm, d), lambda i: (i, 0)),
                pl.BlockSpec((None, N_MOD, d), lambda i: (i // tiles_per_mod, 0, 0)),
                pl.BlockSpec((1, d), lambda i: (0, 0)),
                pl.BlockSpec((None, d, n_in), lambda i: (layer, 0, 0)),
                pl.BlockSpec((2, HEAD_DIM), lambda i: (0, 0)),
                pl.BlockSpec((2, HEAD_DIM), lambda i: (0, 0))]
    args = [x2, mod, norm_g.reshape(1, d), w_in, na_gain, wa_gain]
    if use_rope:
        tiles_per_seq = rope[0].shape[0] // tm
        in_specs += [pl.BlockSpec((tm, HEAD_DIM), lambda i: (i % tiles_per_seq, 0))] * 2
        args += list(rope)
    out_widths = (na_w, na_w, na_w, pool_w, wa_w, kv_w, kv_w)
    out_dtypes = (BF16, BF16, BF16, F32, BF16, BF16, BF16)
    vmem = 2 * tm * d * 4 + tm * d * 2 + 2 * d * n_in * 2 + 4 * tm * n_in * 4 + (4 << 20)
    return pl.pallas_call(
        functools.partial(_proj_kernel, use_rope=use_rope, widths=widths),
        grid=(n_tok // tm,),
        in_specs=in_specs,
        out_specs=[pl.BlockSpec((tm, w), lambda i: (i, 0)) for w in out_widths],
        out_shape=[jax.ShapeDtypeStruct((n_tok, w), dt) for w, dt in zip(out_widths, out_dtypes)],
        compiler_params=_params(("parallel",), vmem),
        name="input_projection",
    )(*args)


def _softmax_weights(s, extra_logit=None):
    m = jnp.max(s, axis=-1, keepdims=True)
    if extra_logit is not None:
        m = jnp.maximum(m, extra_logit)
    p = jnp.exp2(s - m)
    denom = jnp.sum(p, axis=-1, keepdims=True)
    if extra_logit is not None:
        denom = denom + jnp.exp2(extra_logit - m)
    return p.astype(BF16), denom


def _attend(scores, values, extra_logits=None):
    extra_logits = extra_logits or [None] * len(scores)
    weights = [_softmax_weights(s, e) for s, e in zip(scores, extra_logits)]
    return [jnp.dot(p, v, preferred_element_type=F32) / denom for (p, denom), v in zip(weights, values)]


def _qk(q, k):
    return lax.dot_general(q, k, (((1,), (1,)), ((), ())), preferred_element_type=F32)


def _na_kernel(q_ref, kp_ref, kc_ref, kn_ref, vp_ref, vc_ref, vn_ref, kx_ref, vx_ref, bias_ref, o_ref):
    j, n_tiles = pl.program_id(1), pl.num_programs(1)
    blk = NA_Q_ROWS * GRID_W
    q_blocks = q_ref.shape[0] // blk
    n_heads = q_ref.shape[1] // HEAD_DIM
    heads = [slice(h * HEAD_DIM, (h + 1) * HEAD_DIM) for h in range(n_heads)]
    rows = lambda t: slice(t * blk, (t + 1) * blk)

    def window(halo_before, tile, halo_after, qb, hs):
        before = halo_before[:, hs] if qb == 0 else tile[rows(qb - 1), hs]
        after = halo_after[:, hs] if qb == q_blocks - 1 else tile[rows(qb + 1), hs]
        return [before, tile[rows(qb), hs], after]

    scores, values = [], []
    for qb in range(q_blocks):
        variant = 1
        if qb == 0:
            variant = jnp.where(j == 0, 0, variant)
        if qb == q_blocks - 1:
            variant = jnp.where(j == n_tiles - 1, 2, variant)
        for h, hs in enumerate(heads):
            k = jnp.concatenate(window(kp_ref, kc_ref, kn_ref, qb, hs) + [kx_ref[:, hs]], axis=0)
            scores.append(_qk(q_ref[rows(qb), hs], k) + bias_ref[variant, h])
            values.append(jnp.concatenate(window(vp_ref, vc_ref, vn_ref, qb, hs) + [vx_ref[:, hs]], axis=0))
    for e, out in enumerate(_attend(scores, values)):
        qb, h = divmod(e, n_heads)
        o_ref[rows(qb), heads[h]] = out.astype(o_ref.dtype)


def _na_bias_kernel(rpb_ref, o_ref, g_ref, *, n_rows):
    n_ro, n_co = 2 * NA_WIN_H - 1, 2 * NA_WIN_W - 1
    shape = (GRID_W, 2 * GRID_W)
    qc = lax.broadcasted_iota(jnp.int32, shape, 0)
    lane = lax.broadcasted_iota(jnp.int32, shape, 1)
    kc = lane & (GRID_W - 1)
    c0 = jnp.clip(qc - NA_WIN_W // 2, 0, GRID_W - NA_WIN_W)
    in_cols = (kc >= c0) & (kc < c0 + NA_WIN_W)
    col_off = kc - qc + (NA_WIN_W - 1)
    base = pl.program_id(0) * (n_ro * n_co)
    for ro in range(n_ro):
        g = jnp.zeros(shape, F32)
        for c in range(n_co):
            g = jnp.where(col_off == c, rpb_ref[base + ro * n_co + c] * LOG2E, g)
        g_ref[ro] = jnp.where(in_cols, g, NEG_INF)

    masked = jnp.full(shape, NEG_INF, F32)
    n_local = 3 * NA_Q_ROWS * GRID_W
    n_blocks = n_rows // NA_Q_ROWS
    o_ref[:, :, n_local:] = jnp.zeros(o_ref.shape[:2] + (o_ref.shape[2] - n_local,), F32)
    for variant, jb in enumerate((0, 1, n_blocks - 1)):
        for i in range(NA_Q_ROWS):
            r = jb * NA_Q_ROWS + i
            r0 = min(max(r - NA_WIN_H // 2, 0), n_rows - NA_WIN_H)
            halves = []
            for kr in range(3 * NA_Q_ROWS):
                kr_abs = (jb - 1) * NA_Q_ROWS + kr
                in_rows = 0 <= kr_abs < n_rows and r0 <= kr_abs < r0 + NA_WIN_H
                halves.append(g_ref[kr_abs - r + NA_WIN_H - 1] if in_rows else masked)
            for m in range(len(halves) // 2):
                o_ref[variant, i * GRID_W:(i + 1) * GRID_W, m * 2 * GRID_W:(m + 1) * 2 * GRID_W] = jnp.where(
                    lane < GRID_W, halves[2 * m], halves[2 * m + 1])


def _na_bias_table(rpb, n_rows, n_ctx):
    n_heads = rpb.shape[0]
    assert n_rows % NA_Q_ROWS == 0 and n_rows // NA_Q_ROWS >= 3 and (3 * NA_Q_ROWS) % 2 == 0
    shape = (3, n_heads, NA_Q_ROWS * GRID_W, 3 * NA_Q_ROWS * GRID_W + n_ctx)
    return pl.pallas_call(
        functools.partial(_na_bias_kernel, n_rows=n_rows),
        grid=(n_heads,),
        in_specs=[pl.BlockSpec(memory_space=pltpu.SMEM)],
        out_specs=pl.BlockSpec((3, None) + shape[2:], lambda h: (0, h, 0, 0)),
        out_shape=jax.ShapeDtypeStruct(shape, F32),
        scratch_shapes=[pltpu.VMEM((2 * NA_WIN_H - 1, GRID_W, 2 * GRID_W), F32)],
        compiler_params=_params(("parallel",), 24 << 20),
        name="na_bias_table",
    )(rpb.astype(F32).reshape(-1))


def _neighbourhood_attention(q, k, v, kx, vx, bias, *, q_blocks=NA_Q_BLOCKS):
    b, s, w = q.shape
    blk = NA_Q_ROWS * GRID_W
    n_halo = s // blk
    rows = q_blocks * blk
    assert s % rows == 0
    cx = kx.shape[1]
    cur = lambda bb, j: (bb, j, 0)
    prev = lambda bb, j: (bb, jnp.maximum(j * q_blocks - 1, 0), 0)
    nxt = lambda bb, j: (bb, jnp.minimum((j + 1) * q_blocks, n_halo - 1), 0)
    halo = lambda im: pl.BlockSpec((None, blk, w), im)
    tile = pl.BlockSpec((None, rows, w), cur)
    ctx = pl.BlockSpec((None, cx, w), lambda bb, j: (bb, 0, 0))
    vmem = 2 * bias.size * 4 + 8 * q_blocks * blk * bias.shape[3] * 4 + (8 << 20)
    return pl.pallas_call(
        _na_kernel,
        grid=(b, s // rows),
        in_specs=[tile, halo(prev), tile, halo(nxt), halo(prev), tile, halo(nxt), ctx, ctx,
                  pl.BlockSpec(bias.shape, lambda bb, j: (0, 0, 0, 0))],
        out_specs=tile,
        out_shape=jax.ShapeDtypeStruct((b, s, w), BF16),
        compiler_params=_params(("parallel", "parallel"), vmem),
        name="neighbourhood_attention",
    )(q, k, k, k, v, v, v, kx, vx, bias)


def _wa_band_bias(n_ctx):
    blk = WA_BLOCK
    p = np.arange(blk)[:, None]
    k = np.arange(3 * blk + n_ctx)[None, :]
    in_band = (k >= p) & (k <= p + 2 * blk)
    is_prev, is_next, is_ctx = k < blk, (k >= 2 * blk) & (k < 3 * blk), k >= 3 * blk
    tables = []
    for has_prev, has_next in ((False, True), (True, True), (True, False)):
        valid = is_ctx | (in_band & (has_prev | ~is_prev) & (has_next | ~is_next))
        tables.append(np.where(valid, 0.0, NEG_INF).astype(np.float32))
    return np.stack(tables)


def _wa_kernel(sink_ref, band_ref, q_ref, kp_ref, kc_ref, kn_ref, vp_ref, vc_ref, vn_ref, kx_ref, vx_ref,
               o_ref, *, group):
    j, n_tiles = pl.program_id(1), pl.num_programs(1)
    blk = WA_BLOCK
    q_blocks = q_ref.shape[0] // blk
    n_kv = kc_ref.shape[1] // HEAD_DIM

    def window(halo_before, tile, halo_after, qb, hs):
        rows = lambda t: slice(t * blk, (t + 1) * blk)
        before = halo_before[:, hs] if qb == 0 else tile[rows(qb - 1), hs]
        after = halo_after[:, hs] if qb == q_blocks - 1 else tile[rows(qb + 1), hs]
        return [before, tile[rows(qb), hs], after]

    scores, values, sinks = [], [], []
    for qb in range(q_blocks):
        variant = 1
        if qb == 0:
            variant = jnp.where(j == 0, 0, variant)
        if qb == q_blocks - 1:
            variant = jnp.where(j == n_tiles - 1, 2, variant)
        band = jnp.concatenate([band_ref[variant]] * group, axis=0)
        for kv in range(n_kv):
            hs = slice(kv * HEAD_DIM, (kv + 1) * HEAD_DIM)
            heads = [kv * group + g for g in range(group)]
            q = jnp.concatenate([q_ref[qb * blk:(qb + 1) * blk, h * HEAD_DIM:(h + 1) * HEAD_DIM] for h in heads],
                                axis=0)
            k = jnp.concatenate(window(kp_ref, kc_ref, kn_ref, qb, hs) + [kx_ref[:, hs]], axis=0)
            scores.append(_qk(q, k) + band)
            values.append(jnp.concatenate(window(vp_ref, vc_ref, vn_ref, qb, hs) + [vx_ref[:, hs]], axis=0))
            sinks.append(jnp.concatenate([jnp.full((blk, 1), sink_ref[h] * LOG2E, F32) for h in heads], axis=0))
    for e, out in enumerate(_attend(scores, values, sinks)):
        qb, kv = divmod(e, n_kv)
        for g in range(group):
            h = kv * group + g
            o_ref[qb * blk:(qb + 1) * blk, h * HEAD_DIM:(h + 1) * HEAD_DIM] = (
                out[g * blk:(g + 1) * blk, :].astype(o_ref.dtype))


def _window_attention(q, k, v, kx, vx, sink, *, q_blocks=WA_Q_BLOCKS):
    b, s, wq = q.shape
    wk = k.shape[2]
    group = wq // wk
    rows = q_blocks * WA_BLOCK
    n_tiles = s // rows
    n_halo = s // WA_BLOCK
    cx = kx.shape[1]
    assert s % rows == 0 and n_halo >= 2
    cur = lambda bb, j: (bb, j, 0)
    prev = lambda bb, j: (bb, jnp.maximum(j * q_blocks - 1, 0), 0)
    nxt = lambda bb, j: (bb, jnp.minimum((j + 1) * q_blocks, n_halo - 1), 0)
    halo = lambda im: pl.BlockSpec((None, WA_BLOCK, wk), im)
    tile = pl.BlockSpec((None, rows, wk), cur)
    ctx = pl.BlockSpec((None, cx, wk), lambda bb, j: (bb, 0, 0))
    band = jnp.asarray(_wa_band_bias(cx))
    vmem = 16 * q_blocks * group * WA_BLOCK * band.shape[2] * 4 + (8 << 20)
    return pl.pallas_call(
        functools.partial(_wa_kernel, group=group),
        grid=(b, n_tiles),
        in_specs=[pl.BlockSpec(memory_space=pltpu.SMEM),
                  pl.BlockSpec(band.shape, lambda bb, j: (0, 0, 0)),
                  pl.BlockSpec((None, rows, wq), cur),
                  halo(prev), tile, halo(nxt), halo(prev), tile, halo(nxt), ctx, ctx],
        out_specs=pl.BlockSpec((None, rows, wq), cur),
        out_shape=jax.ShapeDtypeStruct((b, s, wq), BF16),
        compiler_params=_params(("parallel", "parallel"), vmem),
        name="window_attention",
    )(sink, band, q, k, k, k, v, v, v, kx, vx)


def _ctx_attn_kernel(*refs, group, use_sink):
    if use_sink:
        sink_ref, q_ref, k_ref, v_ref, o_ref = refs
    else:
        q_ref, k_ref, v_ref, o_ref = refs
    c = q_ref.shape[0]
    for kv in range(k_ref.shape[1] // HEAD_DIM):
        hs = slice(kv * HEAD_DIM, (kv + 1) * HEAD_DIM)
        heads = [kv * group + g for g in range(group)]
        q = jnp.concatenate([q_ref[:, h * HEAD_DIM:(h + 1) * HEAD_DIM] for h in heads], axis=0)
        sink = None
        if use_sink:
            sink = jnp.concatenate([jnp.full((c, 1), sink_ref[h] * LOG2E, F32) for h in heads], axis=0)
        out, = _attend([_qk(q, k_ref[:, hs])], [v_ref[:, hs]], [sink])
        for g, h in enumerate(heads):
            o_ref[:, h * HEAD_DIM:(h + 1) * HEAD_DIM] = out[g * c:(g + 1) * c, :].astype(o_ref.dtype)


def _context_attention(q, k, v, sink):
    b, c, wq = q.shape
    wk = k.shape[2]
    use_sink = sink is not None
    spec = lambda w: pl.BlockSpec((None, c, w), lambda bb: (bb, 0, 0))
    in_specs = [spec(wq), spec(wk), spec(wk)]
    args = [q, k, v]
    if use_sink:
        in_specs = [pl.BlockSpec(memory_space=pltpu.SMEM)] + in_specs
        args = [sink] + args
    return pl.pallas_call(
        functools.partial(_ctx_attn_kernel, group=wq // wk, use_sink=use_sink),
        grid=(b,),
        in_specs=in_specs,
        out_specs=spec(wq),
        out_shape=jax.ShapeDtypeStruct((b, c, wq), BF16),
        compiler_params=_params(("parallel",), 24 << 20),
        name="context_attention",
    )(*args)


def _out_kernel(x_ref, mod_ref, a_ref, u_ref, up_ref, un_ref, c_ref, pw_ref, ps_ref, wo_ref, o_ref, ext_ref,
                *, seq_len):
    tm = x_ref.shape[0]
    tiles_per_seq = seq_len // tm
    t_in_seq = pl.program_id(0) % tiles_per_seq
    halo = POOL_HALO
    has_prev = (t_in_seq > 0).astype(F32)
    has_next = (t_in_seq < tiles_per_seq - 1).astype(F32)
    ext_ref[0:halo, :] = up_ref[...] * has_prev
    ext_ref[halo:halo + tm, :] = u_ref[...]
    ext_ref[halo + tm:, :] = un_ref[...] * has_next

    pos = t_in_seq * tm + lax.broadcasted_iota(jnp.int32, (tm, 1), 0)
    parts = []
    for g, win in enumerate(POOL_WINDOWS):
        ls = slice(g * HEAD_DIM, (g + 1) * HEAD_DIM)
        total = functools.reduce(jnp.add, [ext_ref[halo + d:halo + d + tm, ls]
                                           for d in range(-(win // 2), win - win // 2)])
        lo = jnp.maximum(pos - win // 2, 0)
        hi = jnp.minimum(pos + (win - win // 2 - 1), seq_len - 1)
        delta = total / (hi - lo + 1).astype(F32) - u_ref[:, ls]
        y = jnp.dot(delta.astype(BF16), pw_ref[g], preferred_element_type=F32) * ps_ref[:, ls]
        parts.append(y.astype(BF16))
    na_w = a_ref.shape[1]
    pool_w = len(POOL_WINDOWS) * HEAD_DIM
    mixed = jnp.dot(a_ref[...], wo_ref[0:na_w, :], preferred_element_type=F32)
    mixed += jnp.dot(c_ref[...], wo_ref[na_w + pool_w:, :], preferred_element_type=F32)
    mixed += jnp.dot(jnp.concatenate(parts, axis=1), wo_ref[na_w:na_w + pool_w, :], preferred_element_type=F32)
    o_ref[...] = x_ref[...] + mod_ref[5:6, :] * mixed


def _mix_out(x2, mod, a2, u2, c2, pool_w, pool_layer, pool_scale, w_out, seq_len, *, tm=512):
    n_tok, d = x2.shape
    tm = _row_tile(seq_len, tm)
    tiles_per_mod = n_tok // mod.shape[0] // tm
    n_halo = n_tok // POOL_HALO
    hpt = tm // POOL_HALO
    row = lambda w: pl.BlockSpec((tm, w), lambda i: (i, 0))
    vmem = 4 * tm * d * 4 + 2 * w_out[0].size * 2 + 4 * tm * d * 4 + (4 << 20)
    return pl.pallas_call(
        functools.partial(_out_kernel, seq_len=seq_len),
        grid=(n_tok // tm,),
        in_specs=[row(d),
                  pl.BlockSpec((None, N_MOD, d), lambda i: (i // tiles_per_mod, 0, 0)),
                  row(a2.shape[1]), row(u2.shape[1]),
                  pl.BlockSpec((POOL_HALO, u2.shape[1]), lambda i: (jnp.maximum(i * hpt - 1, 0), 0)),
                  pl.BlockSpec((POOL_HALO, u2.shape[1]), lambda i: (jnp.minimum((i + 1) * hpt, n_halo - 1), 0)),
                  row(c2.shape[1]),
                  pl.BlockSpec((None,) + pool_w.shape[1:], lambda i: (pool_layer, 0, 0, 0)),
                  pl.BlockSpec((1, u2.shape[1]), lambda i: (0, 0)),
                  pl.BlockSpec((None,) + w_out.shape[1:], lambda i: (0, 0, 0))],
        out_specs=row(d),
        out_shape=jax.ShapeDtypeStruct((n_tok, d), F32),
        scratch_shapes=[pltpu.VMEM((tm + 2 * POOL_HALO, u2.shape[1]), F32)],
        compiler_params=_params(("parallel",), vmem),
        name="pool_mix_out",
    )(x2, mod, a2, u2, u2, u2, c2, pool_w, pool_scale.reshape(1, -1), w_out)


def _rope_tables(seq_len):
    n_rows = seq_len // GRID_W
    axis_dim = HEAD_DIM // 2
    inv_freq = ROPE_BASE ** (-jnp.arange(0, axis_dim, 2, dtype=F32) / axis_dim)
    ang_r = jnp.arange(n_rows, dtype=F32)[:, None] * inv_freq[None, :]
    ang_c = jnp.arange(GRID_W, dtype=F32)[:, None] * inv_freq[None, :]
    by_row = lambda t: jnp.broadcast_to(t[:, None, :], (n_rows, GRID_W, t.shape[-1]))
    by_col = lambda t: jnp.broadcast_to(t[None, :, :], (n_rows, GRID_W, t.shape[-1]))
    cos_r, sin_r = by_row(jnp.cos(ang_r)), by_row(jnp.sin(ang_r))
    cos_c, sin_c = by_col(jnp.cos(ang_c)), by_col(jnp.sin(ang_c))
    cos = jnp.concatenate([cos_r, cos_r, cos_c, cos_c], axis=-1).reshape(seq_len, HEAD_DIM)
    sin = jnp.concatenate([-sin_r, sin_r, -sin_c, sin_c], axis=-1).reshape(seq_len, HEAD_DIM)
    return cos, sin


def kernel(x, c, ctx, c_ctx, w_mod, b_mod, norm_w, ffn1_wi, ffn1_wo, ffn2_wi, ffn2_wo, w_in, w_out,
           na_qk_gain, na_rpb, pool_w, pool_scale, wa_qk_gain, wa_sink):
    b, s, d = x.shape
    cx = ctx.shape[1]
    depth = w_mod.shape[0]
    na_w = na_rpb.shape[1] * HEAD_DIM
    pw = pool_scale.shape[1]
    kv_w = (w_in.shape[2] - 3 * na_w - pw - (d - na_w - pw)) // 2
    widths = (na_w, pw, d - na_w - pw, kv_w)

    n_cond = b + 1
    cvecs = jnp.concatenate([c, c_ctx[None, :], jnp.zeros((-n_cond % SUBLANES, d), F32)], axis=0)
    mods = _modulation(cvecs, w_mod, b_mod).reshape(depth, cvecs.shape[0], N_MOD, d)
    rope = _rope_tables(s)
    wi1, wo1 = ffn1_wi[:1].astype(BF16), ffn1_wo[:1].astype(BF16)
    pwb = pool_w.astype(BF16)

    x2 = x.reshape(b * s, d)
    c2 = ctx.reshape(b * cx, d)
    seq = lambda t: t.reshape(b, s, -1)
    cseq = lambda t: t.reshape(b, cx, -1)
    for l in range(depth):
        last = l == depth - 1
        mx, mc = mods[l, :b], mods[l, b:b + 1]

        x2, (win, wout, wi2, wo2) = _ffn(x2, mx, 0, norm_w[l, 0], wi1, wo1, 0,
                                         cast=[(w_in, l), (w_out, l), (ffn2_wi, l), (ffn2_wo, l)])
        c2, _ = _ffn(c2, mc, 0, norm_w[l, 0], wi1, wo1, 0)

        nq, nk, nv, u, wq, wk, wv = _projection(x2, mx, norm_w[l, 1], win, 0, na_qk_gain[l], wa_qk_gain[l],
                                                rope, widths)
        cnq, cnk, cnv, cu, cwq, cwk, cwv = _projection(c2, mc, norm_w[l, 1], win, 0, na_qk_gain[l],
                                                       wa_qk_gain[l], None, widths)
        a_out = _neighbourhood_attention(seq(nq), seq(nk), seq(nv), cseq(cnk), cseq(cnv),
                                         _na_bias_table(na_rpb[l], s // GRID_W, cx))
        c_out = _window_attention(seq(wq), seq(wk), seq(wv), cseq(cwk), cseq(cwv), wa_sink[l])
        x2 = _mix_out(x2, mx, a_out.reshape(b * s, -1), u, c_out.reshape(b * s, -1), pwb, l, pool_scale[l],
                      wout, s)
        if not last:
            ca = _context_attention(cseq(cnq), cseq(cnk), cseq(cnv), None)
            cc = _context_attention(cseq(cwq), cseq(cwk), cseq(cwv), wa_sink[l])
            c2 = _mix_out(c2, mc, ca.reshape(b * cx, -1), cu, cc.reshape(b * cx, -1), pwb, l, pool_scale[l],
                          wout, cx)
            c2, _ = _ffn(c2, mc, 6, norm_w[l, 2], wi2, wo2, 0)
        x2, nxt = _ffn(x2, mx, 6, norm_w[l, 2], wi2, wo2, 0,
                       cast=[] if last else [(ffn1_wi, l + 1), (ffn1_wo, l + 1)])
        if not last:
            wi1, wo1 = nxt
    return x2.reshape(b, s, d)
```

```python
import functools

import numpy as np
import jax
import jax.numpy as jnp
from jax import lax
from jax.experimental import pallas as pl
from jax.experimental.pallas import tpu as pltpu

HEAD_DIM = 128
GRID_W = 64
NA_WIN_H = 8
NA_WIN_W = 16
POOL_WINDOWS = (2, 4, 8, 16)
WA_BLOCK = 128
WA_Q_BLOCKS = 4
N_MOD = 9
MACARON_WEIGHT = 0.5
ROPE_BASE = 10000.0
RMS_EPS = 1e-6
NEG_INF = -1e30
LOG2E = 1.4426950408889634
Q_SCALE = HEAD_DIM ** -0.5 * LOG2E

V7X_VMEM_BYTES = 64 * 1024 * 1024
V7X_VMEM_BUDGET = 56 * 1024 * 1024
SUBLANES = 8
NA_Q_ROWS = NA_WIN_H // 2
NA_Q_BLOCKS = 4
POOL_HALO = 16

BF16 = jnp.bfloat16
F32 = jnp.float32


def _params(semantics, vmem_bytes):
    return pltpu.CompilerParams(dimension_semantics=semantics,
                                vmem_limit_bytes=int(min(max(vmem_bytes, 16 << 20), V7X_VMEM_BUDGET)))


def _row_tile(n_rows, want):
    t = min(want, n_rows)
    assert n_rows % t == 0, (n_rows, t)
    return t


def _mod_norm(x, mod_ref, g_ref, first):
    y = x * lax.rsqrt(jnp.mean(x * x, axis=-1, keepdims=True) + RMS_EPS) * g_ref[...]
    return y * (1.0 + mod_ref[first + 1:first + 2, :]) + mod_ref[first:first + 1, :]


def _mod_kernel(c_ref, w_ref, b_ref, o_ref):
    c = c_ref[...]
    s = (c * jax.nn.sigmoid(c)).astype(BF16)
    o_ref[...] = jnp.dot(s, w_ref[...].astype(BF16), preferred_element_type=F32) + b_ref[...]


def _modulation(cvecs, w_mod, b_mod):
    n_layers, d, nd = w_mod.shape
    rows = cvecs.shape[0]
    tn = _row_tile(nd, 2048)
    return pl.pallas_call(
        _mod_kernel,
        grid=(n_layers, nd // tn),
        in_specs=[pl.BlockSpec((rows, d), lambda l, n: (0, 0)),
                  pl.BlockSpec((None, d, tn), lambda l, n: (l, 0, n)),
                  pl.BlockSpec((None, 1, tn), lambda l, n: (l, 0, n))],
        out_specs=pl.BlockSpec((None, rows, tn), lambda l, n: (l, 0, n)),
        out_shape=jax.ShapeDtypeStruct((n_layers, rows, nd), F32),
        compiler_params=_params(("parallel", "parallel"), 2 * d * tn * 4 + (8 << 20)),
        name="adaln_modulation",
    )(cvecs, w_mod, b_mod.reshape(n_layers, 1, nd))


def _ffn_kernel(*refs, first, tiles_per_mod, n_chunks, n_cast):
    x_hbm, mod_ref, g_ref, wa_ref, wb_ref, wo_ref = refs[:6]
    cast_in = refs[6:6 + n_cast]
    o_hbm = refs[6 + n_cast]
    cast_out = refs[7 + n_cast:7 + 2 * n_cast]
    xa_ref, xb_ref, ha_ref, hb_ref, in_sem, out_sem = refs[7 + 2 * n_cast:]
    i, f = pl.program_id(0), pl.program_id(1)
    n_tiles, nf = pl.num_programs(0), pl.num_programs(1)
    tm = xa_ref.shape[0]
    rows = tm // n_chunks
    head = tm - rows

    def x_in(tile, dst, part, slot):
        r0, n = ((0, head), (head, rows))[part]
        return pltpu.make_async_copy(x_hbm.at[pl.ds(tile * tm + r0, n), :], dst.at[pl.ds(r0, n), :],
                                     in_sem.at[slot, part])

    def x_out(tile, src, slot):
        return pltpu.make_async_copy(src, o_hbm.at[pl.ds(tile * tm, tm), :], out_sem.at[slot])

    def normed(x_rows, batch):
        return _mod_norm(x_rows, mod_ref.at[batch], g_ref, first).astype(BF16)

    def tile_step(slot, x_cur, x_nxt, h_cur, h_nxt):
        other = 1 - slot
        has_next = i + 1 < n_tiles

        if slot == 0:
            @pl.when((i == 0) & (f == 0))
            def _():
                for part in (0, 1):
                    x_in(0, x_cur, part, slot).start()
                x_nxt[...] = jnp.zeros(x_nxt.shape, x_nxt.dtype)
                for part in (0, 1):
                    x_in(0, x_cur, part, slot).wait()
                h_cur[...] = normed(x_cur[...], 0)

        @pl.when((f == 1) & (i > 0))
        def _():
            x_out(i - 1, x_nxt, other).wait()

        @pl.when((f == 1) & has_next)
        def _():
            x_in(i + 1, x_nxt, 0, other).start()

        @pl.when((f == 2) & has_next)
        def _():
            x_in(i + 1, x_nxt, 0, other).wait()
            x_in(i + 1, x_nxt, 1, other).start()

        @pl.when((f == 3) & has_next)
        def _():
            x_in(i + 1, x_nxt, 1, other).wait()

        for src, dst in zip(cast_in, cast_out):
            dst[...] = src[...].astype(dst.dtype)

        chunk = jnp.where((f >= 2) & (f < 2 + n_chunks), f - 2, n_chunks - 1)
        r0 = pl.multiple_of(chunk * rows, rows)
        next_batch = jnp.minimum(i + 1, n_tiles - 1) // tiles_per_mod
        h_nxt[pl.ds(r0, rows), :] = normed(x_nxt[pl.ds(r0, rows), :], next_batch)

        gate = MACARON_WEIGHT * mod_ref[i // tiles_per_mod, first + 2:first + 3, :]
        h = h_cur[...]
        a = jnp.dot(h, wa_ref[...], preferred_element_type=F32)
        b = jnp.dot(h, wb_ref[...], preferred_element_type=F32)
        act = (a * jax.nn.sigmoid(a) * b).astype(BF16)
        x_cur[...] += gate * jnp.dot(act, wo_ref[...], preferred_element_type=F32)

        @pl.when(f == nf - 1)
        def _():
            x_out(i, x_cur, slot).start()

        @pl.when((f == nf - 1) & (i == n_tiles - 1))
        def _():
            x_out(i, x_cur, slot).wait()

    @pl.when(i % 2 == 0)
    def _():
        tile_step(0, xa_ref, xb_ref, ha_ref, hb_ref)

    @pl.when(i % 2 == 1)
    def _():
        tile_step(1, xb_ref, xa_ref, hb_ref, ha_ref)


def _ffn(x2, mod, first, norm_g, wi, wo, layer, cast=(), *, tm=1024, tf=512):
    n_tok, d = x2.shape
    d_ff = wo.shape[1]
    tm = _row_tile(n_tok // mod.shape[0], tm)
    tf = _row_tile(d_ff, tf)
    tiles_per_mod = n_tok // mod.shape[0] // tm
    nf = d_ff // tf
    n_steps = (n_tok // tm) * nf
    n_chunks = 1 << ((nf - 2).bit_length() - 1)
    assert nf >= 4 and tm % (n_chunks * 2 * SUBLANES) == 0, (nf, tm)

    cast_in_specs, cast_out_specs, cast_shapes = [], [], []
    for arr, lyr in cast:
        _, rows, cols = arr.shape
        block = next(r for r in range(2 * SUBLANES, rows + 1, 2 * SUBLANES)
                     if rows % r == 0 and rows // r <= n_steps)
        last = rows // block - 1
        cast_in_specs.append(pl.BlockSpec(
            (None, block, cols), lambda i, f, last=last, lyr=lyr: (lyr, jnp.minimum(i * nf + f, last), 0)))
        cast_out_specs.append(pl.BlockSpec(
            (None, block, cols), lambda i, f, last=last: (0, jnp.minimum(i * nf + f, last), 0)))
        cast_shapes.append(jax.ShapeDtypeStruct((1, rows, cols), BF16))

    vmem = 2 * tm * d * 4 + 2 * tm * d * 2 + 2 * 3 * d * tf * 2 + 8 * tm * tf * 4 + (6 << 20)
    out, *cast_out = pl.pallas_call(
        functools.partial(_ffn_kernel, first=first, tiles_per_mod=tiles_per_mod, n_chunks=n_chunks,
                          n_cast=len(cast)),
        grid=(n_tok // tm, nf),
        in_specs=[pl.BlockSpec(memory_space=pl.ANY),
                  pl.BlockSpec(mod.shape, lambda i, f: (0, 0, 0)),
                  pl.BlockSpec((1, d), lambda i, f: (0, 0)),
                  pl.BlockSpec((None, d, tf), lambda i, f: (layer, 0, f)),
                  pl.BlockSpec((None, d, tf), lambda i, f: (layer, 0, nf + f)),
                  pl.BlockSpec((None, tf, d), lambda i, f: (layer, f, 0))] + cast_in_specs,
        out_specs=[pl.BlockSpec(memory_space=pl.ANY)] + cast_out_specs,
        out_shape=[jax.ShapeDtypeStruct((n_tok, d), F32)] + cast_shapes,
        scratch_shapes=[pltpu.VMEM((tm, d), F32), pltpu.VMEM((tm, d), F32),
                        pltpu.VMEM((tm, d), BF16), pltpu.VMEM((tm, d), BF16),
                        pltpu.SemaphoreType.DMA((2, 2)), pltpu.SemaphoreType.DMA((2,))],
        compiler_params=_params(("arbitrary", "arbitrary"), vmem),
        name="macaron_ffn",
    )(x2, mod, norm_g.reshape(1, d), wi, wi, wo, *[arr for arr, _ in cast])
    return out, cast_out


def _head_norm(p, gain):
    return p * lax.rsqrt(jnp.mean(p * p, axis=-1, keepdims=True) + RMS_EPS) * gain


def _rotate(y, cos, sin_signed):
    lane = lax.broadcasted_iota(jnp.int32, y.shape, 1)
    partner = jnp.where((lane & 32) == 0, pltpu.roll(y, HEAD_DIM - 32, 1), pltpu.roll(y, 32, 1))
    return y * cos + partner * sin_signed


def _proj_kernel(*refs, use_rope, widths):
    if use_rope:
        x_ref, mod_ref, g_ref, w_ref, nag_ref, wag_ref, cos_ref, sin_ref = refs[:8]
        outs = refs[8:]
    else:
        x_ref, mod_ref, g_ref, w_ref, nag_ref, wag_ref = refs[:6]
        outs = refs[6:]
    nq_ref, nk_ref, nv_ref, u_ref, wq_ref, wk_ref, wv_ref = outs
    na_w, pool_w, wa_w, kv_w = widths
    h = _mod_norm(x_ref[...], mod_ref, g_ref, 3).astype(BF16)

    def chunk(col0, width):
        return jnp.dot(h, w_ref[:, col0:col0 + width], preferred_element_type=F32)

    def normed(p, gain, out_ref, out_col0, scale, rope):
        for j in range(p.shape[1] // HEAD_DIM):
            y = _head_norm(p[:, j * HEAD_DIM:(j + 1) * HEAD_DIM], gain)
            if rope:
                y = _rotate(y, cos_ref[...], sin_ref[...])
            if scale != 1.0:
                y = y * scale
            c0 = out_col0 + j * HEAD_DIM
            out_ref[:, c0:c0 + HEAD_DIM] = y.astype(out_ref.dtype)

    c_nq, c_nk, c_nv, c_u = 0, na_w, 2 * na_w, 3 * na_w
    c_wq = c_u + pool_w
    c_wk = c_wq + wa_w
    c_wv = c_wk + kv_w
    half = wa_w // 2
    normed(chunk(c_wq, half), wag_ref[0:1, :], wq_ref, 0, Q_SCALE, use_rope)
    normed(chunk(c_wq + half, half), wag_ref[0:1, :], wq_ref, half, Q_SCALE, use_rope)
    normed(chunk(c_wk, kv_w), wag_ref[1:2, :], wk_ref, 0, 1.0, use_rope)
    normed(chunk(c_nq, na_w), nag_ref[0:1, :], nq_ref, 0, Q_SCALE, False)
    normed(chunk(c_nk, na_w), nag_ref[1:2, :], nk_ref, 0, 1.0, False)
    nv_ref[...] = chunk(c_nv, na_w).astype(BF16)
    u_ref[...] = chunk(c_u, pool_w)
    wv_ref[...] = chunk(c_wv, kv_w).astype(BF16)


def _projection(x2, mod, norm_g, w_in, layer, na_gain, wa_gain, rope, widths, *, tm=512):
    n_tok, d = x2.shape
    na_w, pool_w, wa_w, kv_w = widths
    n_in = w_in.shape[2]
    tm = _row_tile(n_tok, tm)
    tiles_per_mod = n_tok // mod.shape[0] // tm
    use_rope = rope is not None
    in_specs = [pl.BlockSpec((tm, d), lambda i: (i, 0)),
                pl.BlockSpec((None, N_MOD, d), lambda i: (i // tiles_per_mod, 0, 0)),
                pl.BlockSpec((1, d), lambda i: (0, 0)),
                pl.BlockSpec((None, d, n_in), lambda i: (layer, 0, 0)),
                pl.BlockSpec((2, HEAD_DIM), lambda i: (0, 0)),
                pl.BlockSpec((2, HEAD_DIM), lambda i: (0, 0))]
    args = [x2, mod, norm_g.reshape(1, d), w_in, na_gain, wa_gain]
    if use_rope:
        tiles_per_seq = rope[0].shape[0] // tm
        in_specs += [pl.BlockSpec((tm, HEAD_DIM), lambda i: (i % tiles_per_seq, 0))] * 2
        args += list(rope)
    out_widths = (na_w, na_w, na_w, pool_w, wa_w, kv_w, kv_w)
    out_dtypes = (BF16, BF16, BF16, F32, BF16, BF16, BF16)
    vmem = 2 * tm * d * 4 + tm * d * 2 + 2 * d * n_in * 2 + 4 * tm * n_in * 4 + (4 << 20)
    return pl.pallas_call(
        functools.partial(_proj_kernel, use_rope=use_rope, widths=widths),
        grid=(n_tok // tm,),
        in_specs=in_specs,
        out_specs=[pl.BlockSpec((tm, w), lambda i: (i, 0)) for w in out_widths],
        out_shape=[jax.ShapeDtypeStruct((n_tok, w), dt) for w, dt in zip(out_widths, out_dtypes)],
        compiler_params=_params(("parallel",), vmem),
        name="input_projection",
    )(*args)


def _softmax_weights(s, extra_logit=None):
    m = jnp.max(s, axis=-1, keepdims=True)
    if extra_logit is not None:
        m = jnp.maximum(m, extra_logit)
    p = jnp.exp2(s - m)
    denom = jnp.sum(p, axis=-1, keepdims=True)
    if extra_logit is not None:
        denom = denom + jnp.exp2(extra_logit - m)
    return p.astype(BF16), denom


def _attend(scores, values, extra_logits=None):
    extra_logits = extra_logits or [None] * len(scores)
    weights = [_softmax_weights(s, e) for s, e in zip(scores, extra_logits)]
    return [jnp.dot(p, v, preferred_element_type=F32) / denom for (p, denom), v in zip(weights, values)]


def _qk(q, k):
    return lax.dot_general(q, k, (((1,), (1,)), ((), ())), preferred_element_type=F32)


def _na_kernel(q_ref, kp_ref, kc_ref, kn_ref, vp_ref, vc_ref, vn_ref, kx_ref, vx_ref, bias_ref, o_ref):
    j, n_tiles = pl.program_id(1), pl.num_programs(1)
    blk = NA_Q_ROWS * GRID_W
    q_blocks = q_ref.shape[0] // blk
    n_heads = q_ref.shape[1] // HEAD_DIM
    heads = [slice(h * HEAD_DIM, (h + 1) * HEAD_DIM) for h in range(n_heads)]
    rows = lambda t: slice(t * blk, (t + 1) * blk)

    def window(halo_before, tile, halo_after, qb, hs):
        before = halo_before[:, hs] if qb == 0 else tile[rows(qb - 1), hs]
        after = halo_after[:, hs] if qb == q_blocks - 1 else tile[rows(qb + 1), hs]
        return [before, tile[rows(qb), hs], after]

    scores, values = [], []
    for qb in range(q_blocks):
        variant = 1
        if qb == 0:
            variant = jnp.where(j == 0, 0, variant)
        if qb == q_blocks - 1:
            variant = jnp.where(j == n_tiles - 1, 2, variant)
        for h, hs in enumerate(heads):
            k = jnp.concatenate(window(kp_ref, kc_ref, kn_ref, qb, hs) + [kx_ref[:, hs]], axis=0)
            scores.append(_qk(q_ref[rows(qb), hs], k) + bias_ref[variant, h])
            values.append(jnp.concatenate(window(vp_ref, vc_ref, vn_ref, qb, hs) + [vx_ref[:, hs]], axis=0))
    for e, out in enumerate(_attend(scores, values)):
        qb, h = divmod(e, n_heads)
        o_ref[rows(qb), heads[h]] = out.astype(o_ref.dtype)


def _na_bias_kernel(rpb_ref, o_ref, g_ref, *, n_rows):
    n_ro, n_co = 2 * NA_WIN_H - 1, 2 * NA_WIN_W - 1
    shape = (GRID_W, 2 * GRID_W)
    qc = lax.broadcasted_iota(jnp.int32, shape, 0)
    lane = lax.broadcasted_iota(jnp.int32, shape, 1)
    kc = lane & (GRID_W - 1)
    c0 = jnp.clip(qc - NA_WIN_W // 2, 0, GRID_W - NA_WIN_W)
    in_cols = (kc >= c0) & (kc < c0 + NA_WIN_W)
    col_off = kc - qc + (NA_WIN_W - 1)
    base = pl.program_id(0) * (n_ro * n_co)
    for ro in range(n_ro):
        g = jnp.zeros(shape, F32)
        for c in range(n_co):
            g = jnp.where(col_off == c, rpb_ref[base + ro * n_co + c] * LOG2E, g)
        g_ref[ro] = jnp.where(in_cols, g, NEG_INF)

    masked = jnp.full(shape, NEG_INF, F32)
    n_local = 3 * NA_Q_ROWS * GRID_W
    n_blocks = n_rows // NA_Q_ROWS
    o_ref[:, :, n_local:] = jnp.zeros(o_ref.shape[:2] + (o_ref.shape[2] - n_local,), F32)
    for variant, jb in enumerate((0, 1, n_blocks - 1)):
        for i in range(NA_Q_ROWS):
            r = jb * NA_Q_ROWS + i
            r0 = min(max(r - NA_WIN_H // 2, 0), n_rows - NA_WIN_H)
            halves = []
            for kr in range(3 * NA_Q_ROWS):
                kr_abs = (jb - 1) * NA_Q_ROWS + kr
                in_rows = 0 <= kr_abs < n_rows and r0 <= kr_abs < r0 + NA_WIN_H
                halves.append(g_ref[kr_abs - r + NA_WIN_H - 1] if in_rows else masked)
            for m in range(len(halves) // 2):
                o_ref[variant, i * GRID_W:(i + 1) * GRID_W, m * 2 * GRID_W:(m + 1) * 2 * GRID_W] = jnp.where(
                    lane < GRID_W, halves[2 * m], halves[2 * m + 1])


def _na_bias_table(rpb, n_rows, n_ctx):
    n_heads = rpb.shape[0]
    assert n_rows % NA_Q_ROWS == 0 and n_rows // NA_Q_ROWS >= 3 and (3 * NA_Q_ROWS) % 2 == 0
    shape = (3, n_heads, NA_Q_ROWS * GRID_W, 3 * NA_Q_ROWS * GRID_W + n_ctx)
    return pl.pallas_call(
        functools.partial(_na_bias_kernel, n_rows=n_rows),
        grid=(n_heads,),
        in_specs=[pl.BlockSpec(memory_space=pltpu.SMEM)],
        out_specs=pl.BlockSpec((3, None) + shape[2:], lambda h: (0, h, 0, 0)),
        out_shape=jax.ShapeDtypeStruct(shape, F32),
        scratch_shapes=[pltpu.VMEM((2 * NA_WIN_H - 1, GRID_W, 2 * GRID_W), F32)],
        compiler_params=_params(("parallel",), 24 << 20),
        name="na_bias_table",
    )(rpb.astype(F32).reshape(-1))


def _neighbourhood_attention(q, k, v, kx, vx, bias, *, q_blocks=NA_Q_BLOCKS):
    b, s, w = q.shape
    blk = NA_Q_ROWS * GRID_W
    n_halo = s // blk
    rows = q_blocks * blk
    assert s % rows == 0
    cx = kx.shape[1]
    cur = lambda bb, j: (bb, j, 0)
    prev = lambda bb, j: (bb, jnp.maximum(j * q_blocks - 1, 0), 0)
    nxt = lambda bb, j: (bb, jnp.minimum((j + 1) * q_blocks, n_halo - 1), 0)
    halo = lambda im: pl.BlockSpec((None, blk, w), im)
    tile = pl.BlockSpec((None, rows, w), cur)
    ctx = pl.BlockSpec((None, cx, w), lambda bb, j: (bb, 0, 0))
    vmem = 2 * bias.size * 4 + 8 * q_blocks * blk * bias.shape[3] * 4 + (8 << 20)
    return pl.pallas_call(
        _na_kernel,
        grid=(b, s // rows),
        in_specs=[tile, halo(prev), tile, halo(nxt), halo(prev), tile, halo(nxt), ctx, ctx,
                  pl.BlockSpec(bias.shape, lambda bb, j: (0, 0, 0, 0))],
        out_specs=tile,
        out_shape=jax.ShapeDtypeStruct((b, s, w), BF16),
        compiler_params=_params(("parallel", "parallel"), vmem),
        name="neighbourhood_attention",
    )(q, k, k, k, v, v, v, kx, vx, bias)


def _wa_band_bias(n_ctx):
    blk = WA_BLOCK
    p = np.arange(blk)[:, None]
    k = np.arange(3 * blk + n_ctx)[None, :]
    in_band = (k >= p) & (k <= p + 2 * blk)
    is_prev, is_next, is_ctx = k < blk, (k >= 2 * blk) & (k < 3 * blk), k >= 3 * blk
    tables = []
    for has_prev, has_next in ((False, True), (True, True), (True, False)):
        valid = is_ctx | (in_band & (has_prev | ~is_prev) & (has_next | ~is_next))
        tables.append(np.where(valid, 0.0, NEG_INF).astype(np.float32))
    return np.stack(tables)


def _wa_kernel(sink_ref, band_ref, q_ref, kp_ref, kc_ref, kn_ref, vp_ref, vc_ref, vn_ref, kx_ref, vx_ref,
               o_ref, *, group):
    j, n_tiles = pl.program_id(1), pl.num_programs(1)
    blk = WA_BLOCK
    q_blocks = q_ref.shape[0] // blk
    n_kv = kc_ref.shape[1] // HEAD_DIM

    def window(halo_before, tile, halo_after, qb, hs):
        rows = lambda t: slice(t * blk, (t + 1) * blk)
        before = halo_before[:, hs] if qb == 0 else tile[rows(qb - 1), hs]
        after = halo_after[:, hs] if qb == q_blocks - 1 else tile[rows(qb + 1), hs]
        return [before, tile[rows(qb), hs], after]

    scores, values, sinks = [], [], []
    for qb in range(q_blocks):
        variant = 1
        if qb == 0:
            variant = jnp.where(j == 0, 0, variant)
        if qb == q_blocks - 1:
            variant = jnp.where(j == n_tiles - 1, 2, variant)
        band = jnp.concatenate([band_ref[variant]] * group, axis=0)
        for kv in range(n_kv):
            hs = slice(kv * HEAD_DIM, (kv + 1) * HEAD_DIM)
            heads = [kv * group + g for g in range(group)]
            q = jnp.concatenate([q_ref[qb * blk:(qb + 1) * blk, h * HEAD_DIM:(h + 1) * HEAD_DIM] for h in heads],
                                axis=0)
            k = jnp.concatenate(window(kp_ref, kc_ref, kn_ref, qb, hs) + [kx_ref[:, hs]], axis=0)
            scores.append(_qk(q, k) + band)
            values.append(jnp.concatenate(window(vp_ref, vc_ref, vn_ref, qb, hs) + [vx_ref[:, hs]], axis=0))
            sinks.append(jnp.concatenate([jnp.full((blk, 1), sink_ref[h] * LOG2E, F32) for h in heads], axis=0))
    for e, out in enumerate(_attend(scores, values, sinks)):
        qb, kv = divmod(e, n_kv)
        for g in range(group):
            h = kv * group + g
            o_ref[qb * blk:(qb + 1) * blk, h * HEAD_DIM:(h + 1) * HEAD_DIM] = (
                out[g * blk:(g + 1) * blk, :].astype(o_ref.dtype))


def _window_attention(q, k, v, kx, vx, sink, *, q_blocks=WA_Q_BLOCKS):
    b, s, wq = q.shape
    wk = k.shape[2]
    group = wq // wk
    rows = q_blocks * WA_BLOCK
    n_tiles = s // rows
    n_halo = s // WA_BLOCK
    cx = kx.shape[1]
    assert s % rows == 0 and n_halo >= 2
    cur = lambda bb, j: (bb, j, 0)
    prev = lambda bb, j: (bb, jnp.maximum(j * q_blocks - 1, 0), 0)
    nxt = lambda bb, j: (bb, jnp.minimum((j + 1) * q_blocks, n_halo - 1), 0)
    halo = lambda im: pl.BlockSpec((None, WA_BLOCK, wk), im)
    tile = pl.BlockSpec((None, rows, wk), cur)
    ctx = pl.BlockSpec((None, cx, wk), lambda bb, j: (bb, 0, 0))
    band = jnp.asarray(_wa_band_bias(cx))
    vmem = 16 * q_blocks * group * WA_BLOCK * band.shape[2] * 4 + (8 << 20)
    return pl.pallas_call(
        functools.partial(_wa_kernel, group=group),
        grid=(b, n_tiles),
        in_specs=[pl.BlockSpec(memory_space=pltpu.SMEM),
                  pl.BlockSpec(band.shape, lambda bb, j: (0, 0, 0)),
                  pl.BlockSpec((None, rows, wq), cur),
                  halo(prev), tile, halo(nxt), halo(prev), tile, halo(nxt), ctx, ctx],
        out_specs=pl.BlockSpec((None, rows, wq), cur),
        out_shape=jax.ShapeDtypeStruct((b, s, wq), BF16),
        compiler_params=_params(("parallel", "parallel"), vmem),
        name="window_attention",
    )(sink, band, q, k, k, k, v, v, v, kx, vx)


def _ctx_attn_kernel(*refs, group, use_sink):
    if use_sink:
        sink_ref, q_ref, k_ref, v_ref, o_ref = refs
    else:
        q_ref, k_ref, v_ref, o_ref = refs
    c = q_ref.shape[0]
    for kv in range(k_ref.shape[1] // HEAD_DIM):
        hs = slice(kv * HEAD_DIM, (kv + 1) * HEAD_DIM)
        heads = [kv * group + g for g in range(group)]
        q = jnp.concatenate([q_ref[:, h * HEAD_DIM:(h + 1) * HEAD_DIM] for h in heads], axis=0)
        sink = None
        if use_sink:
            sink = jnp.concatenate([jnp.full((c, 1), sink_ref[h] * LOG2E, F32) for h in heads], axis=0)
        out, = _attend([_qk(q, k_ref[:, hs])], [v_ref[:, hs]], [sink])
        for g, h in enumerate(heads):
            o_ref[:, h * HEAD_DIM:(h + 1) * HEAD_DIM] = out[g * c:(g + 1) * c, :].astype(o_ref.dtype)


def _context_attention(q, k, v, sink):
    b, c, wq = q.shape
    wk = k.shape[2]
    use_sink = sink is not None
    spec = lambda w: pl.BlockSpec((None, c, w), lambda bb: (bb, 0, 0))
    in_specs = [spec(wq), spec(wk), spec(wk)]
    args = [q, k, v]
    if use_sink:
        in_specs = [pl.BlockSpec(memory_space=pltpu.SMEM)] + in_specs
        args = [sink] + args
    return pl.pallas_call(
        functools.partial(_ctx_attn_kernel, group=wq // wk, use_sink=use_sink),
        grid=(b,),
        in_specs=in_specs,
        out_specs=spec(wq),
        out_shape=jax.ShapeDtypeStruct((b, c, wq), BF16),
        compiler_params=_params(("parallel",), 24 << 20),
        name="context_attention",
    )(*args)


def _out_kernel(x_ref, mod_ref, a_ref, u_ref, up_ref, un_ref, c_ref, pw_ref, ps_ref, wo_ref, o_ref, ext_ref,
                *, seq_len):
    tm = x_ref.shape[0]
    tiles_per_seq = seq_len // tm
    t_in_seq = pl.program_id(0) % tiles_per_seq
    halo = POOL_HALO
    has_prev = (t_in_seq > 0).astype(F32)
    has_next = (t_in_seq < tiles_per_seq - 1).astype(F32)
    ext_ref[0:halo, :] = up_ref[...] * has_prev
    ext_ref[halo:halo + tm, :] = u_ref[...]
    ext_ref[halo + tm:, :] = un_ref[...] * has_next

    pos = t_in_seq * tm + lax.broadcasted_iota(jnp.int32, (tm, 1), 0)
    parts = []
    for g, win in enumerate(POOL_WINDOWS):
        ls = slice(g * HEAD_DIM, (g + 1) * HEAD_DIM)
        total = functools.reduce(jnp.add, [ext_ref[halo + d:halo + d + tm, ls]
                                           for d in range(-(win // 2), win - win // 2)])
        lo = jnp.maximum(pos - win // 2, 0)
        hi = jnp.minimum(pos + (win - win // 2 - 1), seq_len - 1)
        delta = total / (hi - lo + 1).astype(F32) - u_ref[:, ls]
        y = jnp.dot(delta.astype(BF16), pw_ref[g], preferred_element_type=F32) * ps_ref[:, ls]
        parts.append(y.astype(BF16))
    na_w = a_ref.shape[1]
    pool_w = len(POOL_WINDOWS) * HEAD_DIM
    mixed = jnp.dot(a_ref[...], wo_ref[0:na_w, :], preferred_element_type=F32)
    mixed += jnp.dot(c_ref[...], wo_ref[na_w + pool_w:, :], preferred_element_type=F32)
    mixed += jnp.dot(jnp.concatenate(parts, axis=1), wo_ref[na_w:na_w + pool_w, :], preferred_element_type=F32)
    o_ref[...] = x_ref[...] + mod_ref[5:6, :] * mixed


def _mix_out(x2, mod, a2, u2, c2, pool_w, pool_layer, pool_scale, w_out, seq_len, *, tm=512):
    n_tok, d = x2.shape
    tm = _row_tile(seq_len, tm)
    tiles_per_mod = n_tok // mod.shape[0] // tm
    n_halo = n_tok // POOL_HALO
    hpt = tm // POOL_HALO
    row = lambda w: pl.BlockSpec((tm, w), lambda i: (i, 0))
    vmem = 4 * tm * d * 4 + 2 * w_out[0].size * 2 + 4 * tm * d * 4 + (4 << 20)
    return pl.pallas_call(
        functools.partial(_out_kernel, seq_len=seq_len),
        grid=(n_tok // tm,),
        in_specs=[row(d),
                  pl.BlockSpec((None, N_MOD, d), lambda i: (i // tiles_per_mod, 0, 0)),
                  row(a2.shape[1]), row(u2.shape[1]),
                  pl.BlockSpec((POOL_HALO, u2.shape[1]), lambda i: (jnp.maximum(i * hpt - 1, 0), 0)),
                  pl.BlockSpec((POOL_HALO, u2.shape[1]), lambda i: (jnp.minimum((i + 1) * hpt, n_halo - 1), 0)),
                  row(c2.shape[1]),
                  pl.BlockSpec((None,) + pool_w.shape[1:], lambda i: (pool_layer, 0, 0, 0)),
                  pl.BlockSpec((1, u2.shape[1]), lambda i: (0, 0)),
                  pl.BlockSpec((None,) + w_out.shape[1:], lambda i: (0, 0, 0))],
        out_specs=row(d),
        out_shape=jax.ShapeDtypeStruct((n_tok, d), F32),
        scratch_shapes=[pltpu.VMEM((tm + 2 * POOL_HALO, u2.shape[1]), F32)],
        compiler_params=_params(("parallel",), vmem),
        name="pool_mix_out",
    )(x2, mod, a2, u2, u2, u2, c2, pool_w, pool_scale.reshape(1, -1), w_out)


def _rope_tables(seq_len):
    n_rows = seq_len // GRID_W
    axis_dim = HEAD_DIM // 2
    inv_freq = ROPE_BASE ** (-jnp.arange(0, axis_dim, 2, dtype=F32) / axis_dim)
    ang_r = jnp.arange(n_rows, dtype=F32)[:, None] * inv_freq[None, :]
    ang_c = jnp.arange(GRID_W, dtype=F32)[:, None] * inv_freq[None, :]
    by_row = lambda t: jnp.broadcast_to(t[:, None, :], (n_rows, GRID_W, t.shape[-1]))
    by_col = lambda t: jnp.broadcast_to(t[None, :, :], (n_rows, GRID_W, t.shape[-1]))
    cos_r, sin_r = by_row(jnp.cos(ang_r)), by_row(jnp.sin(ang_r))
    cos_c, sin_c = by_col(jnp.cos(ang_c)), by_col(jnp.sin(ang_c))
    cos = jnp.concatenate([cos_r, cos_r, cos_c, cos_c], axis=-1).reshape(seq_len, HEAD_DIM)
    sin = jnp.concatenate([-sin_r, sin_r, -sin_c, sin_c], axis=-1).reshape(seq_len, HEAD_DIM)
    return cos, sin


def kernel(x, c, ctx, c_ctx, w_mod, b_mod, norm_w, ffn1_wi, ffn1_wo, ffn2_wi, ffn2_wo, w_in, w_out,
           na_qk_gain, na_rpb, pool_w, pool_scale, wa_qk_gain, wa_sink):
    b, s, d = x.shape
    cx = ctx.shape[1]
    depth = w_mod.shape[0]
    na_w = na_rpb.shape[1] * HEAD_DIM
    pw = pool_scale.shape[1]
    kv_w = (w_in.shape[2] - 3 * na_w - pw - (d - na_w - pw)) // 2
    widths = (na_w, pw, d - na_w - pw, kv_w)

    n_cond = b + 1
    cvecs = jnp.concatenate([c, c_ctx[None, :], jnp.zeros((-n_cond % SUBLANES, d), F32)], axis=0)
    mods = _modulation(cvecs, w_mod, b_mod).reshape(depth, cvecs.shape[0], N_MOD, d)
    rope = _rope_tables(s)
    wi1, wo1 = ffn1_wi[:1].astype(BF16), ffn1_wo[:1].astype(BF16)
    pwb = pool_w.astype(BF16)

    x2 = x.reshape(b * s, d)
    c2 = ctx.reshape(b * cx, d)
    seq = lambda t: t.reshape(b, s, -1)
    cseq = lambda t: t.reshape(b, cx, -1)
    for l in range(depth):
        last = l == depth - 1
        mx, mc = mods[l, :b], mods[l, b:b + 1]

        x2, (win, wout, wi2, wo2) = _ffn(x2, mx, 0, norm_w[l, 0], wi1, wo1, 0,
                                         cast=[(w_in, l), (w_out, l), (ffn2_wi, l), (ffn2_wo, l)])
        c2, _ = _ffn(c2, mc, 0, norm_w[l, 0], wi1, wo1, 0)

        nq, nk, nv, u, wq, wk, wv = _projection(x2, mx, norm_w[l, 1], win, 0, na_qk_gain[l], wa_qk_gain[l],
                                                rope, widths)
        cnq, cnk, cnv, cu, cwq, cwk, cwv = _projection(c2, mc, norm_w[l, 1], win, 0, na_qk_gain[l],
                                                       wa_qk_gain[l], None, widths)
        a_out = _neighbourhood_attention(seq(nq), seq(nk), seq(nv), cseq(cnk), cseq(cnv),
                                         _na_bias_table(na_rpb[l], s // GRID_W, cx))
        c_out = _window_attention(seq(wq), seq(wk), seq(wv), cseq(cwk), cseq(cwv), wa_sink[l])
        x2 = _mix_out(x2, mx, a_out.reshape(b * s, -1), u, c_out.reshape(b * s, -1), pwb, l, pool_scale[l],
                      wout, s)
        if not last:
            ca = _context_attention(cseq(cnq), cseq(cnk), cseq(cnv), None)
            cc = _context_attention(cseq(cwq), cseq(cwk), cseq(cwv), wa_sink[l])
            c2 = _mix_out(c2, mc, ca.reshape(b * cx, -1), cu, cc.reshape(b * cx, -1), pwb, l, pool_scale[l],
                          wout, cx)
            c2, _ = _ffn(c2, mc, 6, norm_w[l, 2], wi2, wo2, 0)
        x2, nxt = _ffn(x2, mx, 6, norm_w[l, 2], wi2, wo2, 0,
                       cast=[] if last else [(ffn1_wi, l + 1), (ffn1_wo, l + 1)])
        if not last:
            wi1, wo1 = nxt
    return x2.reshape(b, s, d)
```

```python
import functools

import numpy as np
import jax
import jax.numpy as jnp
from jax import lax
from jax.experimental import pallas as pl
from jax.experimental.pallas import tpu as pltpu

HEAD_DIM = 128
GRID_W = 64
NA_WIN_H = 8
NA_WIN_W = 16
POOL_WINDOWS = (2, 4, 8, 16)
WA_BLOCK = 128
WA_Q_BLOCKS = 8
N_MOD = 9
MACARON_WEIGHT = 0.5
ROPE_BASE = 10000.0
RMS_EPS = 1e-6
NEG_INF = -1e30
LOG2E = 1.4426950408889634
Q_SCALE = HEAD_DIM ** -0.5 * LOG2E

V7X_VMEM_BYTES = 64 * 1024 * 1024
V7X_VMEM_BUDGET = 56 * 1024 * 1024
SUBLANES = 8
NA_Q_ROWS = NA_WIN_H // 2
NA_Q_BLOCKS = 4
POOL_HALO = 16

BF16 = jnp.bfloat16
F32 = jnp.float32


def _params(semantics, vmem_bytes):
    return pltpu.CompilerParams(dimension_semantics=semantics,
                                vmem_limit_bytes=int(min(max(vmem_bytes, 16 << 20), V7X_VMEM_BUDGET)))


def _row_tile(n_rows, want):
    t = min(want, n_rows)
    assert n_rows % t == 0, (n_rows, t)
    return t


def _mod_norm(x, mod_ref, g_ref, first):
    y = x * lax.rsqrt(jnp.mean(x * x, axis=-1, keepdims=True) + RMS_EPS) * g_ref[...]
    return y * (1.0 + mod_ref[first + 1:first + 2, :]) + mod_ref[first:first + 1, :]


def _mod_kernel(c_ref, w_ref, b_ref, o_ref):
    c = c_ref[...]
    s = (c * jax.nn.sigmoid(c)).astype(BF16)
    o_ref[...] = jnp.dot(s, w_ref[...].astype(BF16), preferred_element_type=F32) + b_ref[...]


def _modulation(cvecs, w_mod, b_mod):
    n_layers, d, nd = w_mod.shape
    rows = cvecs.shape[0]
    tn = _row_tile(nd, 2048)
    return pl.pallas_call(
        _mod_kernel,
        grid=(n_layers, nd // tn),
        in_specs=[pl.BlockSpec((rows, d), lambda l, n: (0, 0)),
                  pl.BlockSpec((None, d, tn), lambda l, n: (l, 0, n)),
                  pl.BlockSpec((None, 1, tn), lambda l, n: (l, 0, n))],
        out_specs=pl.BlockSpec((None, rows, tn), lambda l, n: (l, 0, n)),
        out_shape=jax.ShapeDtypeStruct((n_layers, rows, nd), F32),
        compiler_params=_params(("parallel", "parallel"), 2 * d * tn * 4 + (8 << 20)),
        name="adaln_modulation",
    )(cvecs, w_mod, b_mod.reshape(n_layers, 1, nd))


def _ffn_kernel(*refs, first, tiles_per_mod, n_chunks, n_cast):
    x_hbm, mod_ref, g_ref, wa_ref, wb_ref, wo_ref = refs[:6]
    cast_in = refs[6:6 + n_cast]
    o_hbm = refs[6 + n_cast]
    cast_out = refs[7 + n_cast:7 + 2 * n_cast]
    xa_ref, xb_ref, ha_ref, hb_ref, in_sem, out_sem = refs[7 + 2 * n_cast:]
    i, f = pl.program_id(0), pl.program_id(1)
    n_tiles, nf = pl.num_programs(0), pl.num_programs(1)
    tm = xa_ref.shape[0]
    rows = tm // n_chunks
    head = tm - rows

    def x_in(tile, dst, part, slot):
        r0, n = ((0, head), (head, rows))[part]
        return pltpu.make_async_copy(x_hbm.at[pl.ds(tile * tm + r0, n), :], dst.at[pl.ds(r0, n), :],
                                     in_sem.at[slot, part])

    def x_out(tile, src, slot):
        return pltpu.make_async_copy(src, o_hbm.at[pl.ds(tile * tm, tm), :], out_sem.at[slot])

    def normed(x_rows, batch):
        return _mod_norm(x_rows, mod_ref.at[batch], g_ref, first).astype(BF16)

    def tile_step(slot, x_cur, x_nxt, h_cur, h_nxt):
        other = 1 - slot
        has_next = i + 1 < n_tiles

        if slot == 0:
            @pl.when((i == 0) & (f == 0))
            def _():
                for part in (0, 1):
                    x_in(0, x_cur, part, slot).start()
                x_nxt[...] = jnp.zeros(x_nxt.shape, x_nxt.dtype)
                for part in (0, 1):
                    x_in(0, x_cur, part, slot).wait()
                h_cur[...] = normed(x_cur[...], 0)

        @pl.when((f == 1) & (i > 0))
        def _():
            x_out(i - 1, x_nxt, other).wait()

        @pl.when((f == 1) & has_next)
        def _():
            x_in(i + 1, x_nxt, 0, other).start()

        @pl.when((f == 2) & has_next)
        def _():
            x_in(i + 1, x_nxt, 0, other).wait()
            x_in(i + 1, x_nxt, 1, other).start()

        @pl.when((f == 3) & has_next)
        def _():
            x_in(i + 1, x_nxt, 1, other).wait()

        for src, dst in zip(cast_in, cast_out):
            dst[...] = src[...].astype(dst.dtype)

        chunk = jnp.where((f >= 2) & (f < 2 + n_chunks), f - 2, n_chunks - 1)
        r0 = pl.multiple_of(chunk * rows, rows)
        next_batch = jnp.minimum(i + 1, n_tiles - 1) // tiles_per_mod
        h_nxt[pl.ds(r0, rows), :] = normed(x_nxt[pl.ds(r0, rows), :], next_batch)

        gate = MACARON_WEIGHT * mod_ref[i // tiles_per_mod, first + 2:first + 3, :]
        h = h_cur[...]
        a = jnp.dot(h, wa_ref[...], preferred_element_type=F32)
        b = jnp.dot(h, wb_ref[...], preferred_element_type=F32)
        act = (a * jax.nn.sigmoid(a) * b).astype(BF16)
        x_cur[...] += gate * jnp.dot(act, wo_ref[...], preferred_element_type=F32)

        @pl.when(f == nf - 1)
        def _():
            x_out(i, x_cur, slot).start()

        @pl.when((f == nf - 1) & (i == n_tiles - 1))
        def _():
            x_out(i, x_cur, slot).wait()

    @pl.when(i % 2 == 0)
    def _():
        tile_step(0, xa_ref, xb_ref, ha_ref, hb_ref)

    @pl.when(i % 2 == 1)
    def _():
        tile_step(1, xb_ref, xa_ref, hb_ref, ha_ref)


def _ffn(x2, mod, first, norm_g, wi, wo, layer, cast=(), *, tm=1024, tf=512):
    n_tok, d = x2.shape
    d_ff = wo.shape[1]
    tm = _row_tile(n_tok // mod.shape[0], tm)
    tf = _row_tile(d_ff, tf)
    tiles_per_mod = n_tok // mod.shape[0] // tm
    nf = d_ff // tf
    n_steps = (n_tok // tm) * nf
    n_chunks = 1 << ((nf - 2).bit_length() - 1)
    assert nf >= 4 and tm % (n_chunks * 2 * SUBLANES) == 0, (nf, tm)

    cast_in_specs, cast_out_specs, cast_shapes = [], [], []
    for arr, lyr in cast:
        _, rows, cols = arr.shape
        block = next(r for r in range(2 * SUBLANES, rows + 1, 2 * SUBLANES)
                     if rows % r == 0 and rows // r <= n_steps)
        last = rows // block - 1
        cast_in_specs.append(pl.BlockSpec(
            (None, block, cols), lambda i, f, last=last, lyr=lyr: (lyr, jnp.minimum(i * nf + f, last), 0)))
        cast_out_specs.append(pl.BlockSpec(
            (None, block, cols), lambda i, f, last=last: (0, jnp.minimum(i * nf + f, last), 0)))
        cast_shapes.append(jax.ShapeDtypeStruct((1, rows, cols), BF16))

    vmem = 2 * tm * d * 4 + 2 * tm * d * 2 + 2 * 3 * d * tf * 2 + 8 * tm * tf * 4 + (6 << 20)
    out, *cast_out = pl.pallas_call(
        functools.partial(_ffn_kernel, first=first, tiles_per_mod=tiles_per_mod, n_chunks=n_chunks,
                          n_cast=len(cast)),
        grid=(n_tok // tm, nf),
        in_specs=[pl.BlockSpec(memory_space=pl.ANY),
                  pl.BlockSpec(mod.shape, lambda i, f: (0, 0, 0)),
                  pl.BlockSpec((1, d), lambda i, f: (0, 0)),
                  pl.BlockSpec((None, d, tf), lambda i, f: (layer, 0, f)),
                  pl.BlockSpec((None, d, tf), lambda i, f: (layer, 0, nf + f)),
                  pl.BlockSpec((None, tf, d), lambda i, f: (layer, f, 0))] + cast_in_specs,
        out_specs=[pl.BlockSpec(memory_space=pl.ANY)] + cast_out_specs,
        out_shape=[jax.ShapeDtypeStruct((n_tok, d), F32)] + cast_shapes,
        scratch_shapes=[pltpu.VMEM((tm, d), F32), pltpu.VMEM((tm, d), F32),
                        pltpu.VMEM((tm, d), BF16), pltpu.VMEM((tm, d), BF16),
                        pltpu.SemaphoreType.DMA((2, 2)), pltpu.SemaphoreType.DMA((2,))],
        compiler_params=_params(("arbitrary", "arbitrary"), vmem),
        name="macaron_ffn",
    )(x2, mod, norm_g.reshape(1, d), wi, wi, wo, *[arr for arr, _ in cast])
    return out, cast_out


def _head_norm(p, gain):
    return p * lax.rsqrt(jnp.mean(p * p, axis=-1, keepdims=True) + RMS_EPS) * gain


def _rotate(y, cos, sin_signed):
    lane = lax.broadcasted_iota(jnp.int32, y.shape, 1)
    partner = jnp.where((lane & 32) == 0, pltpu.roll(y, HEAD_DIM - 32, 1), pltpu.roll(y, 32, 1))
    return y * cos + partner * sin_signed


def _proj_kernel(*refs, use_rope, widths):
    if use_rope:
        x_ref, mod_ref, g_ref, w_ref, nag_ref, wag_ref, cos_ref, sin_ref = refs[:8]
        outs = refs[8:]
    else:
        x_ref, mod_ref, g_ref, w_ref, nag_ref, wag_ref = refs[:6]
        outs = refs[6:]
    nq_ref, nk_ref, nv_ref, u_ref, wq_ref, wk_ref, wv_ref = outs
    na_w, pool_w, wa_w, kv_w = widths
    h = _mod_norm(x_ref[...], mod_ref, g_ref, 3).astype(BF16)

    def chunk(col0, width):
        return jnp.dot(h, w_ref[:, col0:col0 + width], preferred_element_type=F32)

    def normed(p, gain, out_ref, out_col0, scale, rope):
        for j in range(p.shape[1] // HEAD_DIM):
            y = _head_norm(p[:, j * HEAD_DIM:(j + 1) * HEAD_DIM], gain)
            if rope:
                y = _rotate(y, cos_ref[...], sin_ref[...])
            if scale != 1.0:
                y = y * scale
            c0 = out_col0 + j * HEAD_DIM
            out_ref[:, c0:c0 + HEAD_DIM] = y.astype(out_ref.dtype)

    c_nq, c_nk, c_nv, c_u = 0, na_w, 2 * na_w, 3 * na_w
    c_wq = c_u + pool_w
    c_wk = c_wq + wa_w
    c_wv = c_wk + kv_w
    half = wa_w // 2
    normed(chunk(c_wq, half), wag_ref[0:1, :], wq_ref, 0, Q_SCALE, use_rope)
    normed(chunk(c_wq + half, half), wag_ref[0:1, :], wq_ref, half, Q_SCALE, use_rope)
    normed(chunk(c_wk, kv_w), wag_ref[1:2, :], wk_ref, 0, 1.0, use_rope)
    normed(chunk(c_nq, na_w), nag_ref[0:1, :], nq_ref, 0, Q_SCALE, False)
    normed(chunk(c_nk, na_w), nag_ref[1:2, :], nk_ref, 0, 1.0, False)
    nv_ref[...] = chunk(c_nv, na_w).astype(BF16)
    u_ref[...] = chunk(c_u, pool_w)
    wv_ref[...] = chunk(c_wv, kv_w).astype(BF16)


def _projection(x2, mod, norm_g, w_in, layer, na_gain, wa_gain, rope, widths, *, tm=512):
    n_tok, d = x2.shape
    na_w, pool_w, wa_w, kv_w = widths
    n_in = w_in.shape[2]
    tm = _row_tile(n_tok, tm)
    tiles_per_mod = n_tok // mod.shape[0] // tm
    use_rope = rope is not None
    in_specs = [pl.BlockSpec((tm, d), lambda i: (i, 0)),
                pl.BlockSpec((None, N_MOD, d), lambda i: (i // tiles_per_mod, 0, 0)),
                pl.BlockSpec((1, d), lambda i: (0, 0)),
                pl.BlockSpec((None, d, n_in), lambda i: (layer, 0, 0)),
                pl.BlockSpec((2, HEAD_DIM), lambda i: (0, 0)),
                pl.BlockSpec((2, HEAD_DIM), lambda i: (0, 0))]
    args = [x2, mod, norm_g.reshape(1, d), w_in, na_gain, wa_gain]
    if use_rope:
        tiles_per_seq = rope[0].shape[0] // tm
        in_specs += [pl.BlockSpec((tm, HEAD_DIM), lambda i: (i % tiles_per_seq, 0))] * 2
        args += list(rope)
    out_widths = (na_w, na_w, na_w, pool_w, wa_w, kv_w, kv_w)
    out_dtypes = (BF16, BF16, BF16, F32, BF16, BF16, BF16)
    vmem = 2 * tm * d * 4 + tm * d * 2 + 2 * d * n_in * 2 + 4 * tm * n_in * 4 + (4 << 20)
    return pl.pallas_call(
        functools.partial(_proj_kernel, use_rope=use_rope, widths=widths),
        grid=(n_tok // tm,),
        in_specs=in_specs,
        out_specs=[pl.BlockSpec((tm, w), lambda i: (i, 0)) for w in out_widths],
        out_shape=[jax.ShapeDtypeStruct((n_tok, w), dt) for w, dt in zip(out_widths, out_dtypes)],
        compiler_params=_params(("parallel",), vmem),
        name="input_projection",
    )(*args)


def _softmax_weights(s, extra_logit=None):
    m = jnp.max(s, axis=-1, keepdims=True)
    if extra_logit is not None:
        m = jnp.maximum(m, extra_logit)
    p = jnp.exp2(s - m)
    denom = jnp.sum(p, axis=-1, keepdims=True)
    if extra_logit is not None:
        denom = denom + jnp.exp2(extra_logit - m)
    return p.astype(BF16), denom


def _attend(scores, values, extra_logits=None):
    extra_logits = extra_logits or [None] * len(scores)
    weights = [_softmax_weights(s, e) for s, e in zip(scores, extra_logits)]
    return [jnp.dot(p, v, preferred_element_type=F32) / denom for (p, denom), v in zip(weights, values)]


def _qk(q, k):
    return lax.dot_general(q, k, (((1,), (1,)), ((), ())), preferred_element_type=F32)


def _na_kernel(q_ref, kp_ref, kc_ref, kn_ref, vp_ref, vc_ref, vn_ref, kx_ref, vx_ref, bias_ref, o_ref):
    j, n_tiles = pl.program_id(1), pl.num_programs(1)
    blk = NA_Q_ROWS * GRID_W
    q_blocks = q_ref.shape[0] // blk
    n_heads = q_ref.shape[1] // HEAD_DIM
    heads = [slice(h * HEAD_DIM, (h + 1) * HEAD_DIM) for h in range(n_heads)]
    rows = lambda t: slice(t * blk, (t + 1) * blk)

    def window(halo_before, tile, halo_after, qb, hs):
        before = halo_before[:, hs] if qb == 0 else tile[rows(qb - 1), hs]
        after = halo_after[:, hs] if qb == q_blocks - 1 else tile[rows(qb + 1), hs]
        return [before, tile[rows(qb), hs], after]

    scores, values = [], []
    for qb in range(q_blocks):
        variant = 1
        if qb == 0:
            variant = jnp.where(j == 0, 0, variant)
        if qb == q_blocks - 1:
            variant = jnp.where(j == n_tiles - 1, 2, variant)
        for h, hs in enumerate(heads):
            k = jnp.concatenate(window(kp_ref, kc_ref, kn_ref, qb, hs) + [kx_ref[:, hs]], axis=0)
            scores.append(_qk(q_ref[rows(qb), hs], k) + bias_ref[variant, h])
            values.append(jnp.concatenate(window(vp_ref, vc_ref, vn_ref, qb, hs) + [vx_ref[:, hs]], axis=0))
    for e, out in enumerate(_attend(scores, values)):
        qb, h = divmod(e, n_heads)
        o_ref[rows(qb), heads[h]] = out.astype(o_ref.dtype)


def _na_bias_kernel(rpb_ref, o_ref, g_ref, *, n_rows):
    n_ro, n_co = 2 * NA_WIN_H - 1, 2 * NA_WIN_W - 1
    shape = (GRID_W, 2 * GRID_W)
    qc = lax.broadcasted_iota(jnp.int32, shape, 0)
    lane = lax.broadcasted_iota(jnp.int32, shape, 1)
    kc = lane & (GRID_W - 1)
    c0 = jnp.clip(qc - NA_WIN_W // 2, 0, GRID_W - NA_WIN_W)
    in_cols = (kc >= c0) & (kc < c0 + NA_WIN_W)
    col_off = kc - qc + (NA_WIN_W - 1)
    base = pl.program_id(0) * (n_ro * n_co)
    for ro in range(n_ro):
        g = jnp.zeros(shape, F32)
        for c in range(n_co):
            g = jnp.where(col_off == c, rpb_ref[base + ro * n_co + c] * LOG2E, g)
        g_ref[ro] = jnp.where(in_cols, g, NEG_INF)

    masked = jnp.full(shape, NEG_INF, F32)
    n_local = 3 * NA_Q_ROWS * GRID_W
    n_blocks = n_rows // NA_Q_ROWS
    o_ref[:, :, n_local:] = jnp.zeros(o_ref.shape[:2] + (o_ref.shape[2] - n_local,), F32)
    for variant, jb in enumerate((0, 1, n_blocks - 1)):
        for i in range(NA_Q_ROWS):
            r = jb * NA_Q_ROWS + i
            r0 = min(max(r - NA_WIN_H // 2, 0), n_rows - NA_WIN_H)
            halves = []
            for kr in range(3 * NA_Q_ROWS):
                kr_abs = (jb - 1) * NA_Q_ROWS + kr
                in_rows = 0 <= kr_abs < n_rows and r0 <= kr_abs < r0 + NA_WIN_H
                halves.append(g_ref[kr_abs - r + NA_WIN_H - 1] if in_rows else masked)
            for m in range(len(halves) // 2):
                o_ref[variant, i * GRID_W:(i + 1) * GRID_W, m * 2 * GRID_W:(m + 1) * 2 * GRID_W] = jnp.where(
                    lane < GRID_W, halves[2 * m], halves[2 * m + 1])


def _na_bias_table(rpb, n_rows, n_ctx):
    n_heads = rpb.shape[0]
    assert n_rows % NA_Q_ROWS == 0 and n_rows // NA_Q_ROWS >= 3 and (3 * NA_Q_ROWS) % 2 == 0
    shape = (3, n_heads, NA_Q_ROWS * GRID_W, 3 * NA_Q_ROWS * GRID_W + n_ctx)
    return pl.pallas_call(
        functools.partial(_na_bias_kernel, n_rows=n_rows),
        grid=(n_heads,),
        in_specs=[pl.BlockSpec(memory_space=pltpu.SMEM)],
        out_specs=pl.BlockSpec((3, None) + shape[2:], lambda h: (0, h, 0, 0)),
        out_shape=jax.ShapeDtypeStruct(shape, F32),
        scratch_shapes=[pltpu.VMEM((2 * NA_WIN_H - 1, GRID_W, 2 * GRID_W), F32)],
        compiler_params=_params(("parallel",), 24 << 20),
        name="na_bias_table",
    )(rpb.astype(F32).reshape(-1))


def _neighbourhood_attention(q, k, v, kx, vx, bias, *, q_blocks=NA_Q_BLOCKS):
    b, s, w = q.shape
    blk = NA_Q_ROWS * GRID_W
    n_halo = s // blk
    rows = q_blocks * blk
    assert s % rows == 0
    cx = kx.shape[1]
    cur = lambda bb, j: (bb, j, 0)
    prev = lambda bb, j: (bb, jnp.maximum(j * q_blocks - 1, 0), 0)
    nxt = lambda bb, j: (bb, jnp.minimum((j + 1) * q_blocks, n_halo - 1), 0)
    halo = lambda im: pl.BlockSpec((None, blk, w), im)
    tile = pl.BlockSpec((None, rows, w), cur)
    ctx = pl.BlockSpec((None, cx, w), lambda bb, j: (bb, 0, 0))
    vmem = 2 * bias.size * 4 + 8 * q_blocks * blk * bias.shape[3] * 4 + (8 << 20)
    return pl.pallas_call(
        _na_kernel,
        grid=(b, s // rows),
        in_specs=[tile, halo(prev), tile, halo(nxt), halo(prev), tile, halo(nxt), ctx, ctx,
                  pl.BlockSpec(bias.shape, lambda bb, j: (0, 0, 0, 0))],
        out_specs=tile,
        out_shape=jax.ShapeDtypeStruct((b, s, w), BF16),
        compiler_params=_params(("parallel", "parallel"), vmem),
        name="neighbourhood_attention",
    )(q, k, k, k, v, v, v, kx, vx, bias)


def _wa_band_bias(n_ctx):
    blk = WA_BLOCK
    p = np.arange(blk)[:, None]
    k = np.arange(3 * blk + n_ctx)[None, :]
    in_band = (k >= p) & (k <= p + 2 * blk)
    is_prev, is_next, is_ctx = k < blk, (k >= 2 * blk) & (k < 3 * blk), k >= 3 * blk
    tables = []
    for has_prev, has_next in ((False, True), (True, True), (True, False)):
        valid = is_ctx | (in_band & (has_prev | ~is_prev) & (has_next | ~is_next))
        tables.append(np.where(valid, 0.0, NEG_INF).astype(np.float32))
    return np.stack(tables)


def _wa_kernel(sink_ref, band_ref, q_ref, kp_ref, kc_ref, kn_ref, vp_ref, vc_ref, vn_ref, kx_ref, vx_ref,
               o_ref, *, group):
    j, n_tiles = pl.program_id(1), pl.num_programs(1)
    blk = WA_BLOCK
    q_blocks = q_ref.shape[0] // blk
    n_kv = kc_ref.shape[1] // HEAD_DIM

    def window(halo_before, tile, halo_after, qb, hs):
        rows = lambda t: slice(t * blk, (t + 1) * blk)
        before = halo_before[:, hs] if qb == 0 else tile[rows(qb - 1), hs]
        after = halo_after[:, hs] if qb == q_blocks - 1 else tile[rows(qb + 1), hs]
        return [before, tile[rows(qb), hs], after]

    scores, values, sinks = [], [], []
    for qb in range(q_blocks):
        variant = 1
        if qb == 0:
            variant = jnp.where(j == 0, 0, variant)
        if qb == q_blocks - 1:
            variant = jnp.where(j == n_tiles - 1, 2, variant)
        band = jnp.concatenate([band_ref[variant]] * group, axis=0)
        for kv in range(n_kv):
            hs = slice(kv * HEAD_DIM, (kv + 1) * HEAD_DIM)
            heads = [kv * group + g for g in range(group)]
            q = jnp.concatenate([q_ref[qb * blk:(qb + 1) * blk, h * HEAD_DIM:(h + 1) * HEAD_DIM] for h in heads],
                                axis=0)
            k = jnp.concatenate(window(kp_ref, kc_ref, kn_ref, qb, hs) + [kx_ref[:, hs]], axis=0)
            scores.append(_qk(q, k) + band)
            values.append(jnp.concatenate(window(vp_ref, vc_ref, vn_ref, qb, hs) + [vx_ref[:, hs]], axis=0))
            sinks.append(jnp.concatenate([jnp.full((blk, 1), sink_ref[h] * LOG2E, F32) for h in heads], axis=0))
    for e, out in enumerate(_attend(scores, values, sinks)):
        qb, kv = divmod(e, n_kv)
        for g in range(group):
            h = kv * group + g
            o_ref[qb * blk:(qb + 1) * blk, h * HEAD_DIM:(h + 1) * HEAD_DIM] = (
                out[g * blk:(g + 1) * blk, :].astype(o_ref.dtype))


def _window_attention(q, k, v, kx, vx, sink, *, q_blocks=WA_Q_BLOCKS):
    b, s, wq = q.shape
    wk = k.shape[2]
    group = wq // wk
    rows = q_blocks * WA_BLOCK
    n_tiles = s // rows
    n_halo = s // WA_BLOCK
    cx = kx.shape[1]
    assert s % rows == 0 and n_halo >= 2
    cur = lambda bb, j: (bb, j, 0)
    prev = lambda bb, j: (bb, jnp.maximum(j * q_blocks - 1, 0), 0)
    nxt = lambda bb, j: (bb, jnp.minimum((j + 1) * q_blocks, n_halo - 1), 0)
    halo = lambda im: pl.BlockSpec((None, WA_BLOCK, wk), im)
    tile = pl.BlockSpec((None, rows, wk), cur)
    ctx = pl.BlockSpec((None, cx, wk), lambda bb, j: (bb, 0, 0))
    band = jnp.asarray(_wa_band_bias(cx))
    vmem = 16 * q_blocks * group * WA_BLOCK * band.shape[2] * 4 + (8 << 20)
    return pl.pallas_call(
        functools.partial(_wa_kernel, group=group),
        grid=(b, n_tiles),
        in_specs=[pl.BlockSpec(memory_space=pltpu.SMEM),
                  pl.BlockSpec(band.shape, lambda bb, j: (0, 0, 0)),
                  pl.BlockSpec((None, rows, wq), cur),
                  halo(prev), tile, halo(nxt), halo(prev), tile, halo(nxt), ctx, ctx],
        out_specs=pl.BlockSpec((None, rows, wq), cur),
        out_shape=jax.ShapeDtypeStruct((b, s, wq), BF16),
        compiler_params=_params(("parallel", "parallel"), vmem),
        name="window_attention",
    )(sink, band, q, k, k, k, v, v, v, kx, vx)


def _ctx_attn_kernel(*refs, group, use_sink):
    if use_sink:
        sink_ref, q_ref, k_ref, v_ref, o_ref = refs
    else:
        q_ref, k_ref, v_ref, o_ref = refs
    c = q_ref.shape[0]
    for kv in range(k_ref.shape[1] // HEAD_DIM):
        hs = slice(kv * HEAD_DIM, (kv + 1) * HEAD_DIM)
        heads = [kv * group + g for g in range(group)]
        q = jnp.concatenate([q_ref[:, h * HEAD_DIM:(h + 1) * HEAD_DIM] for h in heads], axis=0)
        sink = None
        if use_sink:
            sink = jnp.concatenate([jnp.full((c, 1), sink_ref[h] * LOG2E, F32) for h in heads], axis=0)
        out, = _attend([_qk(q, k_ref[:, hs])], [v_ref[:, hs]], [sink])
        for g, h in enumerate(heads):
            o_ref[:, h * HEAD_DIM:(h + 1) * HEAD_DIM] = out[g * c:(g + 1) * c, :].astype(o_ref.dtype)


def _context_attention(q, k, v, sink):
    b, c, wq = q.shape
    wk = k.shape[2]
    use_sink = sink is not None
    spec = lambda w: pl.BlockSpec((None, c, w), lambda bb: (bb, 0, 0))
    in_specs = [spec(wq), spec(wk), spec(wk)]
    args = [q, k, v]
    if use_sink:
        in_specs = [pl.BlockSpec(memory_space=pltpu.SMEM)] + in_specs
        args = [sink] + args
    return pl.pallas_call(
        functools.partial(_ctx_attn_kernel, group=wq // wk, use_sink=use_sink),
        grid=(b,),
        in_specs=in_specs,
        out_specs=spec(wq),
        out_shape=jax.ShapeDtypeStruct((b, c, wq), BF16),
        compiler_params=_params(("parallel",), 24 << 20),
        name="context_attention",
    )(*args)


def _pool_inv_counts(seq_len, tm):
    starts = [0] if seq_len == tm else [0, tm, seq_len - tm]
    tables = []
    for start in starts:
        pos = start + np.arange(tm)
        cols = []
        for win in POOL_WINDOWS:
            lo = np.maximum(pos - win // 2, 0)
            hi = np.minimum(pos + (win - win // 2 - 1), seq_len - 1)
            cols.append(np.repeat((1.0 / (hi - lo + 1))[:, None], HEAD_DIM, axis=1))
        tables.append(np.concatenate(cols, axis=1))
    return np.stack(tables).astype(np.float32)


def _out_kernel(x_ref, mod_ref, a_ref, u_ref, up_ref, un_ref, inv_ref, c_ref, pw_ref, ps_ref, wo_ref, o_ref,
                ext_ref, *, seq_len):
    tm = x_ref.shape[0]
    tiles_per_seq = seq_len // tm
    t_in_seq = pl.program_id(0) % tiles_per_seq
    halo = POOL_HALO
    has_prev = (t_in_seq > 0).astype(F32)
    has_next = (t_in_seq < tiles_per_seq - 1).astype(F32)
    ext_ref[0:halo, :] = up_ref[...] * has_prev
    ext_ref[halo:halo + tm, :] = u_ref[...]
    ext_ref[halo + tm:, :] = un_ref[...] * has_next

    parts = []
    for g, win in enumerate(POOL_WINDOWS):
        ls = slice(g * HEAD_DIM, (g + 1) * HEAD_DIM)
        acc = ext_ref[:, ls]
        span = 1
        while span < win:
            acc = acc + pltpu.roll(acc, span, 0)
            span *= 2
        ahead = win - win // 2 - 1
        if ahead:
            acc = pltpu.roll(acc, acc.shape[0] - ahead, 0)
        total = acc[halo:halo + tm]
        delta = total * inv_ref[:, ls] - u_ref[:, ls]
        y = jnp.dot(delta.astype(BF16), pw_ref[g], preferred_element_type=F32) * ps_ref[:, ls]
        parts.append(y.astype(BF16))
    na_w = a_ref.shape[1]
    pool_w = len(POOL_WINDOWS) * HEAD_DIM
    mixed = jnp.dot(a_ref[...], wo_ref[0:na_w, :], preferred_element_type=F32)
    mixed += jnp.dot(c_ref[...], wo_ref[na_w + pool_w:, :], preferred_element_type=F32)
    mixed += jnp.dot(jnp.concatenate(parts, axis=1), wo_ref[na_w:na_w + pool_w, :], preferred_element_type=F32)
    o_ref[...] = x_ref[...] + mod_ref[5:6, :] * mixed


def _mix_out(x2, mod, a2, u2, c2, pool_w, pool_layer, pool_scale, w_out, seq_len, *, tm=512):
    n_tok, d = x2.shape
    tm = _row_tile(seq_len, tm)
    tiles_per_mod = n_tok // mod.shape[0] // tm
    n_halo = n_tok // POOL_HALO
    hpt = tm // POOL_HALO
    row = lambda w: pl.BlockSpec((tm, w), lambda i: (i, 0))
    tiles_per_seq = seq_len // tm
    inv = jnp.asarray(_pool_inv_counts(seq_len, tm))

    def edge(i):
        t = i % tiles_per_seq
        return (0 if tiles_per_seq == 1 else jnp.where(t == 0, 0, jnp.where(t == tiles_per_seq - 1, 2, 1)), 0, 0)

    vmem = 4 * tm * d * 4 + 2 * w_out[0].size * 2 + 4 * tm * d * 4 + (6 << 20)
    return pl.pallas_call(
        functools.partial(_out_kernel, seq_len=seq_len),
        grid=(n_tok // tm,),
        in_specs=[row(d),
                  pl.BlockSpec((None, N_MOD, d), lambda i: (i // tiles_per_mod, 0, 0)),
                  row(a2.shape[1]), row(u2.shape[1]),
                  pl.BlockSpec((POOL_HALO, u2.shape[1]), lambda i: (jnp.maximum(i * hpt - 1, 0), 0)),
                  pl.BlockSpec((POOL_HALO, u2.shape[1]), lambda i: (jnp.minimum((i + 1) * hpt, n_halo - 1), 0)),
                  pl.BlockSpec((None,) + inv.shape[1:], edge),
                  row(c2.shape[1]),
                  pl.BlockSpec((None,) + pool_w.shape[1:], lambda i: (pool_layer, 0, 0, 0)),
                  pl.BlockSpec((1, u2.shape[1]), lambda i: (0, 0)),
                  pl.BlockSpec((None,) + w_out.shape[1:], lambda i: (0, 0, 0))],
        out_specs=row(d),
        out_shape=jax.ShapeDtypeStruct((n_tok, d), F32),
        scratch_shapes=[pltpu.VMEM((tm + 2 * POOL_HALO, u2.shape[1]), F32)],
        compiler_params=_params(("parallel",), vmem),
        name="pool_mix_out",
    )(x2, mod, a2, u2, u2, u2, inv, c2, pool_w, pool_scale.reshape(1, -1), w_out)


def _rope_tables(seq_len):
    n_rows = seq_len // GRID_W
    axis_dim = HEAD_DIM // 2
    inv_freq = ROPE_BASE ** (-jnp.arange(0, axis_dim, 2, dtype=F32) / axis_dim)
    ang_r = jnp.arange(n_rows, dtype=F32)[:, None] * inv_freq[None, :]
    ang_c = jnp.arange(GRID_W, dtype=F32)[:, None] * inv_freq[None, :]
    by_row = lambda t: jnp.broadcast_to(t[:, None, :], (n_rows, GRID_W, t.shape[-1]))
    by_col = lambda t: jnp.broadcast_to(t[None, :, :], (n_rows, GRID_W, t.shape[-1]))
    cos_r, sin_r = by_row(jnp.cos(ang_r)), by_row(jnp.sin(ang_r))
    cos_c, sin_c = by_col(jnp.cos(ang_c)), by_col(jnp.sin(ang_c))
    cos = jnp.concatenate([cos_r, cos_r, cos_c, cos_c], axis=-1).reshape(seq_len, HEAD_DIM)
    sin = jnp.concatenate([-sin_r, sin_r, -sin_c, sin_c], axis=-1).reshape(seq_len, HEAD_DIM)
    return cos, sin


def kernel(x, c, ctx, c_ctx, w_mod, b_mod, norm_w, ffn1_wi, ffn1_wo, ffn2_wi, ffn2_wo, w_in, w_out,
           na_qk_gain, na_rpb, pool_w, pool_scale, wa_qk_gain, wa_sink):
    b, s, d = x.shape
    cx = ctx.shape[1]
    depth = w_mod.shape[0]
    na_w = na_rpb.shape[1] * HEAD_DIM
    pw = pool_scale.shape[1]
    kv_w = (w_in.shape[2] - 3 * na_w - pw - (d - na_w - pw)) // 2
    widths = (na_w, pw, d - na_w - pw, kv_w)

    n_cond = b + 1
    cvecs = jnp.concatenate([c, c_ctx[None, :], jnp.zeros((-n_cond % SUBLANES, d), F32)], axis=0)
    mods = _modulation(cvecs, w_mod, b_mod).reshape(depth, cvecs.shape[0], N_MOD, d)
    rope = _rope_tables(s)
    wi1, wo1 = ffn1_wi[:1].astype(BF16), ffn1_wo[:1].astype(BF16)
    pwb = pool_w.astype(BF16)

    x2 = x.reshape(b * s, d)
    c2 = ctx.reshape(b * cx, d)
    seq = lambda t: t.reshape(b, s, -1)
    cseq = lambda t: t.reshape(b, cx, -1)
    for l in range(depth):
        last = l == depth - 1
        mx, mc = mods[l, :b], mods[l, b:b + 1]

        x2, (win, wout, wi2, wo2) = _ffn(x2, mx, 0, norm_w[l, 0], wi1, wo1, 0,
                                         cast=[(w_in, l), (w_out, l), (ffn2_wi, l), (ffn2_wo, l)])
        c2, _ = _ffn(c2, mc, 0, norm_w[l, 0], wi1, wo1, 0)

        nq, nk, nv, u, wq, wk, wv = _projection(x2, mx, norm_w[l, 1], win, 0, na_qk_gain[l], wa_qk_gain[l],
                                                rope, widths)
        cnq, cnk, cnv, cu, cwq, cwk, cwv = _projection(c2, mc, norm_w[l, 1], win, 0, na_qk_gain[l],
                                                       wa_qk_gain[l], None, widths)
        a_out = _neighbourhood_attention(seq(nq), seq(nk), seq(nv), cseq(cnk), cseq(cnv),
                                         _na_bias_table(na_rpb[l], s // GRID_W, cx))
        c_out = _window_attention(seq(wq), seq(wk), seq(wv), cseq(cwk), cseq(cwv), wa_sink[l])
        x2 = _mix_out(x2, mx, a_out.reshape(b * s, -1), u, c_out.reshape(b * s, -1), pwb, l, pool_scale[l],
                      wout, s)
        if not last:
            ca = _context_attention(cseq(cnq), cseq(cnk), cseq(cnv), None)
            cc = _context_attention(cseq(cwq), cseq(cwk), cseq(cwv), wa_sink[l])
            c2 = _mix_out(c2, mc, ca.reshape(b * cx, -1), cu, cc.reshape(b * cx, -1), pwb, l, pool_scale[l],
                          wout, cx)
            c2, _ = _ffn(c2, mc, 6, norm_w[l, 2], wi2, wo2, 0)
        x2, nxt = _ffn(x2, mx, 6, norm_w[l, 2], wi2, wo2, 0,
                       cast=[] if last else [(ffn1_wi, l + 1), (ffn1_wo, l + 1)])
        if not last:
            wi1, wo1 = nxt
    return x2.reshape(b, s, d)
```

```python
import functools

import numpy as np
import jax
import jax.numpy as jnp
from jax import lax
from jax.experimental import pallas as pl
from jax.experimental.pallas import tpu as pltpu

HEAD_DIM = 128
GRID_W = 64
NA_WIN_H = 8
NA_WIN_W = 16
POOL_WINDOWS = (2, 4, 8, 16)
WA_BLOCK = 128
WA_Q_BLOCKS = 8
N_MOD = 9
MACARON_WEIGHT = 0.5
ROPE_BASE = 10000.0
RMS_EPS = 1e-6
NEG_INF = -1e30
LOG2E = 1.4426950408889634
Q_SCALE = HEAD_DIM ** -0.5 * LOG2E

V7X_VMEM_BYTES = 64 * 1024 * 1024
V7X_VMEM_BUDGET = 56 * 1024 * 1024
SUBLANES = 8
NA_Q_ROWS = NA_WIN_H // 2
NA_Q_BLOCKS = 4
POOL_HALO = 16

BF16 = jnp.bfloat16
F32 = jnp.float32


def _params(semantics, vmem_bytes):
    return pltpu.CompilerParams(dimension_semantics=semantics,
                                vmem_limit_bytes=int(min(max(vmem_bytes, 16 << 20), V7X_VMEM_BUDGET)))


def _row_tile(n_rows, want):
    t = min(want, n_rows)
    assert n_rows % t == 0, (n_rows, t)
    return t


def _mod_norm(x, mod_ref, g_ref, first):
    y = x * lax.rsqrt(jnp.mean(x * x, axis=-1, keepdims=True) + RMS_EPS) * g_ref[...]
    return y * (1.0 + mod_ref[first + 1:first + 2, :]) + mod_ref[first:first + 1, :]


def _mod_kernel(c_ref, w_ref, b_ref, o_ref):
    c = c_ref[...]
    s = (c * jax.nn.sigmoid(c)).astype(BF16)
    o_ref[...] = jnp.dot(s, w_ref[...].astype(BF16), preferred_element_type=F32) + b_ref[...]


def _modulation(cvecs, w_mod, b_mod):
    n_layers, d, nd = w_mod.shape
    rows = cvecs.shape[0]
    tn = _row_tile(nd, 2048)
    return pl.pallas_call(
        _mod_kernel,
        grid=(n_layers, nd // tn),
        in_specs=[pl.BlockSpec((rows, d), lambda l, n: (0, 0)),
                  pl.BlockSpec((None, d, tn), lambda l, n: (l, 0, n)),
                  pl.BlockSpec((None, 1, tn), lambda l, n: (l, 0, n))],
        out_specs=pl.BlockSpec((None, rows, tn), lambda l, n: (l, 0, n)),
        out_shape=jax.ShapeDtypeStruct((n_layers, rows, nd), F32),
        compiler_params=_params(("parallel", "parallel"), 2 * d * tn * 4 + (8 << 20)),
        name="adaln_modulation",
    )(cvecs, w_mod, b_mod.reshape(n_layers, 1, nd))


def _ffn_kernel(*refs, first, tiles_per_mod, n_chunks, n_cast):
    x_hbm, mod_ref, g_ref, wa_ref, wb_ref, wo_ref = refs[:6]
    cast_in = refs[6:6 + n_cast]
    o_hbm = refs[6 + n_cast]
    cast_out = refs[7 + n_cast:7 + 2 * n_cast]
    xa_ref, xb_ref, ha_ref, hb_ref, in_sem, out_sem = refs[7 + 2 * n_cast:]
    i, f = pl.program_id(0), pl.program_id(1)
    n_tiles, nf = pl.num_programs(0), pl.num_programs(1)
    tm = xa_ref.shape[0]
    rows = tm // n_chunks
    head = tm - rows

    def x_in(tile, dst, part, slot):
        r0, n = ((0, head), (head, rows))[part]
        return pltpu.make_async_copy(x_hbm.at[pl.ds(tile * tm + r0, n), :], dst.at[pl.ds(r0, n), :],
                                     in_sem.at[slot, part])

    def x_out(tile, src, slot):
        return pltpu.make_async_copy(src, o_hbm.at[pl.ds(tile * tm, tm), :], out_sem.at[slot])

    def normed(x_rows, batch):
        return _mod_norm(x_rows, mod_ref.at[batch], g_ref, first).astype(BF16)

    def tile_step(slot, x_cur, x_nxt, h_cur, h_nxt):
        other = 1 - slot
        has_next = i + 1 < n_tiles

        if slot == 0:
            @pl.when((i == 0) & (f == 0))
            def _():
                for part in (0, 1):
                    x_in(0, x_cur, part, slot).start()
                x_nxt[...] = jnp.zeros(x_nxt.shape, x_nxt.dtype)
                for part in (0, 1):
                    x_in(0, x_cur, part, slot).wait()
                h_cur[...] = normed(x_cur[...], 0)

        @pl.when((f == 1) & (i > 0))
        def _():
            x_out(i - 1, x_nxt, other).wait()

        @pl.when((f == 1) & has_next)
        def _():
            x_in(i + 1, x_nxt, 0, other).start()

        @pl.when((f == 2) & has_next)
        def _():
            x_in(i + 1, x_nxt, 0, other).wait()
            x_in(i + 1, x_nxt, 1, other).start()

        @pl.when((f == 3) & has_next)
        def _():
            x_in(i + 1, x_nxt, 1, other).wait()

        for src, dst in zip(cast_in, cast_out):
            dst[...] = src[...].astype(dst.dtype)

        chunk = jnp.where((f >= 2) & (f < 2 + n_chunks), f - 2, n_chunks - 1)
        r0 = pl.multiple_of(chunk * rows, rows)
        next_batch = jnp.minimum(i + 1, n_tiles - 1) // tiles_per_mod
        h_nxt[pl.ds(r0, rows), :] = normed(x_nxt[pl.ds(r0, rows), :], next_batch)

        gate = MACARON_WEIGHT * mod_ref[i // tiles_per_mod, first + 2:first + 3, :]
        h = h_cur[...]
        a = jnp.dot(h, wa_ref[...], preferred_element_type=F32)
        b = jnp.dot(h, wb_ref[...], preferred_element_type=F32)
        act = (a * jax.nn.sigmoid(a) * b).astype(BF16)
        x_cur[...] += gate * jnp.dot(act, wo_ref[...], preferred_element_type=F32)

        @pl.when(f == nf - 1)
        def _():
            x_out(i, x_cur, slot).start()

        @pl.when((f == nf - 1) & (i == n_tiles - 1))
        def _():
            x_out(i, x_cur, slot).wait()

    @pl.when(i % 2 == 0)
    def _():
        tile_step(0, xa_ref, xb_ref, ha_ref, hb_ref)

    @pl.when(i % 2 == 1)
    def _():
        tile_step(1, xb_ref, xa_ref, hb_ref, ha_ref)


def _ffn(x2, mod, first, norm_g, wi, wo, layer, cast=(), *, tm=1024, tf=512):
    n_tok, d = x2.shape
    d_ff = wo.shape[1]
    tm = _row_tile(n_tok // mod.shape[0], tm)
    tf = _row_tile(d_ff, tf)
    tiles_per_mod = n_tok // mod.shape[0] // tm
    nf = d_ff // tf
    n_steps = (n_tok // tm) * nf
    n_chunks = 1 << ((nf - 2).bit_length() - 1)
    assert nf >= 4 and tm % (n_chunks * 2 * SUBLANES) == 0, (nf, tm)

    cast_in_specs, cast_out_specs, cast_shapes = [], [], []
    for arr, lyr in cast:
        _, rows, cols = arr.shape
        block = next(r for r in range(2 * SUBLANES, rows + 1, 2 * SUBLANES)
                     if rows % r == 0 and rows // r <= n_steps)
        last = rows // block - 1
        cast_in_specs.append(pl.BlockSpec(
            (None, block, cols), lambda i, f, last=last, lyr=lyr: (lyr, jnp.minimum(i * nf + f, last), 0)))
        cast_out_specs.append(pl.BlockSpec(
            (None, block, cols), lambda i, f, last=last: (0, jnp.minimum(i * nf + f, last), 0)))
        cast_shapes.append(jax.ShapeDtypeStruct((1, rows, cols), BF16))

    vmem = 2 * tm * d * 4 + 2 * tm * d * 2 + 2 * 3 * d * tf * 2 + 8 * tm * tf * 4 + (6 << 20)
    out, *cast_out = pl.pallas_call(
        functools.partial(_ffn_kernel, first=first, tiles_per_mod=tiles_per_mod, n_chunks=n_chunks,
                          n_cast=len(cast)),
        grid=(n_tok // tm, nf),
        in_specs=[pl.BlockSpec(memory_space=pl.ANY),
                  pl.BlockSpec(mod.shape, lambda i, f: (0, 0, 0)),
                  pl.BlockSpec((1, d), lambda i, f: (0, 0)),
                  pl.BlockSpec((None, d, tf), lambda i, f: (layer, 0, f)),
                  pl.BlockSpec((None, d, tf), lambda i, f: (layer, 0, nf + f)),
                  pl.BlockSpec((None, tf, d), lambda i, f: (layer, f, 0))] + cast_in_specs,
        out_specs=[pl.BlockSpec(memory_space=pl.ANY)] + cast_out_specs,
        out_shape=[jax.ShapeDtypeStruct((n_tok, d), F32)] + cast_shapes,
        scratch_shapes=[pltpu.VMEM((tm, d), F32), pltpu.VMEM((tm, d), F32),
                        pltpu.VMEM((tm, d), BF16), pltpu.VMEM((tm, d), BF16),
                        pltpu.SemaphoreType.DMA((2, 2)), pltpu.SemaphoreType.DMA((2,))],
        compiler_params=_params(("arbitrary", "arbitrary"), vmem),
        name="macaron_ffn",
    )(x2, mod, norm_g.reshape(1, d), wi, wi, wo, *[arr for arr, _ in cast])
    return out, cast_out


def _head_norm(p, gain):
    return p * lax.rsqrt(jnp.mean(p * p, axis=-1, keepdims=True) + RMS_EPS) * gain


def _rotate(y, cos, sin_signed):
    lane = lax.broadcasted_iota(jnp.int32, y.shape, 1)
    partner = jnp.where((lane & 32) == 0, pltpu.roll(y, HEAD_DIM - 32, 1), pltpu.roll(y, 32, 1))
    return y * cos + partner * sin_signed


def _proj_kernel(*refs, use_rope, widths, row_groups):
    if use_rope:
        x_ref, mod_ref, g_ref, w_ref, nag_ref, wag_ref, cos_ref, sin_ref = refs[:8]
        outs = refs[8:]
    else:
        x_ref, mod_ref, g_ref, w_ref, nag_ref, wag_ref = refs[:6]
        outs = refs[6:]
    nq_ref, nk_ref, nv_ref, u_ref, wq_ref, wk_ref, wv_ref = outs
    na_w, pool_w, wa_w, kv_w = widths
    c_nq, c_nk, c_nv, c_u = 0, na_w, 2 * na_w, 3 * na_w
    c_wq = c_u + pool_w
    c_wk = c_wq + wa_w
    c_wv = c_wk + kv_w
    half = wa_w // 2

    def project(rs):
        h = _mod_norm(x_ref[rs, :], mod_ref, g_ref, 3).astype(BF16)

        def chunk(col0, width):
            return jnp.dot(h, w_ref[:, col0:col0 + width], preferred_element_type=F32)

        def normed(p, gain, out_ref, out_col0, scale, rope):
            for j in range(p.shape[1] // HEAD_DIM):
                y = _head_norm(p[:, j * HEAD_DIM:(j + 1) * HEAD_DIM], gain)
                if rope:
                    y = _rotate(y, cos_ref[rs, :], sin_ref[rs, :])
                if scale != 1.0:
                    y = y * scale
                c0 = out_col0 + j * HEAD_DIM
                out_ref[rs, c0:c0 + HEAD_DIM] = y.astype(out_ref.dtype)

        normed(chunk(c_wq, half), wag_ref[0:1, :], wq_ref, 0, Q_SCALE, use_rope)
        normed(chunk(c_wq + half, half), wag_ref[0:1, :], wq_ref, half, Q_SCALE, use_rope)
        normed(chunk(c_wk, kv_w), wag_ref[1:2, :], wk_ref, 0, 1.0, use_rope)
        normed(chunk(c_nq, na_w), nag_ref[0:1, :], nq_ref, 0, Q_SCALE, False)
        normed(chunk(c_nk, na_w), nag_ref[1:2, :], nk_ref, 0, 1.0, False)
        nv_ref[rs, :] = chunk(c_nv, na_w).astype(BF16)
        u_ref[rs, :] = chunk(c_u, pool_w)
        wv_ref[rs, :] = chunk(c_wv, kv_w).astype(BF16)

    rows = x_ref.shape[0] // row_groups
    for r in range(row_groups):
        project(slice(r * rows, (r + 1) * rows))


def _projection(x2, mod, norm_g, w_in, layer, na_gain, wa_gain, rope, widths, *, tm=1024, group_rows=512):
    n_tok, d = x2.shape
    na_w, pool_w, wa_w, kv_w = widths
    n_in = w_in.shape[2]
    tm = _row_tile(n_tok // mod.shape[0], tm)
    row_groups = max(tm // group_rows, 1)
    tiles_per_mod = n_tok // mod.shape[0] // tm
    use_rope = rope is not None
    in_specs = [pl.BlockSpec((tm, d), lambda i: (i, 0)),
                pl.BlockSpec((None, N_MOD, d), lambda i: (i // tiles_per_mod, 0, 0)),
                pl.BlockSpec((1, d), lambda i: (0, 0)),
                pl.BlockSpec((None, d, n_in), lambda i: (layer, 0, 0), pipeline_mode=pl.Buffered(1)),
                pl.BlockSpec((2, HEAD_DIM), lambda i: (0, 0)),
                pl.BlockSpec((2, HEAD_DIM), lambda i: (0, 0))]
    args = [x2, mod, norm_g.reshape(1, d), w_in, na_gain, wa_gain]
    if use_rope:
        tiles_per_seq = rope[0].shape[0] // tm
        in_specs += [pl.BlockSpec((tm, HEAD_DIM), lambda i: (i % tiles_per_seq, 0))] * 2
        args += list(rope)
    out_widths = (na_w, na_w, na_w, pool_w, wa_w, kv_w, kv_w)
    out_dtypes = (BF16, BF16, BF16, F32, BF16, BF16, BF16)
    vmem = 2 * tm * d * 4 + tm * d * 2 + d * n_in * 2 + 4 * tm * n_in * 4 + (4 << 20)
    return pl.pallas_call(
        functools.partial(_proj_kernel, use_rope=use_rope, widths=widths, row_groups=row_groups),
        grid=(n_tok // tm,),
        in_specs=in_specs,
        out_specs=[pl.BlockSpec((tm, w), lambda i: (i, 0)) for w in out_widths],
        out_shape=[jax.ShapeDtypeStruct((n_tok, w), dt) for w, dt in zip(out_widths, out_dtypes)],
        compiler_params=_params(("parallel",), vmem),
        name="input_projection",
    )(*args)


def _softmax_weights(s, extra_logit=None):
    m = jnp.max(s, axis=-1, keepdims=True)
    if extra_logit is not None:
        m = jnp.maximum(m, extra_logit)
    p = jnp.exp2(s - m)
    denom = jnp.sum(p, axis=-1, keepdims=True)
    if extra_logit is not None:
        denom = denom + jnp.exp2(extra_logit - m)
    return p.astype(BF16), denom


def _attend(scores, values, extra_logits=None):
    extra_logits = extra_logits or [None] * len(scores)
    weights = [_softmax_weights(s, e) for s, e in zip(scores, extra_logits)]
    return [jnp.dot(p, v, preferred_element_type=F32) / denom for (p, denom), v in zip(weights, values)]


def _qk(q, k):
    return lax.dot_general(q, k, (((1,), (1,)), ((), ())), preferred_element_type=F32)


def _na_kernel(q_ref, kp_ref, kc_ref, kn_ref, vp_ref, vc_ref, vn_ref, kx_ref, vx_ref, bias_ref, o_ref):
    j, n_tiles = pl.program_id(1), pl.num_programs(1)
    blk = NA_Q_ROWS * GRID_W
    q_blocks = q_ref.shape[0] // blk
    n_heads = q_ref.shape[1] // HEAD_DIM
    heads = [slice(h * HEAD_DIM, (h + 1) * HEAD_DIM) for h in range(n_heads)]
    rows = lambda t: slice(t * blk, (t + 1) * blk)

    def window(halo_before, tile, halo_after, qb, hs):
        before = halo_before[:, hs] if qb == 0 else tile[rows(qb - 1), hs]
        after = halo_after[:, hs] if qb == q_blocks - 1 else tile[rows(qb + 1), hs]
        return [before, tile[rows(qb), hs], after]

    scores, values = [], []
    for qb in range(q_blocks):
        variant = 1
        if qb == 0:
            variant = jnp.where(j == 0, 0, variant)
        if qb == q_blocks - 1:
            variant = jnp.where(j == n_tiles - 1, 2, variant)
        for h, hs in enumerate(heads):
            k = jnp.concatenate(window(kp_ref, kc_ref, kn_ref, qb, hs) + [kx_ref[:, hs]], axis=0)
            scores.append(_qk(q_ref[rows(qb), hs], k) + bias_ref[variant, h])
            values.append(jnp.concatenate(window(vp_ref, vc_ref, vn_ref, qb, hs) + [vx_ref[:, hs]], axis=0))
    for e, out in enumerate(_attend(scores, values)):
        qb, h = divmod(e, n_heads)
        o_ref[rows(qb), heads[h]] = out.astype(o_ref.dtype)


def _na_bias_kernel(rpb_ref, o_ref, g_ref, *, n_rows):
    n_ro, n_co = 2 * NA_WIN_H - 1, 2 * NA_WIN_W - 1
    shape = (GRID_W, 2 * GRID_W)
    qc = lax.broadcasted_iota(jnp.int32, shape, 0)
    lane = lax.broadcasted_iota(jnp.int32, shape, 1)
    kc = lane & (GRID_W - 1)
    c0 = jnp.clip(qc - NA_WIN_W // 2, 0, GRID_W - NA_WIN_W)
    in_cols = (kc >= c0) & (kc < c0 + NA_WIN_W)
    col_off = kc - qc + (NA_WIN_W - 1)
    base = pl.program_id(0) * (n_ro * n_co)
    for ro in range(n_ro):
        g = jnp.zeros(shape, F32)
        for c in range(n_co):
            g = jnp.where(col_off == c, rpb_ref[base + ro * n_co + c] * LOG2E, g)
        g_ref[ro] = jnp.where(in_cols, g, NEG_INF)

    masked = jnp.full(shape, NEG_INF, F32)
    n_local = 3 * NA_Q_ROWS * GRID_W
    n_blocks = n_rows // NA_Q_ROWS
    o_ref[:, :, n_local:] = jnp.zeros(o_ref.shape[:2] + (o_ref.shape[2] - n_local,), F32)
    for variant, jb in enumerate((0, 1, n_blocks - 1)):
        for i in range(NA_Q_ROWS):
            r = jb * NA_Q_ROWS + i
            r0 = min(max(r - NA_WIN_H // 2, 0), n_rows - NA_WIN_H)
            halves = []
            for kr in range(3 * NA_Q_ROWS):
                kr_abs = (jb - 1) * NA_Q_ROWS + kr
                in_rows = 0 <= kr_abs < n_rows and r0 <= kr_abs < r0 + NA_WIN_H
                halves.append(g_ref[kr_abs - r + NA_WIN_H - 1] if in_rows else masked)
            for m in range(len(halves) // 2):
                o_ref[variant, i * GRID_W:(i + 1) * GRID_W, m * 2 * GRID_W:(m + 1) * 2 * GRID_W] = jnp.where(
                    lane < GRID_W, halves[2 * m], halves[2 * m + 1])


def _na_bias_table(rpb, n_rows, n_ctx):
    n_heads = rpb.shape[0]
    assert n_rows % NA_Q_ROWS == 0 and n_rows // NA_Q_ROWS >= 3 and (3 * NA_Q_ROWS) % 2 == 0
    shape = (3, n_heads, NA_Q_ROWS * GRID_W, 3 * NA_Q_ROWS * GRID_W + n_ctx)
    return pl.pallas_call(
        functools.partial(_na_bias_kernel, n_rows=n_rows),
        grid=(n_heads,),
        in_specs=[pl.BlockSpec(memory_space=pltpu.SMEM)],
        out_specs=pl.BlockSpec((3, None) + shape[2:], lambda h: (0, h, 0, 0)),
        out_shape=jax.ShapeDtypeStruct(shape, F32),
        scratch_shapes=[pltpu.VMEM((2 * NA_WIN_H - 1, GRID_W, 2 * GRID_W), F32)],
        compiler_params=_params(("parallel",), 24 << 20),
        name="na_bias_table",
    )(rpb.astype(F32).reshape(-1))


def _neighbourhood_attention(q, k, v, kx, vx, bias, *, q_blocks=NA_Q_BLOCKS):
    b, s, w = q.shape
    blk = NA_Q_ROWS * GRID_W
    n_halo = s // blk
    rows = q_blocks * blk
    assert s % rows == 0
    cx = kx.shape[1]
    cur = lambda bb, j: (bb, j, 0)
    prev = lambda bb, j: (bb, jnp.maximum(j * q_blocks - 1, 0), 0)
    nxt = lambda bb, j: (bb, jnp.minimum((j + 1) * q_blocks, n_halo - 1), 0)
    halo = lambda im: pl.BlockSpec((None, blk, w), im)
    tile = pl.BlockSpec((None, rows, w), cur)
    ctx = pl.BlockSpec((None, cx, w), lambda bb, j: (bb, 0, 0))
    vmem = 2 * bias.size * 4 + 8 * q_blocks * blk * bias.shape[3] * 4 + (8 << 20)
    return pl.pallas_call(
        _na_kernel,
        grid=(b, s // rows),
        in_specs=[tile, halo(prev), tile, halo(nxt), halo(prev), tile, halo(nxt), ctx, ctx,
                  pl.BlockSpec(bias.shape, lambda bb, j: (0, 0, 0, 0))],
        out_specs=tile,
        out_shape=jax.ShapeDtypeStruct((b, s, w), BF16),
        compiler_params=_params(("parallel", "parallel"), vmem),
        name="neighbourhood_attention",
    )(q, k, k, k, v, v, v, kx, vx, bias)


def _wa_band_bias(n_ctx):
    blk = WA_BLOCK
    p = np.arange(blk)[:, None]
    k = np.arange(3 * blk + n_ctx)[None, :]
    in_band = (k >= p) & (k <= p + 2 * blk)
    is_prev, is_next, is_ctx = k < blk, (k >= 2 * blk) & (k < 3 * blk), k >= 3 * blk
    tables = []
    for has_prev, has_next in ((False, True), (True, True), (True, False)):
        valid = is_ctx | (in_band & (has_prev | ~is_prev) & (has_next | ~is_next))
        tables.append(np.where(valid, 0.0, NEG_INF).astype(np.float32))
    return np.stack(tables)


def _wa_kernel(sink_ref, band_ref, q_ref, kp_ref, kc_ref, kn_ref, vp_ref, vc_ref, vn_ref, kx_ref, vx_ref,
               o_ref, *, group):
    j, n_tiles = pl.program_id(1), pl.num_programs(1)
    blk = WA_BLOCK
    q_blocks = q_ref.shape[0] // blk
    n_kv = kc_ref.shape[1] // HEAD_DIM

    def window(halo_before, tile, halo_after, qb, hs):
        rows = lambda t: slice(t * blk, (t + 1) * blk)
        before = halo_before[:, hs] if qb == 0 else tile[rows(qb - 1), hs]
        after = halo_after[:, hs] if qb == q_blocks - 1 else tile[rows(qb + 1), hs]
        return [before, tile[rows(qb), hs], after]

    scores, values, sinks = [], [], []
    for qb in range(q_blocks):
        variant = 1
        if qb == 0:
            variant = jnp.where(j == 0, 0, variant)
        if qb == q_blocks - 1:
            variant = jnp.where(j == n_tiles - 1, 2, variant)
        band = jnp.concatenate([band_ref[variant]] * group, axis=0)
        for kv in range(n_kv):
            hs = slice(kv * HEAD_DIM, (kv + 1) * HEAD_DIM)
            heads = [kv * group + g for g in range(group)]
            q = jnp.concatenate([q_ref[qb * blk:(qb + 1) * blk, h * HEAD_DIM:(h + 1) * HEAD_DIM] for h in heads],
                                axis=0)
            k = jnp.concatenate(window(kp_ref, kc_ref, kn_ref, qb, hs) + [kx_ref[:, hs]], axis=0)
            scores.append(_qk(q, k) + band)
            values.append(jnp.concatenate(window(vp_ref, vc_ref, vn_ref, qb, hs) + [vx_ref[:, hs]], axis=0))
            sinks.append(jnp.concatenate([jnp.full((blk, 1), sink_ref[h] * LOG2E, F32) for h in heads], axis=0))
    for e, out in enumerate(_attend(scores, values, sinks)):
        qb, kv = divmod(e, n_kv)
        for g in range(group):
            h = kv * group + g
            o_ref[qb * blk:(qb + 1) * blk, h * HEAD_DIM:(h + 1) * HEAD_DIM] = (
                out[g * blk:(g + 1) * blk, :].astype(o_ref.dtype))


def _window_attention(q, k, v, kx, vx, sink, *, q_blocks=WA_Q_BLOCKS):
    b, s, wq = q.shape
    wk = k.shape[2]
    group = wq // wk
    rows = q_blocks * WA_BLOCK
    n_tiles = s // rows
    n_halo = s // WA_BLOCK
    cx = kx.shape[1]
    assert s % rows == 0 and n_halo >= 2
    cur = lambda bb, j: (bb, j, 0)
    prev = lambda bb, j: (bb, jnp.maximum(j * q_blocks - 1, 0), 0)
    nxt = lambda bb, j: (bb, jnp.minimum((j + 1) * q_blocks, n_halo - 1), 0)
    halo = lambda im: pl.BlockSpec((None, WA_BLOCK, wk), im)
    tile = pl.BlockSpec((None, rows, wk), cur)
    ctx = pl.BlockSpec((None, cx, wk), lambda bb, j: (bb, 0, 0))
    band = jnp.asarray(_wa_band_bias(cx))
    vmem = 16 * q_blocks * group * WA_BLOCK * band.shape[2] * 4 + (8 << 20)
    return pl.pallas_call(
        functools.partial(_wa_kernel, group=group),
        grid=(b, n_tiles),
        in_specs=[pl.BlockSpec(memory_space=pltpu.SMEM),
                  pl.BlockSpec(band.shape, lambda bb, j: (0, 0, 0)),
                  pl.BlockSpec((None, rows, wq), cur),
                  halo(prev), tile, halo(nxt), halo(prev), tile, halo(nxt), ctx, ctx],
        out_specs=pl.BlockSpec((None, rows, wq), cur),
        out_shape=jax.ShapeDtypeStruct((b, s, wq), BF16),
        compiler_params=_params(("parallel", "parallel"), vmem),
        name="window_attention",
    )(sink, band, q, k, k, k, v, v, v, kx, vx)


def _ctx_attn_kernel(*refs, group, use_sink):
    if use_sink:
        sink_ref, q_ref, k_ref, v_ref, o_ref = refs
    else:
        q_ref, k_ref, v_ref, o_ref = refs
    c = q_ref.shape[0]
    for kv in range(k_ref.shape[1] // HEAD_DIM):
        hs = slice(kv * HEAD_DIM, (kv + 1) * HEAD_DIM)
        heads = [kv * group + g for g in range(group)]
        q = jnp.concatenate([q_ref[:, h * HEAD_DIM:(h + 1) * HEAD_DIM] for h in heads], axis=0)
        sink = None
        if use_sink:
            sink = jnp.concatenate([jnp.full((c, 1), sink_ref[h] * LOG2E, F32) for h in heads], axis=0)
        out, = _attend([_qk(q, k_ref[:, hs])], [v_ref[:, hs]], [sink])
        for g, h in enumerate(heads):
            o_ref[:, h * HEAD_DIM:(h + 1) * HEAD_DIM] = out[g * c:(g + 1) * c, :].astype(o_ref.dtype)


def _context_attention(q, k, v, sink):
    b, c, wq = q.shape
    wk = k.shape[2]
    use_sink = sink is not None
    spec = lambda w: pl.BlockSpec((None, c, w), lambda bb: (bb, 0, 0))
    in_specs = [spec(wq), spec(wk), spec(wk)]
    args = [q, k, v]
    if use_sink:
        in_specs = [pl.BlockSpec(memory_space=pltpu.SMEM)] + in_specs
        args = [sink] + args
    return pl.pallas_call(
        functools.partial(_ctx_attn_kernel, group=wq // wk, use_sink=use_sink),
        grid=(b,),
        in_specs=in_specs,
        out_specs=spec(wq),
        out_shape=jax.ShapeDtypeStruct((b, c, wq), BF16),
        compiler_params=_params(("parallel",), 24 << 20),
        name="context_attention",
    )(*args)


def _pool_inv_counts(seq_len, tm):
    starts = [0] if seq_len == tm else [0, tm, seq_len - tm]
    tables = []
    for start in starts:
        pos = start + np.arange(tm)
        cols = []
        for win in POOL_WINDOWS:
            lo = np.maximum(pos - win // 2, 0)
            hi = np.minimum(pos + (win - win // 2 - 1), seq_len - 1)
            cols.append(np.repeat((1.0 / (hi - lo + 1))[:, None], HEAD_DIM, axis=1))
        tables.append(np.concatenate(cols, axis=1))
    return np.stack(tables).astype(np.float32)


def _out_kernel(x_ref, mod_ref, a_ref, u_ref, up_ref, un_ref, inv_ref, c_ref, pw_ref, ps_ref, wo_ref, o_ref,
                ext_ref, *, seq_len):
    tm = x_ref.shape[0]
    tiles_per_seq = seq_len // tm
    t_in_seq = pl.program_id(0) % tiles_per_seq
    halo = POOL_HALO
    has_prev = (t_in_seq > 0).astype(F32)
    has_next = (t_in_seq < tiles_per_seq - 1).astype(F32)
    ext_ref[0:halo, :] = up_ref[...] * has_prev
    ext_ref[halo:halo + tm, :] = u_ref[...]
    ext_ref[halo + tm:, :] = un_ref[...] * has_next

    parts = []
    for g, win in enumerate(POOL_WINDOWS):
        ls = slice(g * HEAD_DIM, (g + 1) * HEAD_DIM)
        acc = ext_ref[:, ls]
        span = 1
        while span < win:
            acc = acc + pltpu.roll(acc, span, 0)
            span *= 2
        ahead = win - win // 2 - 1
        if ahead:
            acc = pltpu.roll(acc, acc.shape[0] - ahead, 0)
        total = acc[halo:halo + tm]
        delta = total * inv_ref[:, ls] - u_ref[:, ls]
        y = jnp.dot(delta.astype(BF16), pw_ref[g], preferred_element_type=F32) * ps_ref[:, ls]
        parts.append(y.astype(BF16))
    na_w = a_ref.shape[1]
    pool_w = len(POOL_WINDOWS) * HEAD_DIM
    mixed = jnp.dot(a_ref[...], wo_ref[0:na_w, :], preferred_element_type=F32)
    mixed += jnp.dot(c_ref[...], wo_ref[na_w + pool_w:, :], preferred_element_type=F32)
    mixed += jnp.dot(jnp.concatenate(parts, axis=1), wo_ref[na_w:na_w + pool_w, :], preferred_element_type=F32)
    o_ref[...] = x_ref[...] + mod_ref[5:6, :] * mixed


def _mix_out(x2, mod, a2, u2, c2, pool_w, pool_layer, pool_scale, w_out, seq_len, *, tm=512):
    n_tok, d = x2.shape
    tm = _row_tile(seq_len, tm)
    tiles_per_mod = n_tok // mod.shape[0] // tm
    n_halo = n_tok // POOL_HALO
    hpt = tm // POOL_HALO
    row = lambda w: pl.BlockSpec((tm, w), lambda i: (i, 0))
    tiles_per_seq = seq_len // tm
    inv = jnp.asarray(_pool_inv_counts(seq_len, tm))

    def edge(i):
        t = i % tiles_per_seq
        return (0 if tiles_per_seq == 1 else jnp.where(t == 0, 0, jnp.where(t == tiles_per_seq - 1, 2, 1)), 0, 0)

    vmem = 4 * tm * d * 4 + 2 * w_out[0].size * 2 + 4 * tm * d * 4 + (6 << 20)
    return pl.pallas_call(
        functools.partial(_out_kernel, seq_len=seq_len),
        grid=(n_tok // tm,),
        in_specs=[row(d),
                  pl.BlockSpec((None, N_MOD, d), lambda i: (i // tiles_per_mod, 0, 0)),
                  row(a2.shape[1]), row(u2.shape[1]),
                  pl.BlockSpec((POOL_HALO, u2.shape[1]), lambda i: (jnp.maximum(i * hpt - 1, 0), 0)),
                  pl.BlockSpec((POOL_HALO, u2.shape[1]), lambda i: (jnp.minimum((i + 1) * hpt, n_halo - 1), 0)),
                  pl.BlockSpec((None,) + inv.shape[1:], edge),
                  row(c2.shape[1]),
                  pl.BlockSpec((None,) + pool_w.shape[1:], lambda i: (pool_layer, 0, 0, 0)),
                  pl.BlockSpec((1, u2.shape[1]), lambda i: (0, 0)),
                  pl.BlockSpec((None,) + w_out.shape[1:], lambda i: (0, 0, 0))],
        out_specs=row(d),
        out_shape=jax.ShapeDtypeStruct((n_tok, d), F32),
        scratch_shapes=[pltpu.VMEM((tm + 2 * POOL_HALO, u2.shape[1]), F32)],
        compiler_params=_params(("parallel",), vmem),
        name="pool_mix_out",
    )(x2, mod, a2, u2, u2, u2, inv, c2, pool_w, pool_scale.reshape(1, -1), w_out)


def _rope_tables(seq_len):
    n_rows = seq_len // GRID_W
    axis_dim = HEAD_DIM // 2
    inv_freq = ROPE_BASE ** (-jnp.arange(0, axis_dim, 2, dtype=F32) / axis_dim)
    ang_r = jnp.arange(n_rows, dtype=F32)[:, None] * inv_freq[None, :]
    ang_c = jnp.arange(GRID_W, dtype=F32)[:, None] * inv_freq[None, :]
    by_row = lambda t: jnp.broadcast_to(t[:, None, :], (n_rows, GRID_W, t.shape[-1]))
    by_col = lambda t: jnp.broadcast_to(t[None, :, :], (n_rows, GRID_W, t.shape[-1]))
    cos_r, sin_r = by_row(jnp.cos(ang_r)), by_row(jnp.sin(ang_r))
    cos_c, sin_c = by_col(jnp.cos(ang_c)), by_col(jnp.sin(ang_c))
    cos = jnp.concatenate([cos_r, cos_r, cos_c, cos_c], axis=-1).reshape(seq_len, HEAD_DIM)
    sin = jnp.concatenate([-sin_r, sin_r, -sin_c, sin_c], axis=-1).reshape(seq_len, HEAD_DIM)
    return cos, sin


def kernel(x, c, ctx, c_ctx, w_mod, b_mod, norm_w, ffn1_wi, ffn1_wo, ffn2_wi, ffn2_wo, w_in, w_out,
           na_qk_gain, na_rpb, pool_w, pool_scale, wa_qk_gain, wa_sink):
    b, s, d = x.shape
    cx = ctx.shape[1]
    depth = w_mod.shape[0]
    na_w = na_rpb.shape[1] * HEAD_DIM
    pw = pool_scale.shape[1]
    kv_w = (w_in.shape[2] - 3 * na_w - pw - (d - na_w - pw)) // 2
    widths = (na_w, pw, d - na_w - pw, kv_w)

    n_cond = b + 1
    cvecs = jnp.concatenate([c, c_ctx[None, :], jnp.zeros((-n_cond % SUBLANES, d), F32)], axis=0)
    mods = _modulation(cvecs, w_mod, b_mod).reshape(depth, cvecs.shape[0], N_MOD, d)
    rope = _rope_tables(s)
    wi1, wo1 = ffn1_wi[:1].astype(BF16), ffn1_wo[:1].astype(BF16)
    pwb = pool_w.astype(BF16)

    x2 = x.reshape(b * s, d)
    c2 = ctx.reshape(b * cx, d)
    seq = lambda t: t.reshape(b, s, -1)
    cseq = lambda t: t.reshape(b, cx, -1)
    for l in range(depth):
        last = l == depth - 1
        mx, mc = mods[l, :b], mods[l, b:b + 1]

        x2, (win, wout, wi2, wo2) = _ffn(x2, mx, 0, norm_w[l, 0], wi1, wo1, 0,
                                         cast=[(w_in, l), (w_out, l), (ffn2_wi, l), (ffn2_wo, l)])
        c2, _ = _ffn(c2, mc, 0, norm_w[l, 0], wi1, wo1, 0)

        nq, nk, nv, u, wq, wk, wv = _projection(x2, mx, norm_w[l, 1], win, 0, na_qk_gain[l], wa_qk_gain[l],
                                                rope, widths)
        cnq, cnk, cnv, cu, cwq, cwk, cwv = _projection(c2, mc, norm_w[l, 1], win, 0, na_qk_gain[l],
                                                       wa_qk_gain[l], None, widths)
        a_out = _neighbourhood_attention(seq(nq), seq(nk), seq(nv), cseq(cnk), cseq(cnv),
                                         _na_bias_table(na_rpb[l], s // GRID_W, cx))
        c_out = _window_attention(seq(wq), seq(wk), seq(wv), cseq(cwk), cseq(cwv), wa_sink[l])
        x2 = _mix_out(x2, mx, a_out.reshape(b * s, -1), u, c_out.reshape(b * s, -1), pwb, l, pool_scale[l],
                      wout, s)
        if not last:
            ca = _context_attention(cseq(cnq), cseq(cnk), cseq(cnv), None)
            cc = _context_attention(cseq(cwq), cseq(cwk), cseq(cwv), wa_sink[l])
            c2 = _mix_out(c2, mc, ca.reshape(b * cx, -1), cu, cc.reshape(b * cx, -1), pwb, l, pool_scale[l],
                          wout, cx)
            c2, _ = _ffn(c2, mc, 6, norm_w[l, 2], wi2, wo2, 0)
        x2, nxt = _ffn(x2, mx, 6, norm_w[l, 2], wi2, wo2, 0,
                       cast=[] if last else [(ffn1_wi, l + 1), (ffn1_wo, l + 1)])
        if not last:
            wi1, wo1 = nxt
    return x2.reshape(b, s, d)
```

```python
import functools

import numpy as np
import jax
import jax.numpy as jnp
from jax import lax
from jax.experimental import pallas as pl
from jax.experimental.pallas import tpu as pltpu

HEAD_DIM = 128
GRID_W = 64
NA_WIN_H = 8
NA_WIN_W = 16
POOL_WINDOWS = (2, 4, 8, 16)
WA_BLOCK = 128
WA_Q_BLOCKS = 8
N_MOD = 9
MACARON_WEIGHT = 0.5
ROPE_BASE = 10000.0
RMS_EPS = 1e-6
NEG_INF = -1e30
LOG2E = 1.4426950408889634
Q_SCALE = HEAD_DIM ** -0.5 * LOG2E

V7X_VMEM_BYTES = 64 * 1024 * 1024
V7X_VMEM_BUDGET = V7X_VMEM_BYTES - (8 << 20)
SUBLANES = 8
NA_Q_ROWS = NA_WIN_H // 2
NA_Q_BLOCKS = 4
POOL_HALO = 16

BF16 = jnp.bfloat16
F32 = jnp.float32


def _params(semantics, vmem_bytes):
    return pltpu.CompilerParams(dimension_semantics=semantics,
                                vmem_limit_bytes=int(min(max(vmem_bytes, 16 << 20), V7X_VMEM_BUDGET)))


def _row_tile(n_rows, want):
    t = min(want, n_rows)
    assert n_rows % t == 0, (n_rows, t)
    return t


def _mod_norm(x, mod_ref, g_ref, first):
    gain = g_ref[...] * (1.0 + mod_ref[first + 1:first + 2, :])
    return x * lax.rsqrt(jnp.mean(x * x, axis=-1, keepdims=True) + RMS_EPS) * gain + mod_ref[first:first + 1, :]


def _mod_kernel(c_ref, w_ref, b_ref, o_ref):
    c = c_ref[...]
    s = (c * jax.nn.sigmoid(c)).astype(BF16)
    o_ref[...] = jnp.dot(s, w_ref[...].astype(BF16), preferred_element_type=F32) + b_ref[...]


def _modulation(cvecs, w_mod, b_mod):
    n_layers, d, nd = w_mod.shape
    rows = cvecs.shape[0]
    tn = _row_tile(nd, 2048)
    return pl.pallas_call(
        _mod_kernel,
        grid=(n_layers, nd // tn),
        in_specs=[pl.BlockSpec((rows, d), lambda l, n: (0, 0)),
                  pl.BlockSpec((None, d, tn), lambda l, n: (l, 0, n)),
                  pl.BlockSpec((None, 1, tn), lambda l, n: (l, 0, n))],
        out_specs=pl.BlockSpec((None, rows, tn), lambda l, n: (l, 0, n)),
        out_shape=jax.ShapeDtypeStruct((n_layers, rows, nd), F32),
        compiler_params=_params(("parallel", "parallel"), 2 * d * tn * 4 + (8 << 20)),
        name="adaln_modulation",
    )(cvecs, w_mod, b_mod.reshape(n_layers, 1, nd))


def _ffn_kernel(*refs, first, tiles_per_mod, n_chunks, n_cast):
    x_hbm, mod_ref, g_ref, wa_ref, wb_ref, wo_ref = refs[:6]
    cast_in = refs[6:6 + n_cast]
    o_hbm = refs[6 + n_cast]
    cast_out = refs[7 + n_cast:7 + 2 * n_cast]
    xa_ref, xb_ref, ha_ref, hb_ref, in_sem, out_sem = refs[7 + 2 * n_cast:]
    i, f = pl.program_id(0), pl.program_id(1)
    n_tiles, nf = pl.num_programs(0), pl.num_programs(1)
    tm = xa_ref.shape[0]
    rows = tm // n_chunks
    head = tm - rows

    def x_in(tile, dst, part, slot):
        r0, n = ((0, head), (head, rows))[part]
        return pltpu.make_async_copy(x_hbm.at[pl.ds(tile * tm + r0, n), :], dst.at[pl.ds(r0, n), :],
                                     in_sem.at[slot, part])

    def x_out(tile, src, slot):
        return pltpu.make_async_copy(src, o_hbm.at[pl.ds(tile * tm, tm), :], out_sem.at[slot])

    def normed(x_rows, batch):
        return _mod_norm(x_rows, mod_ref.at[batch], g_ref, first).astype(BF16)

    def tile_step(slot, x_cur, x_nxt, h_cur, h_nxt):
        other = 1 - slot
        has_next = i + 1 < n_tiles

        if slot == 0:
            @pl.when((i == 0) & (f == 0))
            def _():
                for part in (0, 1):
                    x_in(0, x_cur, part, slot).start()
                x_nxt[...] = jnp.zeros(x_nxt.shape, x_nxt.dtype)
                for part in (0, 1):
                    x_in(0, x_cur, part, slot).wait()
                h_cur[...] = normed(x_cur[...], 0)

        @pl.when((f == 1) & (i > 0))
        def _():
            x_out(i - 1, x_nxt, other).wait()

        @pl.when((f == 1) & has_next)
        def _():
            x_in(i + 1, x_nxt, 0, other).start()

        @pl.when((f == 2) & has_next)
        def _():
            x_in(i + 1, x_nxt, 0, other).wait()
            x_in(i + 1, x_nxt, 1, other).start()

        @pl.when((f == 3) & has_next)
        def _():
            x_in(i + 1, x_nxt, 1, other).wait()

        for src, dst in zip(cast_in, cast_out):
            dst[...] = src[...].astype(dst.dtype)

        chunk = jnp.where((f >= 2) & (f < 2 + n_chunks), f - 2, n_chunks - 1)
        r0 = pl.multiple_of(chunk * rows, rows)
        next_batch = jnp.minimum(i + 1, n_tiles - 1) // tiles_per_mod
        h_nxt[pl.ds(r0, rows), :] = normed(x_nxt[pl.ds(r0, rows), :], next_batch)

        gate = MACARON_WEIGHT * mod_ref[i // tiles_per_mod, first + 2:first + 3, :]
        h = h_cur[...]
        a = jnp.dot(h, wa_ref[...], preferred_element_type=F32)
        b = jnp.dot(h, wb_ref[...], preferred_element_type=F32)
        act = (a * jax.nn.sigmoid(a) * b).astype(BF16)
        x_cur[...] += gate * jnp.dot(act, wo_ref[...], preferred_element_type=F32)

        @pl.when(f == nf - 1)
        def _():
            x_out(i, x_cur, slot).start()

        @pl.when((f == nf - 1) & (i == n_tiles - 1))
        def _():
            x_out(i, x_cur, slot).wait()

    @pl.when(i % 2 == 0)
    def _():
        tile_step(0, xa_ref, xb_ref, ha_ref, hb_ref)

    @pl.when(i % 2 == 1)
    def _():
        tile_step(1, xb_ref, xa_ref, hb_ref, ha_ref)


def _ffn(x2, mod, first, norm_g, wi, wo, layer, cast=(), *, tm=1024, tf=512):
    n_tok, d = x2.shape
    d_ff = wo.shape[1]
    tm = _row_tile(n_tok // mod.shape[0], tm)
    tf = _row_tile(d_ff, tf)
    tiles_per_mod = n_tok // mod.shape[0] // tm
    nf = d_ff // tf
    n_steps = (n_tok // tm) * nf
    n_chunks = 1 << ((nf - 2).bit_length() - 1)
    assert nf >= 4 and tm % (n_chunks * 2 * SUBLANES) == 0, (nf, tm)

    cast_in_specs, cast_out_specs, cast_shapes = [], [], []
    for arr, lyr in cast:
        _, rows, cols = arr.shape
        block = next(r for r in range(2 * SUBLANES, rows + 1, 2 * SUBLANES)
                     if rows % r == 0 and rows // r <= n_steps)
        last = rows // block - 1
        cast_in_specs.append(pl.BlockSpec(
            (None, block, cols), lambda i, f, last=last, lyr=lyr: (lyr, jnp.minimum(i * nf + f, last), 0)))
        cast_out_specs.append(pl.BlockSpec(
            (None, block, cols), lambda i, f, last=last: (0, jnp.minimum(i * nf + f, last), 0)))
        cast_shapes.append(jax.ShapeDtypeStruct((1, rows, cols), BF16))

    vmem = 2 * tm * d * 4 + 2 * tm * d * 2 + 2 * 3 * d * tf * 2 + 8 * tm * tf * 4 + (6 << 20)
    out, *cast_out = pl.pallas_call(
        functools.partial(_ffn_kernel, first=first, tiles_per_mod=tiles_per_mod, n_chunks=n_chunks,
                          n_cast=len(cast)),
        grid=(n_tok // tm, nf),
        in_specs=[pl.BlockSpec(memory_space=pl.ANY),
                  pl.BlockSpec(mod.shape, lambda i, f: (0, 0, 0)),
                  pl.BlockSpec((1, d), lambda i, f: (0, 0)),
                  pl.BlockSpec((None, d, tf), lambda i, f: (layer, 0, f)),
                  pl.BlockSpec((None, d, tf), lambda i, f: (layer, 0, nf + f)),
                  pl.BlockSpec((None, tf, d), lambda i, f: (layer, f, 0))] + cast_in_specs,
        out_specs=[pl.BlockSpec(memory_space=pl.ANY)] + cast_out_specs,
        out_shape=[jax.ShapeDtypeStruct((n_tok, d), F32)] + cast_shapes,
        scratch_shapes=[pltpu.VMEM((tm, d), F32), pltpu.VMEM((tm, d), F32),
                        pltpu.VMEM((tm, d), BF16), pltpu.VMEM((tm, d), BF16),
                        pltpu.SemaphoreType.DMA((2, 2)), pltpu.SemaphoreType.DMA((2,))],
        compiler_params=_params(("arbitrary", "arbitrary"), vmem),
        name="macaron_ffn",
    )(x2, mod, norm_g.reshape(1, d), wi, wi, wo, *[arr for arr, _ in cast])
    return out, cast_out


def _head_norm(p, gain):
    return p * lax.rsqrt(jnp.mean(p * p, axis=-1, keepdims=True) + RMS_EPS) * gain


def _rotate(y, cos, sin_signed):
    lane = lax.broadcasted_iota(jnp.int32, y.shape, 1)
    partner = jnp.where((lane & 32) == 0, pltpu.roll(y, HEAD_DIM - 32, 1), pltpu.roll(y, 32, 1))
    return y * cos + partner * sin_signed


def _proj_kernel(*refs, use_rope, widths, row_groups):
    if use_rope:
        x_ref, mod_ref, g_ref, w_ref, nag_ref, wag_ref, cos_ref, sin_ref = refs[:8]
        outs = refs[8:]
    else:
        x_ref, mod_ref, g_ref, w_ref, nag_ref, wag_ref = refs[:6]
        outs = refs[6:]
    nq_ref, nk_ref, nv_ref, u_ref, wq_ref, wk_ref, wv_ref = outs
    na_w, pool_w, wa_w, kv_w = widths
    c_nq, c_nk, c_nv, c_u = 0, na_w, 2 * na_w, 3 * na_w
    c_wq = c_u + pool_w
    c_wk = c_wq + wa_w
    c_wv = c_wk + kv_w
    half = wa_w // 2

    def project(rs):
        h = _mod_norm(x_ref[rs, :], mod_ref, g_ref, 3).astype(BF16)

        def chunk(col0, width):
            return jnp.dot(h, w_ref[:, col0:col0 + width], preferred_element_type=F32)

        def normed(p, gain, out_ref, out_col0, scale, rope):
            for j in range(p.shape[1] // HEAD_DIM):
                y = _head_norm(p[:, j * HEAD_DIM:(j + 1) * HEAD_DIM], gain)
                if rope:
                    y = _rotate(y, cos_ref[rs, :], sin_ref[rs, :])
                if scale != 1.0:
                    y = y * scale
                c0 = out_col0 + j * HEAD_DIM
                out_ref[rs, c0:c0 + HEAD_DIM] = y.astype(out_ref.dtype)

        normed(chunk(c_wq, half), wag_ref[0:1, :], wq_ref, 0, Q_SCALE, use_rope)
        normed(chunk(c_wq + half, half), wag_ref[0:1, :], wq_ref, half, Q_SCALE, use_rope)
        normed(chunk(c_wk, kv_w), wag_ref[1:2, :], wk_ref, 0, 1.0, use_rope)
        normed(chunk(c_nq, na_w), nag_ref[0:1, :], nq_ref, 0, Q_SCALE, False)
        normed(chunk(c_nk, na_w), nag_ref[1:2, :], nk_ref, 0, 1.0, False)
        nv_ref[rs, :] = chunk(c_nv, na_w).astype(BF16)
        u_ref[rs, :] = chunk(c_u, pool_w)
        wv_ref[rs, :] = chunk(c_wv, kv_w).astype(BF16)

    rows = x_ref.shape[0] // row_groups
    for r in range(row_groups):
        project(slice(r * rows, (r + 1) * rows))


def _projection(x2, mod, norm_g, w_in, layer, na_gain, wa_gain, rope, widths, *, tm=1024, group_rows=512):
    n_tok, d = x2.shape
    na_w, pool_w, wa_w, kv_w = widths
    n_in = w_in.shape[2]
    tm = _row_tile(n_tok // mod.shape[0], tm)
    row_groups = max(tm // group_rows, 1)
    tiles_per_mod = n_tok // mod.shape[0] // tm
    use_rope = rope is not None
    in_specs = [pl.BlockSpec((tm, d), lambda i: (i, 0)),
                pl.BlockSpec((None, N_MOD, d), lambda i: (i // tiles_per_mod, 0, 0)),
                pl.BlockSpec((1, d), lambda i: (0, 0)),
                pl.BlockSpec((None, d, n_in), lambda i: (layer, 0, 0), pipeline_mode=pl.Buffered(1)),
                pl.BlockSpec((2, HEAD_DIM), lambda i: (0, 0)),
                pl.BlockSpec((2, HEAD_DIM), lambda i: (0, 0))]
    args = [x2, mod, norm_g.reshape(1, d), w_in, na_gain, wa_gain]
    if use_rope:
        tiles_per_seq = rope[0].shape[0] // tm
        in_specs += [pl.BlockSpec((tm, HEAD_DIM), lambda i: (i % tiles_per_seq, 0))] * 2
        args += list(rope)
    out_widths = (na_w, na_w, na_w, pool_w, wa_w, kv_w, kv_w)
    out_dtypes = (BF16, BF16, BF16, F32, BF16, BF16, BF16)
    vmem = 2 * tm * d * 4 + tm * d * 2 + d * n_in * 2 + 4 * tm * n_in * 4 + (4 << 20)
    return pl.pallas_call(
        functools.partial(_proj_kernel, use_rope=use_rope, widths=widths, row_groups=row_groups),
        grid=(n_tok // tm,),
        in_specs=in_specs,
        out_specs=[pl.BlockSpec((tm, w), lambda i: (i, 0)) for w in out_widths],
        out_shape=[jax.ShapeDtypeStruct((n_tok, w), dt) for w, dt in zip(out_widths, out_dtypes)],
        compiler_params=_params(("parallel",), vmem),
        name="input_projection",
    )(*args)


def _softmax_weights(s, extra_logit=None):
    m = jnp.max(s, axis=-1, keepdims=True)
    if extra_logit is not None:
        m = jnp.maximum(m, extra_logit)
    p = jnp.exp2(s - m)
    denom = jnp.sum(p, axis=-1, keepdims=True)
    if extra_logit is not None:
        denom = denom + jnp.exp2(extra_logit - m)
    return p.astype(BF16), denom


def _attend(scores, values, extra_logits=None):
    extra_logits = extra_logits or [None] * len(scores)
    weights = [_softmax_weights(s, e) for s, e in zip(scores, extra_logits)]
    return [jnp.dot(p, v, preferred_element_type=F32) / denom for (p, denom), v in zip(weights, values)]


def _qk(q, k):
    return lax.dot_general(q, k, (((1,), (1,)), ((), ())), preferred_element_type=F32)


def _na_kernel(q_ref, kp_ref, kc_ref, kn_ref, vp_ref, vc_ref, vn_ref, kx_ref, vx_ref, bias_ref, o_ref):
    j, n_tiles = pl.program_id(1), pl.num_programs(1)
    blk = NA_Q_ROWS * GRID_W
    q_blocks = q_ref.shape[0] // blk
    n_heads = q_ref.shape[1] // HEAD_DIM
    heads = [slice(h * HEAD_DIM, (h + 1) * HEAD_DIM) for h in range(n_heads)]
    rows = lambda t: slice(t * blk, (t + 1) * blk)

    def window(halo_before, tile, halo_after, qb, hs):
        before = halo_before[:, hs] if qb == 0 else tile[rows(qb - 1), hs]
        after = halo_after[:, hs] if qb == q_blocks - 1 else tile[rows(qb + 1), hs]
        return [before, tile[rows(qb), hs], after]

    scores, values = [], []
    for qb in range(q_blocks):
        variant = 1
        if qb == 0:
            variant = jnp.where(j == 0, 0, variant)
        if qb == q_blocks - 1:
            variant = jnp.where(j == n_tiles - 1, 2, variant)
        for h, hs in enumerate(heads):
            k = jnp.concatenate(window(kp_ref, kc_ref, kn_ref, qb, hs) + [kx_ref[:, hs]], axis=0)
            scores.append(_qk(q_ref[rows(qb), hs], k) + bias_ref[variant, h])
            values.append(jnp.concatenate(window(vp_ref, vc_ref, vn_ref, qb, hs) + [vx_ref[:, hs]], axis=0))
    for e, out in enumerate(_attend(scores, values)):
        qb, h = divmod(e, n_heads)
        o_ref[rows(qb), heads[h]] = out.astype(o_ref.dtype)


def _na_bias_kernel(rpb_ref, o_ref, g_ref, *, n_rows):
    n_ro, n_co = 2 * NA_WIN_H - 1, 2 * NA_WIN_W - 1
    shape = (GRID_W, 2 * GRID_W)
    qc = lax.broadcasted_iota(jnp.int32, shape, 0)
    lane = lax.broadcasted_iota(jnp.int32, shape, 1)
    kc = lane & (GRID_W - 1)
    c0 = jnp.clip(qc - NA_WIN_W // 2, 0, GRID_W - NA_WIN_W)
    in_cols = (kc >= c0) & (kc < c0 + NA_WIN_W)
    col_off = kc - qc + (NA_WIN_W - 1)
    base = pl.program_id(0) * (n_ro * n_co)
    for ro in range(n_ro):
        g = jnp.zeros(shape, F32)
        for c in range(n_co):
            g = jnp.where(col_off == c, rpb_ref[base + ro * n_co + c] * LOG2E, g)
        g_ref[ro] = jnp.where(in_cols, g, NEG_INF)

    masked = jnp.full(shape, NEG_INF, F32)
    n_local = 3 * NA_Q_ROWS * GRID_W
    n_blocks = n_rows // NA_Q_ROWS
    o_ref[:, :, n_local:] = jnp.zeros(o_ref.shape[:2] + (o_ref.shape[2] - n_local,), F32)
    for variant, jb in enumerate((0, 1, n_blocks - 1)):
        for i in range(NA_Q_ROWS):
            r = jb * NA_Q_ROWS + i
            r0 = min(max(r - NA_WIN_H // 2, 0), n_rows - NA_WIN_H)
            halves = []
            for kr in range(3 * NA_Q_ROWS):
                kr_abs = (jb - 1) * NA_Q_ROWS + kr
                in_rows = 0 <= kr_abs < n_rows and r0 <= kr_abs < r0 + NA_WIN_H
                halves.append(g_ref[kr_abs - r + NA_WIN_H - 1] if in_rows else masked)
            for m in range(len(halves) // 2):
                o_ref[variant, i * GRID_W:(i + 1) * GRID_W, m * 2 * GRID_W:(m + 1) * 2 * GRID_W] = jnp.where(
                    lane < GRID_W, halves[2 * m], halves[2 * m + 1])


def _na_bias_table(rpb, n_rows, n_ctx):
    n_heads = rpb.shape[0]
    assert n_rows % NA_Q_ROWS == 0 and n_rows // NA_Q_ROWS >= 3 and (3 * NA_Q_ROWS) % 2 == 0
    shape = (3, n_heads, NA_Q_ROWS * GRID_W, 3 * NA_Q_ROWS * GRID_W + n_ctx)
    return pl.pallas_call(
        functools.partial(_na_bias_kernel, n_rows=n_rows),
        grid=(n_heads,),
        in_specs=[pl.BlockSpec(memory_space=pltpu.SMEM)],
        out_specs=pl.BlockSpec((3, None) + shape[2:], lambda h: (0, h, 0, 0)),
        out_shape=jax.ShapeDtypeStruct(shape, F32),
        scratch_shapes=[pltpu.VMEM((2 * NA_WIN_H - 1, GRID_W, 2 * GRID_W), F32)],
        compiler_params=_params(("parallel",), 24 << 20),
        name="na_bias_table",
    )(rpb.astype(F32).reshape(-1))


def _neighbourhood_attention(q, k, v, kx, vx, bias, *, q_blocks=NA_Q_BLOCKS):
    b, s, w = q.shape
    blk = NA_Q_ROWS * GRID_W
    n_halo = s // blk
    rows = q_blocks * blk
    assert s % rows == 0
    cx = kx.shape[1]
    cur = lambda bb, j: (bb, j, 0)
    prev = lambda bb, j: (bb, jnp.maximum(j * q_blocks - 1, 0), 0)
    nxt = lambda bb, j: (bb, jnp.minimum((j + 1) * q_blocks, n_halo - 1), 0)
    halo = lambda im: pl.BlockSpec((None, blk, w), im)
    tile = pl.BlockSpec((None, rows, w), cur)
    ctx = pl.BlockSpec((None, cx, w), lambda bb, j: (bb, 0, 0))
    vmem = 2 * bias.size * 4 + 8 * q_blocks * blk * bias.shape[3] * 4 + (8 << 20)
    return pl.pallas_call(
        _na_kernel,
        grid=(b, s // rows),
        in_specs=[tile, halo(prev), tile, halo(nxt), halo(prev), tile, halo(nxt), ctx, ctx,
                  pl.BlockSpec(bias.shape, lambda bb, j: (0, 0, 0, 0))],
        out_specs=tile,
        out_shape=jax.ShapeDtypeStruct((b, s, w), BF16),
        compiler_params=_params(("parallel", "parallel"), vmem),
        name="neighbourhood_attention",
    )(q, k, k, k, v, v, v, kx, vx, bias)


def _wa_band_bias(n_ctx):
    blk = WA_BLOCK
    p = np.arange(blk)[:, None]
    k = np.arange(3 * blk + n_ctx)[None, :]
    in_band = (k >= p) & (k <= p + 2 * blk)
    is_prev, is_next, is_ctx = k < blk, (k >= 2 * blk) & (k < 3 * blk), k >= 3 * blk
    tables = []
    for has_prev, has_next in ((False, True), (True, True), (True, False)):
        valid = is_ctx | (in_band & (has_prev | ~is_prev) & (has_next | ~is_next))
        tables.append(np.where(valid, 0.0, NEG_INF).astype(np.float32))
    return np.stack(tables)


def _wa_kernel(sink_ref, band_ref, q_ref, kp_ref, kc_ref, kn_ref, vp_ref, vc_ref, vn_ref, kx_ref, vx_ref,
               o_ref, *, group):
    j, n_tiles = pl.program_id(1), pl.num_programs(1)
    blk = WA_BLOCK
    q_blocks = q_ref.shape[0] // blk
    n_kv = kc_ref.shape[1] // HEAD_DIM

    def window(halo_before, tile, halo_after, qb, hs):
        rows = lambda t: slice(t * blk, (t + 1) * blk)
        before = halo_before[:, hs] if qb == 0 else tile[rows(qb - 1), hs]
        after = halo_after[:, hs] if qb == q_blocks - 1 else tile[rows(qb + 1), hs]
        return [before, tile[rows(qb), hs], after]

    scores, values, sinks = [], [], []
    for qb in range(q_blocks):
        variant = 1
        if qb == 0:
            variant = jnp.where(j == 0, 0, variant)
        if qb == q_blocks - 1:
            variant = jnp.where(j == n_tiles - 1, 2, variant)
        band = jnp.concatenate([band_ref[variant]] * group, axis=0)
        for kv in range(n_kv):
            hs = slice(kv * HEAD_DIM, (kv + 1) * HEAD_DIM)
            heads = [kv * group + g for g in range(group)]
            q = jnp.concatenate([q_ref[qb * blk:(qb + 1) * blk, h * HEAD_DIM:(h + 1) * HEAD_DIM] for h in heads],
                                axis=0)
            k = jnp.concatenate(window(kp_ref, kc_ref, kn_ref, qb, hs) + [kx_ref[:, hs]], axis=0)
            scores.append(_qk(q, k) + band)
            values.append(jnp.concatenate(window(vp_ref, vc_ref, vn_ref, qb, hs) + [vx_ref[:, hs]], axis=0))
            sinks.append(jnp.concatenate([jnp.full((blk, 1), sink_ref[h] * LOG2E, F32) for h in heads], axis=0))
    for e, out in enumerate(_attend(scores, values, sinks)):
        qb, kv = divmod(e, n_kv)
        for g in range(group):
            h = kv * group + g
            o_ref[qb * blk:(qb + 1) * blk, h * HEAD_DIM:(h + 1) * HEAD_DIM] = (
                out[g * blk:(g + 1) * blk, :].astype(o_ref.dtype))


def _window_attention(q, k, v, kx, vx, sink, *, q_blocks=WA_Q_BLOCKS):
    b, s, wq = q.shape
    wk = k.shape[2]
    group = wq // wk
    rows = q_blocks * WA_BLOCK
    n_tiles = s // rows
    n_halo = s // WA_BLOCK
    cx = kx.shape[1]
    assert s % rows == 0 and n_halo >= 2
    cur = lambda bb, j: (bb, j, 0)
    prev = lambda bb, j: (bb, jnp.maximum(j * q_blocks - 1, 0), 0)
    nxt = lambda bb, j: (bb, jnp.minimum((j + 1) * q_blocks, n_halo - 1), 0)
    halo = lambda im: pl.BlockSpec((None, WA_BLOCK, wk), im)
    tile = pl.BlockSpec((None, rows, wk), cur)
    ctx = pl.BlockSpec((None, cx, wk), lambda bb, j: (bb, 0, 0))
    band = jnp.asarray(_wa_band_bias(cx))
    vmem = 16 * q_blocks * group * WA_BLOCK * band.shape[2] * 4 + (8 << 20)
    return pl.pallas_call(
        functools.partial(_wa_kernel, group=group),
        grid=(b, n_tiles),
        in_specs=[pl.BlockSpec(memory_space=pltpu.SMEM),
                  pl.BlockSpec(band.shape, lambda bb, j: (0, 0, 0)),
                  pl.BlockSpec((None, rows, wq), cur),
                  halo(prev), tile, halo(nxt), halo(prev), tile, halo(nxt), ctx, ctx],
        out_specs=pl.BlockSpec((None, rows, wq), cur),
        out_shape=jax.ShapeDtypeStruct((b, s, wq), BF16),
        compiler_params=_params(("parallel", "parallel"), vmem),
        name="window_attention",
    )(sink, band, q, k, k, k, v, v, v, kx, vx)


def _ctx_attn_kernel(*refs, group, use_sink):
    if use_sink:
        sink_ref, q_ref, k_ref, v_ref, o_ref = refs
    else:
        q_ref, k_ref, v_ref, o_ref = refs
    c = q_ref.shape[0]
    for kv in range(k_ref.shape[1] // HEAD_DIM):
        hs = slice(kv * HEAD_DIM, (kv + 1) * HEAD_DIM)
        heads = [kv * group + g for g in range(group)]
        q = jnp.concatenate([q_ref[:, h * HEAD_DIM:(h + 1) * HEAD_DIM] for h in heads], axis=0)
        sink = None
        if use_sink:
            sink = jnp.concatenate([jnp.full((c, 1), sink_ref[h] * LOG2E, F32) for h in heads], axis=0)
        out, = _attend([_qk(q, k_ref[:, hs])], [v_ref[:, hs]], [sink])
        for g, h in enumerate(heads):
            o_ref[:, h * HEAD_DIM:(h + 1) * HEAD_DIM] = out[g * c:(g + 1) * c, :].astype(o_ref.dtype)


def _context_attention(q, k, v, sink):
    b, c, wq = q.shape
    wk = k.shape[2]
    use_sink = sink is not None
    spec = lambda w: pl.BlockSpec((None, c, w), lambda bb: (bb, 0, 0))
    in_specs = [spec(wq), spec(wk), spec(wk)]
    args = [q, k, v]
    if use_sink:
        in_specs = [pl.BlockSpec(memory_space=pltpu.SMEM)] + in_specs
        args = [sink] + args
    return pl.pallas_call(
        functools.partial(_ctx_attn_kernel, group=wq // wk, use_sink=use_sink),
        grid=(b,),
        in_specs=in_specs,
        out_specs=spec(wq),
        out_shape=jax.ShapeDtypeStruct((b, c, wq), BF16),
        compiler_params=_params(("parallel",), 24 << 20),
        name="context_attention",
    )(*args)


def _pool_inv_counts(seq_len, tm):
    starts = [0] if seq_len == tm else [0, tm, seq_len - tm]
    tables = []
    for start in starts:
        pos = start + np.arange(tm)
        cols = []
        for win in POOL_WINDOWS:
            lo = np.maximum(pos - win // 2, 0)
            hi = np.minimum(pos + (win - win // 2 - 1), seq_len - 1)
            cols.append(np.repeat((1.0 / (hi - lo + 1))[:, None], HEAD_DIM, axis=1))
        tables.append(np.concatenate(cols, axis=1))
    return np.stack(tables).astype(np.float32)


def _out_kernel(x_ref, mod_ref, a_ref, u_ref, up_ref, un_ref, inv_ref, c_ref, pw_ref, ps_ref, wo_ref, o_ref,
                ext_ref, *, seq_len):
    tm = x_ref.shape[0]
    tiles_per_seq = seq_len // tm
    t_in_seq = pl.program_id(0) % tiles_per_seq
    halo = POOL_HALO
    has_prev = (t_in_seq > 0).astype(F32)
    has_next = (t_in_seq < tiles_per_seq - 1).astype(F32)
    ext_ref[0:halo, :] = up_ref[...] * has_prev
    ext_ref[halo:halo + tm, :] = u_ref[...]
    ext_ref[halo + tm:, :] = un_ref[...] * has_next

    parts = []
    for g, win in enumerate(POOL_WINDOWS):
        ls = slice(g * HEAD_DIM, (g + 1) * HEAD_DIM)
        acc = ext_ref[:, ls]
        span = 1
        while span < win:
            acc = acc + pltpu.roll(acc, span, 0)
            span *= 2
        ahead = win - win // 2 - 1
        if ahead:
            acc = pltpu.roll(acc, acc.shape[0] - ahead, 0)
        total = acc[halo:halo + tm]
        delta = total * inv_ref[:, ls] - u_ref[:, ls]
        y = jnp.dot(delta.astype(BF16), pw_ref[g], preferred_element_type=F32) * ps_ref[:, ls]
        parts.append(y.astype(BF16))
    na_w = a_ref.shape[1]
    pool_w = len(POOL_WINDOWS) * HEAD_DIM
    mixed = jnp.dot(a_ref[...], wo_ref[0:na_w, :], preferred_element_type=F32)
    mixed += jnp.dot(c_ref[...], wo_ref[na_w + pool_w:, :], preferred_element_type=F32)
    mixed += jnp.dot(jnp.concatenate(parts, axis=1), wo_ref[na_w:na_w + pool_w, :], preferred_element_type=F32)
    o_ref[...] = x_ref[...] + mod_ref[5:6, :] * mixed


def _mix_out(x2, mod, a2, u2, c2, pool_w, pool_layer, pool_scale, w_out, seq_len, *, tm=512):
    n_tok, d = x2.shape
    tm = _row_tile(seq_len, tm)
    tiles_per_mod = n_tok // mod.shape[0] // tm
    n_halo = n_tok // POOL_HALO
    hpt = tm // POOL_HALO
    row = lambda w: pl.BlockSpec((tm, w), lambda i: (i, 0))
    tiles_per_seq = seq_len // tm
    inv = jnp.asarray(_pool_inv_counts(seq_len, tm))

    def edge(i):
        t = i % tiles_per_seq
        return (0 if tiles_per_seq == 1 else jnp.where(t == 0, 0, jnp.where(t == tiles_per_seq - 1, 2, 1)), 0, 0)

    vmem = 4 * tm * d * 4 + 2 * w_out[0].size * 2 + 4 * tm * d * 4 + (6 << 20)
    return pl.pallas_call(
        functools.partial(_out_kernel, seq_len=seq_len),
        grid=(n_tok // tm,),
        in_specs=[row(d),
                  pl.BlockSpec((None, N_MOD, d), lambda i: (i // tiles_per_mod, 0, 0)),
                  row(a2.shape[1]), row(u2.shape[1]),
                  pl.BlockSpec((POOL_HALO, u2.shape[1]), lambda i: (jnp.maximum(i * hpt - 1, 0), 0)),
                  pl.BlockSpec((POOL_HALO, u2.shape[1]), lambda i: (jnp.minimum((i + 1) * hpt, n_halo - 1), 0)),
                  pl.BlockSpec((None,) + inv.shape[1:], edge),
                  row(c2.shape[1]),
                  pl.BlockSpec((None,) + pool_w.shape[1:], lambda i: (pool_layer, 0, 0, 0)),
                  pl.BlockSpec((1, u2.shape[1]), lambda i: (0, 0)),
                  pl.BlockSpec((None,) + w_out.shape[1:], lambda i: (0, 0, 0))],
        out_specs=row(d),
        out_shape=jax.ShapeDtypeStruct((n_tok, d), F32),
        scratch_shapes=[pltpu.VMEM((tm + 2 * POOL_HALO, u2.shape[1]), F32)],
        compiler_params=_params(("parallel",), vmem),
        name="pool_mix_out",
    )(x2, mod, a2, u2, u2, u2, inv, c2, pool_w, pool_scale.reshape(1, -1), w_out)


def _rope_tables(seq_len):
    n_rows = seq_len // GRID_W
    axis_dim = HEAD_DIM // 2
    inv_freq = ROPE_BASE ** (-jnp.arange(0, axis_dim, 2, dtype=F32) / axis_dim)
    ang_r = jnp.arange(n_rows, dtype=F32)[:, None] * inv_freq[None, :]
    ang_c = jnp.arange(GRID_W, dtype=F32)[:, None] * inv_freq[None, :]
    by_row = lambda t: jnp.broadcast_to(t[:, None, :], (n_rows, GRID_W, t.shape[-1]))
    by_col = lambda t: jnp.broadcast_to(t[None, :, :], (n_rows, GRID_W, t.shape[-1]))
    cos_r, sin_r = by_row(jnp.cos(ang_r)), by_row(jnp.sin(ang_r))
    cos_c, sin_c = by_col(jnp.cos(ang_c)), by_col(jnp.sin(ang_c))
    cos = jnp.concatenate([cos_r, cos_r, cos_c, cos_c], axis=-1).reshape(seq_len, HEAD_DIM)
    sin = jnp.concatenate([-sin_r, sin_r, -sin_c, sin_c], axis=-1).reshape(seq_len, HEAD_DIM)
    return cos, sin


def kernel(x, c, ctx, c_ctx, w_mod, b_mod, norm_w, ffn1_wi, ffn1_wo, ffn2_wi, ffn2_wo, w_in, w_out,
           na_qk_gain, na_rpb, pool_w, pool_scale, wa_qk_gain, wa_sink):
    b, s, d = x.shape
    cx = ctx.shape[1]
    depth = w_mod.shape[0]
    na_w = na_rpb.shape[1] * HEAD_DIM
    pw = pool_scale.shape[1]
    kv_w = (w_in.shape[2] - 3 * na_w - pw - (d - na_w - pw)) // 2
    widths = (na_w, pw, d - na_w - pw, kv_w)

    n_cond = b + 1
    cvecs = jnp.concatenate([c, c_ctx[None, :], jnp.zeros((-n_cond % SUBLANES, d), F32)], axis=0)
    mods = _modulation(cvecs, w_mod, b_mod).reshape(depth, cvecs.shape[0], N_MOD, d)
    rope = _rope_tables(s)
    wi1, wo1 = ffn1_wi[:1].astype(BF16), ffn1_wo[:1].astype(BF16)
    pwb = pool_w.astype(BF16)

    x2 = x.reshape(b * s, d)
    c2 = ctx.reshape(b * cx, d)
    seq = lambda t: t.reshape(b, s, -1)
    cseq = lambda t: t.reshape(b, cx, -1)
    for l in range(depth):
        last = l == depth - 1
        mx, mc = mods[l, :b], mods[l, b:b + 1]

        x2, (win, wout, wi2, wo2) = _ffn(x2, mx, 0, norm_w[l, 0], wi1, wo1, 0,
                                         cast=[(w_in, l), (w_out, l), (ffn2_wi, l), (ffn2_wo, l)])
        c2, _ = _ffn(c2, mc, 0, norm_w[l, 0], wi1, wo1, 0)

        nq, nk, nv, u, wq, wk, wv = _projection(x2, mx, norm_w[l, 1], win, 0, na_qk_gain[l], wa_qk_gain[l],
                                                rope, widths)
        cnq, cnk, cnv, cu, cwq, cwk, cwv = _projection(c2, mc, norm_w[l, 1], win, 0, na_qk_gain[l],
                                                       wa_qk_gain[l], None, widths)
        a_out = _neighbourhood_attention(seq(nq), seq(nk), seq(nv), cseq(cnk), cseq(cnv),
                                         _na_bias_table(na_rpb[l], s // GRID_W, cx))
        c_out = _window_attention(seq(wq), seq(wk), seq(wv), cseq(cwk), cseq(cwv), wa_sink[l])
        x2 = _mix_out(x2, mx, a_out.reshape(b * s, -1), u, c_out.reshape(b * s, -1), pwb, l, pool_scale[l],
                      wout, s)
        if not last:
            ca = _context_attention(cseq(cnq), cseq(cnk), cseq(cnv), None)
            cc = _context_attention(cseq(cwq), cseq(cwk), cseq(cwv), wa_sink[l])
            c2 = _mix_out(c2, mc, ca.reshape(b * cx, -1), cu, cc.reshape(b * cx, -1), pwb, l, pool_scale[l],
                          wout, cx)
            c2, _ = _ffn(c2, mc, 6, norm_w[l, 2], wi2, wo2, 0)
        x2, nxt = _ffn(x2, mx, 6, norm_w[l, 2], wi2, wo2, 0,
                       cast=[] if last else [(ffn1_wi, l + 1), (ffn1_wo, l + 1)])
        if not last:
            wi1, wo1 = nxt
    return x2.reshape(b, s, d)
```
